```python
import jax
import jax.numpy as jnp
from jax import lax
import numpy as np

D_MODEL = 1024
BATCH = 4
SEQ = 4096
DEPTH = 4
DEC_BATCH = 128
DEC_SEQ = 4
PAST_LEN = 8192
PAGE_SIZE = 128

HEAD_DIM = 64
SCALE = HEAD_DIM ** -0.5
BLK = 128
N_MIXERS = 4
ROPE_THETA = 10000.0
RMS_EPS = 1e-6
D_FF = -(-(8 * D_MODEL) // (3 * 256)) * 256

A_HEADS = D_MODEL // HEAD_DIM
A_KV = 2
A_GROUP = A_HEADS // A_KV
CMP_BLOCK = 32
CMP_STRIDE = 16
CMP_PARTS = CMP_BLOCK // CMP_STRIDE
CMP_HIDDEN = 256
SEL_BLOCK = 64
N_SEL = 16
A_WINDOW = 512
FORCE_SCORE = 1e9

B_HEADS = D_MODEL // HEAD_DIM
B_KV = 2
B_GROUP = B_HEADS // B_KV
B_WINDOW = 128

C_HEADS = D_MODEL // HEAD_DIM
C_KV = 4
C_GROUP = C_HEADS // C_KV

D_GROUPS = ((128, 1), (512, 4), (2048, 16))
D_HEADS = D_MODEL // (2 * HEAD_DIM)

A_SPLIT = (A_HEADS * HEAD_DIM,) + (A_KV * HEAD_DIM,) * 6 + (3 * A_HEADS,)
B_SPLIT = (B_HEADS * HEAD_DIM, B_KV * HEAD_DIM, B_KV * HEAD_DIM)
C_SPLIT = (C_HEADS * HEAD_DIM, C_KV * HEAD_DIM, C_KV * HEAD_DIM, C_HEADS)
D_SPLIT = (D_HEADS * HEAD_DIM,) * (3 * len(D_GROUPS))

kernel_name = 'hybrid_nsa_swa_fox_dilated_step'


def rms_norm(x, g):
    xf = x.astype(jnp.float32)
    y = xf * lax.rsqrt(jnp.mean(xf * xf, axis=-1, keepdims=True) + RMS_EPS)
    return (y * g.astype(jnp.float32)).astype(x.dtype)


def rope(x, pos):
    half = x.shape[-1] // 2
    inv = ROPE_THETA ** (-jnp.arange(half, dtype=jnp.float32) / half)
    ang = pos.astype(jnp.float32)[:, None] * inv[None, :]
    cos, sin = jnp.cos(ang)[:, None, :], jnp.sin(ang)[:, None, :]
    xf = x.astype(jnp.float32)
    x1, x2 = xf[..., :half], xf[..., half:]
    return jnp.concatenate([x1 * cos - x2 * sin, x2 * cos + x1 * sin], axis=-1).astype(x.dtype)


def split_cols(y, sizes):
    return jnp.split(y, np.cumsum(sizes)[:-1].tolist(), axis=-1)


def swiglu(x, w_gate, w_up, w_down):
    return (jax.nn.silu(x @ w_gate) * (x @ w_up)) @ w_down


def masked_softmax(s, mask, sink=None):
    s = jnp.where(mask, s, -jnp.inf)
    m = jnp.max(s, axis=-1)
    if sink is not None:
        m = jnp.maximum(m, sink)
    m = jnp.where(jnp.isfinite(m), m, 0.0)
    e = jnp.exp(s - m[..., None])
    den = jnp.sum(e, axis=-1)
    if sink is not None:
        den = den + jnp.exp(sink - m)
    p = e / jnp.where(den > 0, den, 1.0)[..., None]
    return p, m + jnp.log(den)


def gqa_probs(q, k, mask, sink=None):
    s = jnp.einsum('btkgd,bskd->bkgts', q, k, preferred_element_type=jnp.float32) * SCALE
    return masked_softmax(s, mask, sink)[0]


def gqa_apply(p, v):
    return jnp.einsum('bkgts,bskd->btkgd', p.astype(v.dtype), v)


def to_blocks(x, axis):
    shp = x.shape
    x = x.reshape(shp[:axis] + (shp[axis] // BLK, BLK) + shp[axis + 1:])
    return jnp.moveaxis(x, axis, 0)


def from_blocks(x, axis):
    x = jnp.moveaxis(x, 0, axis)
    shp = x.shape
    return x.reshape(shp[:axis] + (shp[axis] * shp[axis + 1],) + shp[axis + 2:])


def band_attention(q, k, v, n_back, sink=None):
    nb = -(-n_back // BLK)
    span = (nb + 1) * BLK
    pad = ((0, 0), (nb * BLK, 0), (0, 0), (0, 0))
    kp, vp = jnp.pad(k, pad), jnp.pad(v, pad)
    rel = jnp.arange(BLK)[:, None] + nb * BLK - jnp.arange(span)[None, :]

    def per_block(args):
        n, q_blk = args
        start = n * BLK
        kb = lax.dynamic_slice_in_dim(kp, start, span, axis=1)
        vb = lax.dynamic_slice_in_dim(vp, start, span, axis=1)
        valid = (rel >= 0) & (rel <= n_back) & (start + jnp.arange(span) >= nb * BLK)[None, :]
        s = jnp.einsum('bqkgd,bskd->bkgqs', q_blk, kb, preferred_element_type=jnp.float32) * SCALE
        p, lse = masked_softmax(s, valid, sink)
        return gqa_apply(p, vb), lse

    o, lse = lax.map(per_block, (jnp.arange(q.shape[1] // BLK), to_blocks(q, 1)))
    return from_blocks(o, 1), from_blocks(jnp.moveaxis(lse, -1, 2), 1)


def window_sample(q, buf, k_new, v_new, n_back, sink=None):
    lb, t = buf.shape[1], q.shape[1]
    k = jnp.concatenate([buf[:, :, 0], k_new], axis=1)
    v = jnp.concatenate([buf[:, :, 1], v_new], axis=1)
    qpos = PAST_LEN + jnp.arange(t)
    kpos = jnp.concatenate([PAST_LEN - lb + jnp.arange(lb), qpos])
    dist = qpos[:, None] - kpos[None, :]
    return gqa_apply(gqa_probs(q, k, (dist >= 0) & (dist <= n_back), sink), v)


def nsa_project(xn, pos, w_in):
    b, t, _ = xn.shape
    q, kc, vc, ks, vs, kw, vw, g = split_cols(xn @ w_in, A_SPLIT)
    q = q.reshape(b, t, A_HEADS, HEAD_DIM)
    kv = lambda z: z.reshape(b, t, A_KV, HEAD_DIM)
    q_cmp = q.reshape(b, t, A_KV, A_GROUP, HEAD_DIM)
    q_rot = rope(q, pos).reshape(b, t, A_KV, A_GROUP, HEAD_DIM)
    gates = jax.nn.sigmoid(g.astype(jnp.float32)).reshape(b, t, A_KV, A_GROUP, 3)
    return q_cmp, q_rot, kv(kc), kv(vc), rope(kv(ks), pos), kv(vs), rope(kv(kw), pos), kv(vw), gates


def cmp_chunk_proj(x, w1):
    b, l, h, d = x.shape
    c = x.reshape(b, l // CMP_STRIDE, CMP_STRIDE, h, d).transpose(0, 1, 3, 2, 4)
    c = c.reshape(b, l // CMP_STRIDE, h, CMP_STRIDE * d)
    return jnp.einsum('bnkc,rch->rbnkh', c, w1.reshape(CMP_PARTS, CMP_STRIDE * d, -1))


def cmp_combine(parts, b1, w2):
    n = parts.shape[2] - CMP_PARTS + 1
    pre = b1
    for r in range(CMP_PARTS):
        pre = pre + parts[r, :, r:r + n]
    return jax.nn.silu(pre) @ w2


def cmp_attend(q, k_cmp, v_cmp, qpos):
    end = jnp.arange(k_cmp.shape[1]) * CMP_STRIDE + CMP_BLOCK - 1
    p = gqa_probs(q, k_cmp, end[None, :] <= qpos[:, None])
    return gqa_apply(p, v_cmp), p


def select_blocks(p_cmp, qpos, n_blocks):
    n_cmp = p_cmp.shape[-1]
    c_start = jnp.arange(n_cmp) * CMP_STRIDE
    s_start = jnp.arange(n_blocks) * SEL_BLOCK
    cover = ((c_start[:, None] < s_start[None, :] + SEL_BLOCK) & (c_start[:, None] + CMP_BLOCK > s_start[None, :])).astype(jnp.float32)
    imp = jnp.einsum('bkgti,ij->bktj', p_cmp, cover)
    cur = (qpos // SEL_BLOCK)[:, None]
    j = jnp.arange(n_blocks)[None, :]
    forced = (j == 0) | (j == cur) | (j == cur - 1)
    imp = jnp.where(j * SEL_BLOCK <= qpos[:, None], jnp.where(forced, FORCE_SCORE, imp), -jnp.inf)
    return lax.top_k(imp, min(N_SEL, n_blocks))[1]


def sel_attend(q, kg, vg, idx, qpos):
    b, kvh, t, k = idx.shape
    kg = kg.reshape(b, kvh, t, k * SEL_BLOCK, HEAD_DIM)
    vg = vg.reshape(b, kvh, t, k * SEL_BLOCK, HEAD_DIM)
    tok = (idx[..., None] * SEL_BLOCK + jnp.arange(SEL_BLOCK)).reshape(b, kvh, t, k * SEL_BLOCK)
    s = jnp.einsum('btkgd,bktsd->bkgts', q, kg, preferred_element_type=jnp.float32) * SCALE
    p, _ = masked_softmax(s, (tok <= qpos[:, None])[:, :, None])
    return jnp.einsum('bkgts,bktsd->btkgd', p.astype(vg.dtype), vg)


def nsa_merge(o_cmp, o_sel, o_win, gates, w_out):
    o = gates[..., 0, None] * o_cmp + gates[..., 1, None] * o_sel + gates[..., 2, None] * o_win
    b, t = o.shape[:2]
    return o.astype(o_cmp.dtype).reshape(b, t, -1) @ w_out


def nsa_prompt(xn, pos, w_in, phi_w1, phi_b1, phi_w2, w_out):
    b, t, _ = xn.shape
    q_cmp, q, kc, vc, ks, vs, kw, vw, gates = nsa_project(xn, pos, w_in)
    k_cmp = cmp_combine(cmp_chunk_proj(kc, phi_w1[0]), phi_b1[0], phi_w2[0])
    v_cmp = cmp_combine(cmp_chunk_proj(vc, phi_w1[1]), phi_b1[1], phi_w2[1])
    o_cmp, p_cmp = cmp_attend(q_cmp, k_cmp, v_cmp, pos)
    n_blocks = t // SEL_BLOCK
    idx = select_blocks(p_cmp, pos, n_blocks)
    kb = ks.reshape(b, n_blocks, SEL_BLOCK, A_KV, HEAD_DIM).transpose(0, 3, 1, 2, 4)
    vb = vs.reshape(b, n_blocks, SEL_BLOCK, A_KV, HEAD_DIM).transpose(0, 3, 1, 2, 4)
    bi = jnp.arange(b)[:, None, None, None]
    hi = jnp.arange(A_KV)[None, :, None, None]

    def sel_block(args):
        q_blk, idx_blk, pos_blk = args
        return sel_attend(q_blk, kb[bi, hi, idx_blk], vb[bi, hi, idx_blk], idx_blk, pos_blk)

    o_sel = from_blocks(lax.map(sel_block, (to_blocks(q, 1), to_blocks(idx, 2), pos.reshape(-1, BLK))), 1)
    o_win, _ = band_attention(q, kw, vw, A_WINDOW - 1)
    w = min(A_WINDOW, t)
    return (nsa_merge(o_cmp, o_sel, o_win, gates, w_out), jnp.stack([kc, vc], axis=2),
            jnp.stack([ks, vs], axis=2), jnp.stack([kw[:, -w:], vw[:, -w:]], axis=2))


def nsa_sample(xn, pos, cache_cmp, cache_sel, cache_win, page_table, w_in, phi_w1, phi_b1, phi_w2, w_out):
    b, t, _ = xn.shape
    q_cmp, q, kc, vc, ks, vs, kw, vw, gates = nsa_project(xn, pos, w_in)
    past = cache_cmp[page_table].reshape(b, PAST_LEN, 2, A_KV, HEAD_DIM)
    n_new = (t // CMP_STRIDE) * CMP_STRIDE

    def compress(j, new):
        parts = jnp.concatenate([cmp_chunk_proj(past[:, :, j], phi_w1[j]),
                                 cmp_chunk_proj(new[:, :n_new], phi_w1[j])], axis=2)
        return cmp_combine(parts, phi_b1[j], phi_w2[j])

    o_cmp, p_cmp = cmp_attend(q_cmp, compress(0, kc), compress(1, vc), pos)
    n_pb = PAST_LEN // SEL_BLOCK
    n_nb = -(-t // SEL_BLOCK)
    idx = select_blocks(p_cmp, pos, n_pb + n_nb)
    bpp = PAGE_SIZE // SEL_BLOCK
    bi = jnp.arange(b)[:, None, None, None]
    hi = jnp.arange(A_KV)[None, :, None, None]
    ip = jnp.minimum(idx, n_pb - 1)
    phys = page_table[bi, ip // bpp] * bpp + ip % bpp
    g_past = cache_sel.reshape(-1, SEL_BLOCK, 2, A_KV, HEAD_DIM)[phys, :, :, hi]
    new_kv = jnp.stack([ks, vs], axis=2)
    new_blocks = jnp.pad(new_kv, ((0, 0), (0, n_nb * SEL_BLOCK - t), (0, 0), (0, 0), (0, 0)))
    new_blocks = new_blocks.reshape(b, n_nb, SEL_BLOCK, 2, A_KV, HEAD_DIM)
    g_new = new_blocks[bi, jnp.clip(idx - n_pb, 0, n_nb - 1), :, :, hi]
    g = jnp.where((idx < n_pb)[..., None, None, None], g_past, g_new)
    o_sel = sel_attend(q, g[..., 0, :], g[..., 1, :], idx, pos)
    o_win = window_sample(q, cache_win, kw, vw, A_WINDOW - 1)
    return (nsa_merge(o_cmp, o_sel, o_win, gates, w_out), jnp.stack([kc, vc], axis=2),
            new_kv, jnp.stack([kw, vw], axis=2))


def swa_project(xn, pos, w_in):
    b, t, _ = xn.shape
    q, k, v = split_cols(xn @ w_in, B_SPLIT)
    q = rope(q.reshape(b, t, B_HEADS, HEAD_DIM), pos).reshape(b, t, B_KV, B_GROUP, HEAD_DIM)
    k = rope(k.reshape(b, t, B_KV, HEAD_DIM), pos)
    return q, k, v.reshape(b, t, B_KV, HEAD_DIM)


def swa_prompt(xn, pos, w_in, sinks, w_out):
    b, t, _ = xn.shape
    q, k, v = swa_project(xn, pos, w_in)
    o, _ = band_attention(q, k, v, B_WINDOW - 1, sinks.astype(jnp.float32).reshape(B_KV, B_GROUP, 1))
    w = min(B_WINDOW, t)
    return o.reshape(b, t, -1) @ w_out, jnp.stack([k[:, -w:], v[:, -w:]], axis=2)


def swa_sample(xn, pos, cache_win, w_in, sinks, w_out):
    b, t, _ = xn.shape
    q, k, v = swa_project(xn, pos, w_in)
    o = window_sample(q, cache_win, k, v, B_WINDOW - 1, sinks.astype(jnp.float32).reshape(B_KV, B_GROUP, 1))
    return o.reshape(b, t, -1) @ w_out, jnp.stack([k, v], axis=2)


def fox_project(xn, w_in, b_f):
    b, t, _ = xn.shape
    q, k, v, f = split_cols(xn @ w_in, C_SPLIT)
    logf = jax.nn.log_sigmoid(f.astype(jnp.float32) + b_f.astype(jnp.float32))
    return (q.reshape(b, t, C_KV, C_GROUP, HEAD_DIM), k.reshape(b, t, C_KV, HEAD_DIM),
            v.reshape(b, t, C_KV, HEAD_DIM), logf)


def per_head(z):
    b, l, _ = z.shape
    return z.reshape(b, l, C_KV, C_GROUP).transpose(0, 2, 3, 1)


def fox_prompt(xn, pos, w_in, b_f, w_out):
    b, t, _ = xn.shape
    q, k, v, logf = fox_project(xn, w_in, b_f)
    cum = jnp.cumsum(logf, axis=1)
    cum_k = per_head(cum)[:, :, :, None, :]

    def q_block(args):
        q_blk, cum_blk, pos_blk = args
        s = jnp.einsum('btkgd,bskd->bkgts', q_blk, k, preferred_element_type=jnp.float32) * SCALE
        s = s + per_head(cum_blk)[..., None] - cum_k
        p, _ = masked_softmax(s, pos[None, :] <= pos_blk[:, None])
        return gqa_apply(p, v)

    o = from_blocks(lax.map(q_block, (to_blocks(q, 1), to_blocks(cum, 1), pos.reshape(-1, BLK))), 1)
    return o.reshape(b, t, -1) @ w_out, jnp.stack([k, v], axis=2), logf


def fox_sample(xn, cache_kv, cache_logf, page_table, w_in, b_f, w_out):
    b, t, _ = xn.shape
    q, k, v, logf = fox_project(xn, w_in, b_f)
    past = cache_kv[page_table].reshape(b, PAST_LEN, 2, C_KV, HEAD_DIM)
    lf_past = cache_logf[page_table].reshape(b, PAST_LEN, C_HEADS).astype(jnp.float32)
    r_past = lax.cumsum(lf_past, axis=1, reverse=True) - lf_past
    f_new = jnp.cumsum(logf, axis=1)
    fq = per_head(f_new)[..., :, None]
    s_past = jnp.einsum('btkgd,bskd->bkgts', q, past[:, :, 0], preferred_element_type=jnp.float32) * SCALE
    s_past = s_past + per_head(r_past)[..., None, :] + fq
    s_new = jnp.einsum('btkgd,bskd->bkgts', q, k, preferred_element_type=jnp.float32) * SCALE
    s_new = s_new + fq - per_head(f_new)[..., None, :]
    causal = jnp.arange(t)[None, :] <= jnp.arange(t)[:, None]
    mask = jnp.concatenate([jnp.ones((t, PAST_LEN), dtype=bool), causal], axis=1)
    p, _ = masked_softmax(jnp.concatenate([s_past, s_new], axis=-1), mask)
    o = gqa_apply(p[..., :PAST_LEN], past[:, :, 1]) + gqa_apply(p[..., PAST_LEN:], v)
    return o.reshape(b, t, -1) @ w_out, jnp.stack([k, v], axis=2), logf


def dil_project(xn, pos, w_in):
    b, t, _ = xn.shape
    parts = [z.reshape(b, t, D_HEADS, HEAD_DIM) for z in split_cols(xn @ w_in, D_SPLIT)]
    return [(rope(parts[3 * g], pos), rope(parts[3 * g + 1], pos), parts[3 * g + 2])
            for g in range(len(D_GROUPS))]


def dil_merge(outs, lses, w_out):
    wts = jax.nn.softmax(jnp.stack(lses), axis=0)
    o = sum(wts[g][..., None] * outs[g] for g in range(len(outs)))
    b, t = o.shape[:2]
    return o.astype(outs[0].dtype).reshape(b, t, -1) @ w_out


def dil_prompt(xn, pos, w_in, w_out):
    b, t, _ = xn.shape
    outs, lses, states = [], [], []
    for (window, dil), (q, k, v) in zip(D_GROUPS, dil_project(xn, pos, w_in)):
        n_sub = t // dil
        n_pad = -(-n_sub // BLK) * BLK - n_sub

        def to_sub(z):
            z = z.reshape(b, n_sub, dil, D_HEADS, HEAD_DIM).transpose(0, 2, 1, 3, 4)
            z = z.reshape(b * dil, n_sub, D_HEADS, HEAD_DIM)
            return jnp.pad(z, ((0, 0), (0, n_pad), (0, 0), (0, 0)))

        o, lse = band_attention(to_sub(q)[:, :, :, None], to_sub(k), to_sub(v), window // dil)
        o = o[:, :n_sub, :, 0].reshape(b, dil, n_sub, D_HEADS, HEAD_DIM).transpose(0, 2, 1, 3, 4)
        outs.append(o.reshape(b, t, D_HEADS, HEAD_DIM))
        lse = lse[:, :n_sub, :, 0].reshape(b, dil, n_sub, D_HEADS).transpose(0, 2, 1, 3)
        lses.append(lse.reshape(b, t, D_HEADS))
        w = min(window, t)
        states.append(jnp.stack([k[:, -w:], v[:, -w:]], axis=2))
    return dil_merge(outs, lses, w_out), states


def dil_sample(xn, pos, caches, w_in, w_out):
    b, t, _ = xn.shape
    outs, lses, states = [], [], []
    new_i = jnp.arange(t)
    for (window, dil), (q, k, v), buf in zip(D_GROUPS, dil_project(xn, pos, w_in), caches):
        lb = buf.shape[1]
        idx = lb + new_i[:, None] - jnp.arange(window // dil + 1)[None, :] * dil
        new_kv = jnp.stack([k, v], axis=2)
        g = jnp.where((idx < lb)[None, :, :, None, None, None], buf[:, jnp.clip(idx, 0, lb - 1)],
                      new_kv[:, jnp.clip(idx - lb, 0, t - 1)])
        s = jnp.einsum('bthd,btjhd->bhtj', q, g[:, :, :, 0], preferred_element_type=jnp.float32) * SCALE
        p, lse = masked_softmax(s, (idx >= 0)[None, None])
        outs.append(jnp.einsum('bhtj,btjhd->bthd', p.astype(v.dtype), g[:, :, :, 1]))
        lses.append(lse.transpose(0, 2, 1))
        states.append(new_kv)
    return dil_merge(outs, lses, w_out), states


def setup_inputs(seed: int = 0) -> dict:
    key = jax.random.key(seed)
    keys = iter(jax.random.split(key, 48))

    def nrm(shape, scale):
        return jax.random.normal(next(keys), shape, jnp.float32) * scale

    hd = HEAD_DIM
    n_pages = PAST_LEN // PAGE_SIZE
    n_pool = (5 * DEC_BATCH * n_pages + 3) // 4

    def win(w, h):
        return nrm((DEC_BATCH, min(w, PAST_LEN), 2, h, hd), 1.0)

    perm = jax.random.permutation(next(keys), n_pool)
    page_table = perm[:DEC_BATCH * n_pages].reshape(DEC_BATCH, n_pages).astype(jnp.int32)
    return {
        'x_prompt': nrm((BATCH, SEQ, D_MODEL), 1.0),
        'x_sample': nrm((DEC_BATCH, DEC_SEQ, D_MODEL), 1.0),
        'cache_a_cmp': nrm((n_pool, PAGE_SIZE, 2, A_KV, hd), 1.0),
        'cache_a_sel': nrm((n_pool, PAGE_SIZE, 2, A_KV, hd), 1.0),
        'cache_a_win': win(A_WINDOW, A_KV),
        'cache_b_win': win(B_WINDOW, B_KV),
        'cache_c_kv': nrm((n_pool, PAGE_SIZE, 2, C_KV, hd), 1.0),
        'cache_c_logf': jax.nn.log_sigmoid(2.5 + nrm((n_pool, PAGE_SIZE, C_HEADS), 1.0)),
        'cache_d1': win(D_GROUPS[0][0], D_HEADS),
        'cache_d2': win(D_GROUPS[1][0], D_HEADS),
        'cache_d3': win(D_GROUPS[2][0], D_HEADS),
        'page_table': page_table,
        'norm_mix': 1.0 + nrm((DEPTH, D_MODEL), 0.05),
        'norm_ffn': 1.0 + nrm((DEPTH, D_MODEL), 0.05),
        'norm_final': 1.0 + nrm((D_MODEL,), 0.05),
        'a_w_in': nrm((D_MODEL, sum(A_SPLIT)), D_MODEL ** -0.5),
        'a_phi_w1': nrm((2, CMP_BLOCK * hd, CMP_HIDDEN), (CMP_BLOCK * hd) ** -0.5),
        'a_phi_b1': nrm((2, CMP_HIDDEN), 0.02),
        'a_phi_w2': nrm((2, CMP_HIDDEN, hd), CMP_HIDDEN ** -0.5),
        'a_w_out': nrm((A_HEADS * hd, D_MODEL), (A_HEADS * hd) ** -0.5),
        'b_w_in': nrm((D_MODEL, sum(B_SPLIT)), D_MODEL ** -0.5),
        'b_sinks': nrm((B_HEADS,), 1.0),
        'b_w_out': nrm((B_HEADS * hd, D_MODEL), (B_HEADS * hd) ** -0.5),
        'c_w_in': nrm((D_MODEL, sum(C_SPLIT)), D_MODEL ** -0.5),
        'c_b_f': jax.random.uniform(next(keys), (C_HEADS,), jnp.float32, 1.0, 4.0),
        'c_w_out': nrm((C_HEADS * hd, D_MODEL), (C_HEADS * hd) ** -0.5),
        'd_w_in': nrm((D_MODEL, sum(D_SPLIT)), D_MODEL ** -0.5),
        'd_w_out': nrm((D_HEADS * hd, D_MODEL), (D_HEADS * hd) ** -0.5),
        'ffn_w_gate': nrm((DEPTH, D_MODEL, D_FF), D_MODEL ** -0.5),
        'ffn_w_up': nrm((DEPTH, D_MODEL, D_FF), D_MODEL ** -0.5),
        'ffn_w_down': nrm((DEPTH, D_FF, D_MODEL), D_FF ** -0.5),
    }


def reference(x_prompt, x_sample, cache_a_cmp, cache_a_sel, cache_a_win, cache_b_win, cache_c_kv,
              cache_c_logf, cache_d1, cache_d2, cache_d3, page_table, norm_mix, norm_ffn, norm_final,
              a_w_in, a_phi_w1, a_phi_b1, a_phi_w2, a_w_out, b_w_in, b_sinks, b_w_out,
              c_w_in, c_b_f, c_w_out, d_w_in, d_w_out, ffn_w_gate, ffn_w_up, ffn_w_down):
    hp, hs = x_prompt, x_sample
    pos_p = jnp.arange(hp.shape[1])
    pos_s = PAST_LEN + jnp.arange(hs.shape[1])
    for layer in range(DEPTH):
        xp = rms_norm(hp, norm_mix[layer])
        xs = rms_norm(hs, norm_mix[layer])
        kind = layer % N_MIXERS
        if kind == 0:
            yp, a_cmp_p, a_sel_p, a_win_p = nsa_prompt(xp, pos_p, a_w_in, a_phi_w1, a_phi_b1, a_phi_w2, a_w_out)
            ys, a_cmp_s, a_sel_s, a_win_s = nsa_sample(xs, pos_s, cache_a_cmp, cache_a_sel, cache_a_win, page_table,
                                                       a_w_in, a_phi_w1, a_phi_b1, a_phi_w2, a_w_out)
        elif kind == 1:
            yp, b_win_p = swa_prompt(xp, pos_p, b_w_in, b_sinks, b_w_out)
            ys, b_win_s = swa_sample(xs, pos_s, cache_b_win, b_w_in, b_sinks, b_w_out)
        elif kind == 2:
            yp, c_kv_p, c_logf_p = fox_prompt(xp, pos_p, c_w_in, c_b_f, c_w_out)
            ys, c_kv_s, c_logf_s = fox_sample(xs, cache_c_kv, cache_c_logf, page_table, c_w_in, c_b_f, c_w_out)
        else:
            yp, (d1_p, d2_p, d3_p) = dil_prompt(xp, pos_p, d_w_in, d_w_out)
            ys, (d1_s, d2_s, d3_s) = dil_sample(xs, pos_s, (cache_d1, cache_d2, cache_d3), d_w_in, d_w_out)
        hp = hp + yp
        hs = hs + ys
        hp = hp + swiglu(rms_norm(hp, norm_ffn[layer]), ffn_w_gate[layer], ffn_w_up[layer], ffn_w_down[layer])
        hs = hs + swiglu(rms_norm(hs, norm_ffn[layer]), ffn_w_gate[layer], ffn_w_up[layer], ffn_w_down[layer])
    y_prompt = rms_norm(hp, norm_final)
    y_sample = rms_norm(hs, norm_final)
    return (y_prompt, y_sample, a_cmp_p, a_cmp_s, a_sel_p, a_sel_s, a_win_p, a_win_s, b_win_p, b_win_s,
            c_kv_p, c_kv_s, c_logf_p, c_logf_s, d1_p, d1_s, d2_p, d2_s, d3_p, d3_s)
```

```python
import functools

import numpy as np
import jax
import jax.numpy as jnp
from jax import lax
from jax.experimental import pallas as pl
from jax.experimental.pallas import tpu as pltpu

F32 = jnp.float32
BF16 = jnp.bfloat16

HD = 64
SCALE = HD ** -0.5
RMS_EPS = 1e-6
ROPE_THETA = 10000.0
PAGE = 128
CMP_STRIDE = 16
CMP_BLOCK = 32
CMP_HIDDEN = 256
SEL_BLOCK = 64
N_SEL = 16
A_WINDOW = 512
B_WINDOW = 128
D_GROUPS = ((128, 1), (512, 4), (2048, 16))
FORCE_SCORE = 1e9
NEG = -1e30
DEAD = -3e38
LANE = 128
VMEM_LIMIT = 52 * 1024 * 1024


def _params(sem, vmem=VMEM_LIMIT):
    return pltpu.CompilerParams(dimension_semantics=sem, vmem_limit_bytes=vmem)


def _nt(a, b):
    return lax.dot_general(a, b, (((1,), (1,)), ((), ())), preferred_element_type=F32)


def _dot(a, b):
    return jnp.dot(a, b, preferred_element_type=F32)


def _split3(x):
    hi = x.astype(BF16)
    r1 = x - hi.astype(F32)
    mid = r1.astype(BF16)
    lo = (r1 - mid.astype(F32)).astype(BF16)
    return hi, mid, lo


def _rms(x, g):
    var = jnp.mean(x * x, axis=-1, keepdims=True)
    return x * lax.rsqrt(var + RMS_EPS) * g


def _rope_chunk(x, cos2, sin2):
    lane = lax.broadcasted_iota(jnp.int32, x.shape, 1)
    first = (lane % HD) < (HD // 2)
    swapped = jnp.where(first, pltpu.roll(x, LANE - HD // 2, axis=1), pltpu.roll(x, HD // 2, axis=1))
    return x * cos2 + swapped * sin2


def _rope_tables(pos):
    half = HD // 2
    inv = ROPE_THETA ** (-jnp.arange(half, dtype=F32) / half)
    ang = pos.astype(F32)[:, None] * inv[None, :]
    c, s = jnp.cos(ang), jnp.sin(ang)
    return jnp.tile(c, (1, 4)), jnp.tile(jnp.concatenate([-s, s], axis=1), (1, 2))


def _norm_proj_kernel(x_ref, g_ref, w_ref, cos_ref, sin_ref, *out_refs, segs):
    xn = _rms(x_ref[...], g_ref[...]).astype(BF16)
    outs = list(out_refs)
    for c0, c1, r0, r1, dup in segs:
        acc = _dot(xn, w_ref[:, c0:c1])
        if dup:
            outs.pop(0)[...] = acc
        o_ref = outs.pop(0)
        if r1 <= r0:
            o_ref[...] = acc
            continue
        if r0 > 0:
            o_ref[:, 0:r0] = acc[:, 0:r0]
        for a in range(r0, r1, LANE):
            o_ref[:, a:a + LANE] = _rope_chunk(acc[:, a:a + LANE], cos_ref[...], sin_ref[...])
        if r1 < c1 - c0:
            o_ref[:, r1:] = acc[:, r1:]


def norm_proj(x, g, w, cos2, sin2, segs, tm):
    n, d = x.shape
    tm = min(tm, n)
    out_shape, out_specs = [], []
    for c0, c1, _, _, dup in segs:
        for _ in range(2 if dup else 1):
            out_shape.append(jax.ShapeDtypeStruct((n, c1 - c0), F32))
            out_specs.append(pl.BlockSpec((tm, c1 - c0), lambda i: (i, 0)))
    return pl.pallas_call(
        functools.partial(_norm_proj_kernel, segs=tuple(segs)),
        grid=(n // tm,),
        in_specs=[pl.BlockSpec((tm, d), lambda i: (i, 0)),
                  pl.BlockSpec((1, d), lambda i: (0, 0)),
                  pl.BlockSpec(w.shape, lambda i: (0, 0)),
                  pl.BlockSpec((tm, LANE), lambda i: (i % (cos2.shape[0] // tm), 0)),
                  pl.BlockSpec((tm, LANE), lambda i: (i % (sin2.shape[0] // tm), 0))],
        out_specs=out_specs, out_shape=out_shape,
        compiler_params=_params(("parallel",)), name="norm_proj",
    )(x, g.reshape(1, d), w, cos2, sin2)


def _out_proj_kernel(o_ref, h_ref, w_ref, out_ref):
    out_ref[...] = h_ref[...] + _dot(o_ref[...].astype(BF16), w_ref[...])


def out_proj(o, h, w, tm=512):
    n, k = o.shape
    d = w.shape[1]
    tm = min(tm, n)
    return pl.pallas_call(
        _out_proj_kernel, grid=(n // tm,),
        in_specs=[pl.BlockSpec((tm, k), lambda i: (i, 0)), pl.BlockSpec((tm, d), lambda i: (i, 0)),
                  pl.BlockSpec(w.shape, lambda i: (0, 0))],
        out_specs=pl.BlockSpec((tm, d), lambda i: (i, 0)),
        out_shape=jax.ShapeDtypeStruct((n, d), F32),
        compiler_params=_params(("parallel",)), name="out_proj",
    )(o, h, w)


def _merge_a_kernel(oc_ref, os_ref, ow_ref, g_ref, h_ref, w_ref, out_ref, o_sc, *, heads):
    gate = jax.nn.sigmoid(g_ref[...])
    for h in range(heads):
        sl = slice(h * HD, (h + 1) * HD)
        o = (gate[:, 3 * h:3 * h + 1] * oc_ref[:, sl] + gate[:, 3 * h + 1:3 * h + 2] * os_ref[:, sl]
             + gate[:, 3 * h + 2:3 * h + 3] * ow_ref[:, sl])
        o_sc[:, sl] = o
    out_ref[...] = h_ref[...] + _dot(o_sc[...].astype(BF16), w_ref[...])


def merge_a(o_cmp, o_sel, o_win, gates, h, w, tm=512):
    n, k = o_cmp.shape
    d = w.shape[1]
    tm = min(tm, n)
    row = lambda c: pl.BlockSpec((tm, c), lambda i: (i, 0))
    return pl.pallas_call(
        functools.partial(_merge_a_kernel, heads=k // HD), grid=(n // tm,),
        in_specs=[row(k), row(k), row(k), row(gates.shape[1]), row(d), pl.BlockSpec(w.shape, lambda i: (0, 0))],
        out_specs=row(d), out_shape=jax.ShapeDtypeStruct((n, d), F32),
        scratch_shapes=[pltpu.VMEM((tm, k), F32)],
        compiler_params=_params(("parallel",)), name="merge_a",
    )(o_cmp, o_sel, o_win, gates, h, w)


def _merge_d_kernel(o1, o2, o3, l1, l2, l3, h_ref, w_ref, out_ref):
    a, b, c = l1[...], l2[...], l3[...]
    m = jnp.maximum(jnp.maximum(a, b), c)
    ea, eb, ec = jnp.exp(a - m), jnp.exp(b - m), jnp.exp(c - m)
    den = ea + eb + ec
    o = (ea / den) * o1[...] + (eb / den) * o2[...] + (ec / den) * o3[...]
    out_ref[...] = h_ref[...] + _dot(o.astype(BF16), w_ref[...])


def merge_d(outs, lses, h, w, tm=512):
    n, k = outs[0].shape
    d = w.shape[1]
    tm = min(tm, n)
    row = lambda c: pl.BlockSpec((tm, c), lambda i: (i, 0))
    return pl.pallas_call(
        _merge_d_kernel, grid=(n // tm,),
        in_specs=[row(k)] * 6 + [row(d), pl.BlockSpec(w.shape, lambda i: (0, 0))],
        out_specs=row(d), out_shape=jax.ShapeDtypeStruct((n, d), F32),
        compiler_params=_params(("parallel",)), name="merge_d",
    )(*outs, *lses, h, w)


def _ffn_kernel(h_ref, g_ref, wg_ref, wu_ref, wd_ref, out_ref, xn_sc, acc_sc):
    j = pl.program_id(1)

    @pl.when(j == 0)
    def _():
        x = h_ref[...]
        xn_sc[...] = _rms(x, g_ref[...]).astype(BF16)
        acc_sc[...] = x

    xn = xn_sc[...]
    a = _dot(xn, wg_ref[...])
    u = _dot(xn, wu_ref[...])
    mid = (a * jax.nn.sigmoid(a) * u).astype(BF16)
    acc_sc[...] += _dot(mid, wd_ref[...])

    @pl.when(j == pl.num_programs(1) - 1)
    def _():
        out_ref[...] = acc_sc[...]


def ffn(h, g, wg, wu, wd, tm=512, n_f=2):
    n, d = h.shape
    f = wg.shape[1]
    tm = min(tm, n)
    tf = f // n_f
    return pl.pallas_call(
        _ffn_kernel, grid=(n // tm, n_f),
        in_specs=[pl.BlockSpec((tm, d), lambda i, j: (i, 0)), pl.BlockSpec((1, d), lambda i, j: (0, 0)),
                  pl.BlockSpec((d, tf), lambda i, j: (0, j)), pl.BlockSpec((d, tf), lambda i, j: (0, j)),
                  pl.BlockSpec((tf, d), lambda i, j: (j, 0))],
        out_specs=pl.BlockSpec((tm, d), lambda i, j: (i, 0)),
        out_shape=jax.ShapeDtypeStruct((n, d), F32),
        scratch_shapes=[pltpu.VMEM((tm, d), BF16), pltpu.VMEM((tm, d), F32)],
        compiler_params=_params(("parallel", "arbitrary")), name="ffn",
    )(h, g.reshape(1, d), wg, wu, wd)


def _final_norm_kernel(x_ref, g_ref, o_ref):
    o_ref[...] = _rms(x_ref[...], g_ref[...])


def final_norm(x, g, tm=1024):
    n, d = x.shape
    tm = min(tm, n)
    return pl.pallas_call(
        _final_norm_kernel, grid=(n // tm,),
        in_specs=[pl.BlockSpec((tm, d), lambda i: (i, 0)), pl.BlockSpec((1, d), lambda i: (0, 0))],
        out_specs=pl.BlockSpec((tm, d), lambda i: (i, 0)),
        out_shape=jax.ShapeDtypeStruct((n, d), F32),
        compiler_params=_params(("parallel",)), name="final_norm",
    )(x, g.reshape(1, d))


def _fox_gate_kernel(f_ref, b_ref, lf_ref, cum_ref):
    x = f_ref[0] + b_ref[...]
    lf = jnp.minimum(x, 0.0) - jnp.log1p(jnp.exp(-jnp.abs(x)))
    lf_ref[0] = lf
    n = x.shape[1]
    lane = lax.broadcasted_iota(jnp.int32, x.shape, 1)
    c = lf
    s = 1
    while s < n:
        c = c + jnp.where(lane >= s, pltpu.roll(c, s, axis=1), 0.0)
        s *= 2
    cum_ref[0] = c


def fox_gate(f_t, b_f):
    b, hh, n = f_t.shape
    spec = pl.BlockSpec((1, hh, n), lambda i: (i, 0, 0))
    return pl.pallas_call(
        _fox_gate_kernel, grid=(b,),
        in_specs=[spec, pl.BlockSpec((hh, 1), lambda i: (0, 0))],
        out_specs=[spec, spec], out_shape=[jax.ShapeDtypeStruct(f_t.shape, F32)] * 2,
        compiler_params=_params(("parallel",)), name="fox_gate",
    )(f_t, b_f.reshape(hh, 1))


def _band_kernel(*refs, t, nb, n_back, hq, group, with_sink, with_lse):
    refs = list(refs)
    q_ref, k_ref, v_ref = refs[:3]
    pos = 3
    sink_ref = None
    if with_sink:
        sink_ref = refs[pos]
        pos += 1
    o_ref = refs[pos]
    pos += 1
    lse_ref = None
    if with_lse:
        lse_ref = refs[pos]
        pos += 1
    m_sc, l_sc, acc_sc = refs[pos:pos + 3]
    qi, j = pl.program_id(2), pl.program_id(3)
    kt = qi - nb + j

    @pl.when(j == 0)
    def _():
        m_sc[...] = jnp.full(m_sc.shape, NEG, F32)
        l_sc[...] = jnp.zeros(l_sc.shape, F32)
        acc_sc[...] = jnp.zeros(acc_sc.shape, F32)

    @pl.when(kt >= 0)
    def _():
        dist = (qi * t + lax.broadcasted_iota(jnp.int32, (t, t), 0)) - (kt * t + lax.broadcasted_iota(jnp.int32, (t, t), 1))
        mask = (dist >= 0) & (dist <= n_back)
        for h in range(hq):
            kv = h // group
            q = q_ref[0, :, h * HD:(h + 1) * HD].astype(BF16)
            k = k_ref[0, :, kv * HD:(kv + 1) * HD].astype(BF16)
            v = v_ref[0, :, kv * HD:(kv + 1) * HD].astype(BF16)
            s = jnp.where(mask, _nt(q, k) * SCALE, NEG)
            m_prev = m_sc[h]
            m_new = jnp.maximum(m_prev, jnp.max(s, axis=1, keepdims=True))
            alpha = jnp.exp(m_prev - m_new)
            p = jnp.where(mask, jnp.exp(s - m_new), 0.0)
            l_sc[h] = alpha * l_sc[h] + jnp.sum(p, axis=1, keepdims=True)
            acc_sc[h] = alpha * acc_sc[h] + _dot(p.astype(BF16), v)
            m_sc[h] = m_new

    @pl.when(j == nb)
    def _():
        for h in range(hq):
            m, l, acc = m_sc[h], l_sc[h], acc_sc[h]
            if with_sink:
                sk = sink_ref[:, h:h + 1]
                m_f = jnp.maximum(m, sk)
                scale_old = jnp.exp(m - m_f)
                l = l * scale_old + jnp.exp(sk - m_f)
                acc = acc * scale_old
                m = m_f
            o_ref[0, :, h * HD:(h + 1) * HD] = acc / l
            if with_lse:
                lse_ref[0, :, h * HD:(h + 1) * HD] = jnp.broadcast_to(m + jnp.log(l), (t, HD))


def band_attention(q, kv, n_back, t, cq, ck, n_col, k_blk, v_blk, sink=None, with_lse=False):
    b, l, _ = q.shape
    nb = -(-n_back // t)
    hq, hkv = cq // HD, ck // HD
    kmap = lambda blk: (lambda bi, c, qi, j: (bi, jnp.maximum(qi - nb + j, 0), blk(c)))
    in_specs = [pl.BlockSpec((1, t, cq), lambda bi, c, qi, j: (bi, qi, c)),
                pl.BlockSpec((1, t, ck), kmap(k_blk)), pl.BlockSpec((1, t, ck), kmap(v_blk))]
    args = [q, kv, kv]
    if sink is not None:
        in_specs.append(pl.BlockSpec((1, hq), lambda bi, c, qi, j: (0, 0)))
        args.append(sink.reshape(1, hq))
    o_spec = pl.BlockSpec((1, t, cq), lambda bi, c, qi, j: (bi, qi, c))
    o_shape = jax.ShapeDtypeStruct(q.shape, F32)
    res = pl.pallas_call(
        functools.partial(_band_kernel, t=t, nb=nb, n_back=n_back, hq=hq, group=hq // hkv,
                          with_sink=sink is not None, with_lse=with_lse),
        grid=(b, n_col, l // t, nb + 1),
        in_specs=in_specs,
        out_specs=[o_spec, o_spec] if with_lse else o_spec,
        out_shape=[o_shape, o_shape] if with_lse else o_shape,
        scratch_shapes=[pltpu.VMEM((hq, t, 1), F32), pltpu.VMEM((hq, t, 1), F32), pltpu.VMEM((hq, t, HD), F32)],
        compiler_params=_params(("parallel", "parallel", "parallel", "arbitrary")), name="band_attention",
    )(*args)
    return res


def _flash_kernel(*refs, t, hq, group, mode):
    if mode == "fox":
        q_ref, k_ref, v_ref, cq_ref, ck_ref, o_ref, m_sc, l_sc, acc_sc = refs
    else:
        q_ref, k_ref, v_ref, sel_ref, o_ref, m_sc, l_sc, acc_sc = refs
    qi, kj = pl.program_id(2), pl.program_id(3)

    @pl.when(kj == 0)
    def _():
        m_sc[...] = jnp.full(m_sc.shape, NEG, F32)
        l_sc[...] = jnp.zeros(l_sc.shape, F32)
        acc_sc[...] = jnp.zeros(acc_sc.shape, F32)

    @pl.when(kj <= qi)
    def _():
        causal = (qi * t + lax.broadcasted_iota(jnp.int32, (t, t), 0)) >= (kj * t + lax.broadcasted_iota(jnp.int32, (t, t), 1))
        masks = {}
        if mode == "sel":
            selb = sel_ref[0].astype(BF16)
            width = selb.shape[1]
            n_blk = width // (hq // group)
            row = lax.broadcasted_iota(jnp.int32, (width, t), 0)
            key_blk = (kj * t + lax.broadcasted_iota(jnp.int32, (width, t), 1)) // SEL_BLOCK
            for kv in range(hq // group):
                expand = jnp.where(row == kv * n_blk + key_blk, 1.0, 0.0).astype(BF16)
                masks[kv] = causal & (_dot(selb, expand) > 0.5)
        for h in range(hq):
            kv = h // group
            q = q_ref[0, :, h * HD:(h + 1) * HD].astype(BF16)
            k = k_ref[0, :, kv * HD:(kv + 1) * HD].astype(BF16)
            v = v_ref[0, :, kv * HD:(kv + 1) * HD].astype(BF16)
            s = _nt(q, k) * SCALE
            if mode == "fox":
                s = s + cq_ref[0, 0, :, h:h + 1] - ck_ref[0, 0, h:h + 1, :]
                mask = causal
            else:
                mask = masks[kv]
            s = jnp.where(mask, s, NEG)
            m_prev = m_sc[h]
            m_new = jnp.maximum(m_prev, jnp.max(s, axis=1, keepdims=True))
            alpha = jnp.exp(m_prev - m_new)
            p = jnp.where(mask, jnp.exp(s - m_new), 0.0)
            l_sc[h] = alpha * l_sc[h] + jnp.sum(p, axis=1, keepdims=True)
            acc_sc[h] = alpha * acc_sc[h] + _dot(p.astype(BF16), v)
            m_sc[h] = m_new

    @pl.when(kj == pl.num_programs(3) - 1)
    def _():
        for h in range(hq):
            o_ref[0, :, h * HD:(h + 1) * HD] = acc_sc[h] / l_sc[h]


def flash_attention(q, kv, t, cq, n_col, n_kv_blk, mode, extra):
    b, l, _ = q.shape
    hq = cq // HD
    nq = l // t
    kmap = lambda off: (lambda bi, c, qi, kj: (bi, jnp.minimum(kj, qi), off + c))
    in_specs = [pl.BlockSpec((1, t, cq), lambda bi, c, qi, kj: (bi, qi, c)),
                pl.BlockSpec((1, t, LANE), kmap(0)), pl.BlockSpec((1, t, LANE), kmap(n_kv_blk))]
    if mode == "fox":
        in_specs += [pl.BlockSpec((1, 1, t, hq), lambda bi, c, qi, kj: (bi, c, qi, 0)),
                     pl.BlockSpec((1, 1, hq, t), lambda bi, c, qi, kj: (bi, c, 0, jnp.minimum(kj, qi)))]
    else:
        in_specs += [pl.BlockSpec((1, t, extra[0].shape[2]), lambda bi, c, qi, kj: (bi, qi, 0))]
    return pl.pallas_call(
        functools.partial(_flash_kernel, t=t, hq=hq, group=hq // 2, mode=mode),
        grid=(b, n_col, nq, nq), in_specs=in_specs,
        out_specs=pl.BlockSpec((1, t, cq), lambda bi, c, qi, kj: (bi, qi, c)),
        out_shape=jax.ShapeDtypeStruct(q.shape, F32),
        scratch_shapes=[pltpu.VMEM((hq, t, 1), F32), pltpu.VMEM((hq, t, 1), F32), pltpu.VMEM((hq, t, HD), F32)],
        compiler_params=_params(("parallel", "parallel", "parallel", "arbitrary")), name="flash_" + mode,
    )(q, kv, kv, *extra)


def _compress_kernel(*refs, n_src, rows, paged):
    refs = refs[1:] if paged else refs
    src = refs[:n_src]
    w1_ref, b1_ref, w2_ref, out_ref, x_sc = refs[n_src:]
    total = n_src * rows
    half = LANE * CMP_STRIDE
    for j in range(2):
        for i in range(n_src):
            for s in range(CMP_STRIDE):
                x_sc[i * rows:(i + 1) * rows, s * LANE:(s + 1) * LANE] = (
                    src[i][0, :, s * 2 * LANE + j * LANE:s * 2 * LANE + (j + 1) * LANE])
        parts = _dot(x_sc[...].astype(BF16), w1_ref[j])
        first, second = parts[:, :half // 4], parts[:, half // 4:]
        pre = b1_ref[j] + first + pltpu.roll(second, total - 1, axis=0)
        act = pre * jax.nn.sigmoid(pre)
        out_ref[0, :, j * LANE:(j + 1) * LANE] = _dot(act.astype(BF16), w2_ref[j])


def compress(x, page_table, w1p, b1p, w2p):
    if page_table is None:
        b, rows, width = x.shape
        n_src = 1
        grid_spec = pl.GridSpec(
            grid=(b,),
            in_specs=[pl.BlockSpec((1, rows, width), lambda i: (i, 0, 0)),
                      pl.BlockSpec(w1p.shape, lambda i: (0, 0, 0)), pl.BlockSpec(b1p.shape, lambda i: (0, 0, 0)),
                      pl.BlockSpec(w2p.shape, lambda i: (0, 0, 0))],
            out_specs=pl.BlockSpec((1, rows, 2 * LANE), lambda i: (i, 0, 0)),
            scratch_shapes=[pltpu.VMEM((rows, width // 2), F32)])
        args = (x, w1p, b1p, w2p)
    else:
        b, n_src = page_table.shape
        _, rows, width = x.shape
        page_spec = lambda p: pl.BlockSpec((1, rows, width), lambda i, pt: (pt[i, p], 0, 0))
        grid_spec = pltpu.PrefetchScalarGridSpec(
            num_scalar_prefetch=1, grid=(b,),
            in_specs=[page_spec(p) for p in range(n_src)] + [
                pl.BlockSpec(w1p.shape, lambda i, pt: (0, 0, 0)), pl.BlockSpec(b1p.shape, lambda i, pt: (0, 0, 0)),
                pl.BlockSpec(w2p.shape, lambda i, pt: (0, 0, 0))],
            out_specs=pl.BlockSpec((1, n_src * rows, 2 * LANE), lambda i, pt: (i, 0, 0)),
            scratch_shapes=[pltpu.VMEM((n_src * rows, width // 2), F32)])
        args = (page_table,) + (x,) * n_src + (w1p, b1p, w2p)
    return pl.pallas_call(
        functools.partial(_compress_kernel, n_src=n_src, rows=rows, paged=page_table is not None),
        grid_spec=grid_spec,
        out_shape=jax.ShapeDtypeStruct((b, n_src * rows, 2 * LANE), F32),
        compiler_params=_params(("arbitrary",)), name="compress",
    )(*args)


def _compress_weights(phi_w1, phi_b1, phi_w2, n_kv):
    hidden = phi_w1.shape[-1]
    parts = CMP_BLOCK // CMP_STRIDE
    eye = jnp.eye(n_kv, dtype=F32)
    w1 = phi_w1.reshape(2, parts, CMP_STRIDE, HD, hidden)
    w1p = jnp.einsum("jrsdh,kl->jskdrlh", w1, eye).reshape(2, CMP_STRIDE * n_kv * HD, parts * n_kv * hidden)
    b1p = jnp.tile(phi_b1[:, None, :], (1, 1, n_kv))
    w2p = jnp.einsum("jhd,kl->jkhld", phi_w2, eye).reshape(2, n_kv * hidden, n_kv * HD)
    return w1p.astype(BF16), b1p, w2p.astype(BF16)


def _top_blocks(imp, n_take):
    lane = lax.broadcasted_iota(jnp.int32, imp.shape, 1).astype(F32)
    sel = jnp.zeros(imp.shape, F32)
    cur = imp
    for _ in range(n_take):
        mx = jnp.max(cur, axis=1, keepdims=True)
        first = jnp.min(jnp.where(cur == mx, lane, 1e9), axis=1, keepdims=True)
        pick = lane == first
        sel = jnp.where(pick, 1.0, sel)
        cur = jnp.where(pick, DEAD, cur)
    return sel


def _cmp_prompt_kernel(q_ref, kvc_ref, cover_ref, o_ref, sel_ref, *, t, n_cmp, n_kv, group, n_blk):
    qi = pl.program_id(1)
    rows = kvc_ref.shape[1]
    qpos = qi * t + lax.broadcasted_iota(jnp.int32, (t, 1), 0)
    ci = lax.broadcasted_iota(jnp.int32, (1, rows), 1)
    valid = (ci * CMP_STRIDE + CMP_BLOCK - 1 <= qpos) & (ci < n_cmp)
    blk = lax.broadcasted_iota(jnp.int32, (1, n_blk), 1)
    cur = qpos // SEL_BLOCK
    forced = (blk == 0) | (blk == cur) | (blk == cur - 1)
    cover = cover_ref[...]
    for kv in range(n_kv):
        kc = kvc_ref[0, :, kv * HD:(kv + 1) * HD].astype(BF16)
        vc = kvc_ref[0, :, (n_kv + kv) * HD:(n_kv + kv + 1) * HD].astype(BF16)
        imp = jnp.zeros((t, n_blk), F32)
        for g in range(group):
            h = kv * group + g
            s = jnp.where(valid, _nt(q_ref[0, :, h * HD:(h + 1) * HD].astype(BF16), kc) * SCALE, NEG)
            m = jnp.max(s, axis=1, keepdims=True)
            e = jnp.where(valid, jnp.exp(s - m), 0.0)
            den = jnp.sum(e, axis=1, keepdims=True)
            p = (e / jnp.where(den > 0, den, 1.0)).astype(BF16)
            o_ref[0, :, h * HD:(h + 1) * HD] = _dot(p, vc)
            imp = imp + _dot(p, cover)
        imp = jnp.where(blk * SEL_BLOCK <= qpos, jnp.where(forced, FORCE_SCORE, imp), NEG)
        sel_ref[0, :, kv * n_blk:(kv + 1) * n_blk] = _top_blocks(imp, min(N_SEL, n_blk))


def _cover_matrix(rows, n_cmp, n_blk, width):
    c_start = np.arange(rows)[:, None] * CMP_STRIDE
    s_start = np.arange(width)[None, :] * SEL_BLOCK
    cov = (c_start < s_start + SEL_BLOCK) & (c_start + CMP_BLOCK > s_start)
    cov &= (np.arange(rows)[:, None] < n_cmp) & (np.arange(width)[None, :] < n_blk)
    return jnp.asarray(cov, dtype=BF16)


def cmp_attend_prompt(q, kvc, t, n_kv):
    b, l, cq = q.shape
    rows = kvc.shape[1]
    n_blk = l // SEL_BLOCK
    cover = _cover_matrix(rows, rows - 1, n_blk, n_blk)
    return pl.pallas_call(
        functools.partial(_cmp_prompt_kernel, t=t, n_cmp=rows - 1, n_kv=n_kv, group=cq // HD // n_kv, n_blk=n_blk),
        grid=(b, l // t),
        in_specs=[pl.BlockSpec((1, t, cq), lambda bi, qi: (bi, qi, 0)),
                  pl.BlockSpec((1, rows, kvc.shape[2]), lambda bi, qi: (bi, 0, 0)),
                  pl.BlockSpec(cover.shape, lambda bi, qi: (0, 0))],
        out_specs=[pl.BlockSpec((1, t, cq), lambda bi, qi: (bi, qi, 0)),
                   pl.BlockSpec((1, t, n_kv * n_blk), lambda bi, qi: (bi, qi, 0))],
        out_shape=[jax.ShapeDtypeStruct(q.shape, F32), jax.ShapeDtypeStruct((b, l, n_kv * n_blk), F32)],
        compiler_params=_params(("parallel", "parallel")), name="cmp_attend_prompt",
    )(q, kvc, cover)


def _diag_heads(x, rows_kv, n_kv):
    out = jnp.zeros((x.shape[0], HD), F32)
    for kv in range(n_kv):
        out = out + jnp.where(rows_kv == kv, x[:, kv * HD:(kv + 1) * HD], 0.0)
    return out


def _cmp_sample_kernel(q_ref, kvc_ref, cover_ref, o_ref, sel_ref, *, n_tok, heads, n_kv, n_cmp, past_len, n_blk):
    rows = kvc_ref.shape[1]
    r_all = n_tok * heads
    c = n_kv * HD
    width = cover_ref.shape[1]
    group = heads // n_kv
    ridx = lax.broadcasted_iota(jnp.int32, (r_all, 1), 0)
    qpos = past_len + ridx // heads
    ci = lax.broadcasted_iota(jnp.int32, (1, rows), 1)
    valid = (ci * CMP_STRIDE + CMP_BLOCK - 1 <= qpos) & (ci < n_cmp)
    kc = kvc_ref[0, :, 0:c].astype(BF16)
    vc = kvc_ref[0, :, c:2 * c].astype(BF16)
    s = jnp.where(valid, _nt(q_ref[0].astype(BF16), kc) * SCALE, NEG)
    m = jnp.max(s, axis=1, keepdims=True)
    e = jnp.where(valid, jnp.exp(s - m), 0.0)
    den = jnp.sum(e, axis=1, keepdims=True)
    p = (e / jnp.where(den > 0, den, 1.0)).astype(BF16)
    o_ref[0] = _diag_heads(_dot(p, vc), (ridx % heads) // group, n_kv)
    imp = jnp.sum(_dot(p, cover_ref[...]).reshape(r_all // group, group, width), axis=1)
    g_rows = r_all // group
    gpos = past_len + lax.broadcasted_iota(jnp.int32, (g_rows, 1), 0) // n_kv
    blk = lax.broadcasted_iota(jnp.int32, (1, width), 1)
    cur = gpos // SEL_BLOCK
    forced = (blk == 0) | (blk == cur) | (blk == cur - 1)
    imp = jnp.where(blk * SEL_BLOCK <= gpos, jnp.where(forced, FORCE_SCORE, imp), NEG)
    imp = jnp.where(blk < n_blk, imp, DEAD)
    sel_ref[0] = _top_blocks(imp, min(N_SEL, n_blk))


def cmp_attend_sample(qbd, kvc, n_tok, heads, n_kv, past_len):
    b, r_all, c = qbd.shape
    rows = kvc.shape[1]
    n_blk = past_len // SEL_BLOCK + -(-n_tok // SEL_BLOCK)
    width = -(-n_blk // LANE) * LANE
    cover = _cover_matrix(rows, rows - 1, n_blk, width)
    g_rows = r_all // (heads // n_kv)
    return pl.pallas_call(
        functools.partial(_cmp_sample_kernel, n_tok=n_tok, heads=heads, n_kv=n_kv, n_cmp=rows - 1,
                          past_len=past_len, n_blk=n_blk),
        grid=(b,),
        in_specs=[pl.BlockSpec((1, r_all, c), lambda i: (i, 0, 0)),
                  pl.BlockSpec((1, rows, 2 * c), lambda i: (i, 0, 0)),
                  pl.BlockSpec(cover.shape, lambda i: (0, 0))],
        out_specs=[pl.BlockSpec((1, r_all, HD), lambda i: (i, 0, 0)),
                   pl.BlockSpec((1, g_rows, width), lambda i: (i, 0, 0))],
        out_shape=[jax.ShapeDtypeStruct((b, r_all, HD), F32), jax.ShapeDtypeStruct((b, g_rows, width), F32)],
        compiler_params=_params(("parallel",)), name="cmp_attend_sample",
    )(qbd, kvc, cover)


def _dec_init(m_sc, l_sc, acc_sc):
    m_sc[...] = jnp.full(m_sc.shape, NEG, F32)
    l_sc[...] = jnp.zeros(l_sc.shape, F32)
    acc_sc[...] = jnp.zeros(acc_sc.shape, F32)


def _dec_update(qb, k, v, bias, mask, m_sc, l_sc, acc_sc):
    s = _nt(qb, k) * SCALE
    if bias is not None:
        s = s + bias
    if mask is not None:
        s = jnp.where(mask, s, NEG)
    m_prev = m_sc[...]
    m_new = jnp.maximum(m_prev, jnp.max(s, axis=1, keepdims=True))
    alpha = jnp.exp(m_prev - m_new)
    p = jnp.exp(s - m_new)
    if mask is not None:
        p = jnp.where(mask, p, 0.0)
    l_sc[...] = alpha * l_sc[...] + jnp.sum(p, axis=1, keepdims=True)
    acc_sc[...] = alpha * acc_sc[...] + _dot(p.astype(BF16), v)
    m_sc[...] = m_new


def _dec_new_tokens(q, knew, vnew, biases, masks, m_sc, l_sc, acc_sc):
    n_new = knew.shape[0]
    scores = []
    for u in range(n_new):
        s = jnp.sum(q * knew[u:u + 1, :], axis=1, keepdims=True) * SCALE
        if biases is not None:
            s = s + biases[u]
        scores.append(jnp.where(masks[u], s, NEG))
    m_prev = m_sc[...]
    m_new = m_prev
    for s in scores:
        m_new = jnp.maximum(m_new, s)
    alpha = jnp.exp(m_prev - m_new)
    l = alpha * l_sc[...]
    acc = alpha * acc_sc[...]
    for u in range(n_new):
        p = jnp.where(masks[u], jnp.exp(scores[u] - m_new), 0.0)
        l = l + p
        acc = acc + p * vnew[u:u + 1, :]
    l_sc[...] = l
    acc_sc[...] = acc
    m_sc[...] = m_new


def _token_cumsum(x, n_tok, heads):
    parts, run = [], None
    for u in range(n_tok):
        seg = x[u * heads:(u + 1) * heads]
        run = seg if run is None else run + seg
        parts.append(run)
    return jnp.concatenate(parts, axis=0), parts


def _dec_window_kernel(*refs, n_tok, heads, n_kv, lb, n_back, dil, chunk, with_sink, with_lse):
    refs = list(refs)
    q_ref, cache_ref, new_ref = refs[:3]
    pos = 3
    sink_ref = None
    if with_sink:
        sink_ref = refs[pos]
        pos += 1
    o_ref = refs[pos]
    pos += 1
    lse_ref = None
    if with_lse:
        lse_ref = refs[pos]
        pos += 1
    m_sc, l_sc, acc_sc = refs[pos:pos + 3]
    r_all = n_tok * heads
    c = n_kv * HD
    group = heads // n_kv
    ridx = lax.broadcasted_iota(jnp.int32, (r_all, 1), 0)
    tok = ridx // heads
    _dec_init(m_sc, l_sc, acc_sc)
    q = q_ref[0]
    qb = q.astype(BF16)

    def ok(dist):
        good = (dist >= 0) & (dist <= n_back)
        if dil > 1:
            good = good & (dist % dil == 0)
        return good

    for c0 in range(0, lb, chunk):
        k = cache_ref[0, c0:c0 + chunk, 0:c].astype(BF16)
        v = cache_ref[0, c0:c0 + chunk, c:2 * c].astype(BF16)
        dist = (lb + tok) - (c0 + lax.broadcasted_iota(jnp.int32, (1, chunk), 1))
        _dec_update(qb, k, v, None, ok(dist), m_sc, l_sc, acc_sc)
    new = new_ref[0]
    _dec_new_tokens(q, new[:, 0:c], new[:, c:2 * c], None, [ok(tok - u) for u in range(n_tok)], m_sc, l_sc, acc_sc)
    m, l, acc = m_sc[...], l_sc[...], acc_sc[...]
    if with_sink:
        sk = sink_ref[...]
        m_f = jnp.maximum(m, sk)
        scale_old = jnp.exp(m - m_f)
        l = l * scale_old + jnp.exp(sk - m_f)
        acc = acc * scale_old
        m = m_f
    o_ref[0] = _diag_heads(acc / l, (ridx % heads) // group, n_kv)
    if with_lse:
        lse_ref[0] = jnp.broadcast_to(m + jnp.log(l), (r_all, HD))


def dec_window(qbd, cache, new, n_tok, heads, n_kv, n_back, dil, sink_rows=None, with_lse=False):
    b, r_all, c = qbd.shape
    lb = cache.shape[1]
    in_specs = [pl.BlockSpec((1, r_all, c), lambda i: (i, 0, 0)),
                pl.BlockSpec((1, lb, 2 * c), lambda i: (i, 0, 0)),
                pl.BlockSpec((1, n_tok, 2 * c), lambda i: (i, 0, 0))]
    args = [qbd, cache, new]
    if sink_rows is not None:
        in_specs.append(pl.BlockSpec((r_all, 1), lambda i: (0, 0)))
        args.append(sink_rows)
    o_spec = pl.BlockSpec((1, r_all, HD), lambda i: (i, 0, 0))
    o_shape = jax.ShapeDtypeStruct((b, r_all, HD), F32)
    return pl.pallas_call(
        functools.partial(_dec_window_kernel, n_tok=n_tok, heads=heads, n_kv=n_kv, lb=lb, n_back=n_back, dil=dil,
                          chunk=min(lb, 512), with_sink=sink_rows is not None, with_lse=with_lse),
        grid=(b,), in_specs=in_specs,
        out_specs=[o_spec, o_spec] if with_lse else o_spec,
        out_shape=[o_shape, o_shape] if with_lse else o_shape,
        scratch_shapes=[pltpu.VMEM((r_all, 1), F32), pltpu.VMEM((r_all, 1), F32), pltpu.VMEM((r_all, c), F32)],
        compiler_params=_params(("parallel",)), name="dec_window",
    )(*args)


def _dec_paged_kernel(*refs, n_pg, n_tok, heads, n_kv, mode):
    pt_ref = refs[0]
    del pt_ref
    pages = refs[1:1 + n_pg]
    pos = 1 + n_pg
    lf_pages = None
    if mode == "fox":
        lf_pages = refs[pos:pos + n_pg]
        pos += n_pg
    q_ref, new_ref, x_ref = refs[pos:pos + 3]
    pos += 3
    if mode == "fox":
        tri_ref = refs[pos]
        pos += 1
    o_ref = refs[pos]
    pos += 1
    k_sc, v_sc, m_sc, l_sc, acc_sc = refs[pos:pos + 5]
    pos += 5
    if mode == "fox":
        c_sc, carry_sc = refs[pos:pos + 2]
    gi = pl.program_id(1)
    r_all = n_tok * heads
    c = n_kv * HD
    group = heads // n_kv
    tk = n_pg * PAGE
    ridx = lax.broadcasted_iota(jnp.int32, (r_all, 1), 0)
    tok = ridx // heads
    head = ridx % heads

    @pl.when(gi == 0)
    def _():
        _dec_init(m_sc, l_sc, acc_sc)
        if mode == "fox":
            carry_sc[...] = jnp.zeros(carry_sc.shape, F32)

    for i in range(n_pg):
        k_sc[i * PAGE:(i + 1) * PAGE, :] = pages[i][0, :, 0:c].astype(BF16)
        v_sc[i * PAGE:(i + 1) * PAGE, :] = pages[i][0, :, c:2 * c].astype(BF16)
    q = q_ref[0]
    qb = q.astype(BF16)
    hsel = jnp.where(head == lax.broadcasted_iota(jnp.int32, (r_all, heads), 1), 1.0, 0.0)

    if mode == "fox":
        tri = tri_ref[...]
        carry = carry_sc[...]
        for i in range(n_pg):
            hi, mid, lo = _split3(lf_pages[i][0])
            cs = _dot(tri, hi) + _dot(tri, mid) + _dot(tri, lo) + carry
            c_sc[i * PAGE:(i + 1) * PAGE, :] = cs
            carry = cs[PAGE - 1:PAGE, :]
        carry_sc[...] = carry
        hi, mid, lo = _split3(c_sc[...])
        hb = hsel.astype(BF16)
        bias = _token_cumsum(x_ref[0], n_tok, heads)[0] - (_nt(hb, hi) + _nt(hb, mid) + _nt(hb, lo))
        _dec_update(qb, k_sc[...], v_sc[...], bias, None, m_sc, l_sc, acc_sc)
    else:
        flags = x_ref[0].astype(BF16)
        width = flags.shape[1]
        key_blk = (gi * tk + lax.broadcasted_iota(jnp.int32, (width, tk), 1)) // SEL_BLOCK
        expand = jnp.where(lax.broadcasted_iota(jnp.int32, (width, tk), 0) == key_blk, 1.0, 0.0).astype(BF16)
        _dec_update(qb, k_sc[...], v_sc[...], None, _dot(flags, expand) > 0.5, m_sc, l_sc, acc_sc)

    @pl.when(gi == pl.num_programs(1) - 1)
    def _():
        new = new_ref[0]
        causal = [tok >= u for u in range(n_tok)]
        if mode == "fox":
            fq, per_tok = _token_cumsum(x_ref[0], n_tok, heads)
            total = jnp.sum(hsel * carry_sc[...], axis=1, keepdims=True)
            biases = [fq - jnp.concatenate([per_tok[u]] * n_tok, axis=0) - total for u in range(n_tok)]
            _dec_new_tokens(q, new[:, 0:c], new[:, c:2 * c], biases, causal, m_sc, l_sc, acc_sc)
        else:
            n_past_blk = (pl.num_programs(1) * tk) // SEL_BLOCK
            flags = x_ref[0]
            lane = lax.broadcasted_iota(jnp.int32, flags.shape, 1)
            new_on = jnp.sum(jnp.where(lane == n_past_blk, flags, 0.0), axis=1, keepdims=True) > 0.5
            _dec_new_tokens(q, new[:, 0:c], new[:, c:2 * c], None, [m & new_on for m in causal], m_sc, l_sc, acc_sc)
        o_ref[0] = _diag_heads(acc_sc[...] / l_sc[...], head // group, n_kv)


def dec_paged(qbd, pool, page_table, new, x, n_tok, heads, n_kv, mode, n_pg, lf_pool=None):
    b, r_all, c = qbd.shape
    n_pages = page_table.shape[1]
    page_spec = lambda p, w: pl.BlockSpec((1, PAGE, w), lambda i, g, pt: (pt[i, g * n_pg + p], 0, 0))
    in_specs = [page_spec(p, 2 * c) for p in range(n_pg)]
    args = [pool] * n_pg
    if mode == "fox":
        in_specs += [page_spec(p, heads) for p in range(n_pg)]
        args += [lf_pool] * n_pg
    in_specs += [pl.BlockSpec((1, r_all, c), lambda i, g, pt: (i, 0, 0)),
                 pl.BlockSpec((1, n_tok, 2 * c), lambda i, g, pt: (i, 0, 0)),
                 pl.BlockSpec((1, r_all, x.shape[2]), lambda i, g, pt: (i, 0, 0))]
    args += [qbd, new, x]
    scratch = [pltpu.VMEM((n_pg * PAGE, c), BF16), pltpu.VMEM((n_pg * PAGE, c), BF16),
               pltpu.VMEM((r_all, 1), F32), pltpu.VMEM((r_all, 1), F32), pltpu.VMEM((r_all, c), F32)]
    if mode == "fox":
        tri = jnp.asarray(np.tril(np.ones((PAGE, PAGE), np.float32)), dtype=BF16)
        in_specs.append(pl.BlockSpec((PAGE, PAGE), lambda i, g, pt: (0, 0)))
        args.append(tri)
        scratch += [pltpu.VMEM((n_pg * PAGE, heads), F32), pltpu.VMEM((1, heads), F32)]
    return pl.pallas_call(
        functools.partial(_dec_paged_kernel, n_pg=n_pg, n_tok=n_tok, heads=heads, n_kv=n_kv, mode=mode),
        grid_spec=pltpu.PrefetchScalarGridSpec(
            num_scalar_prefetch=1, grid=(b, n_pages // n_pg), in_specs=in_specs,
            out_specs=pl.BlockSpec((1, r_all, HD), lambda i, g, pt: (i, 0, 0)), scratch_shapes=scratch),
        out_shape=jax.ShapeDtypeStruct((b, r_all, HD), F32),
        compiler_params=_params(("parallel", "arbitrary")), name="dec_paged_" + mode,
    )(page_table, *args)


def _block_diag_q(q, n_tok, heads, n_kv):
    b = q.shape[0] // n_tok
    own = jnp.asarray(np.arange(heads)[:, None] // (heads // n_kv) == np.arange(n_kv)[None, :], dtype=F32)
    qb = q.reshape(b, n_tok, heads, 1, HD) * own[None, None, :, :, None]
    return qb.reshape(b, n_tok * heads, n_kv * HD)


def _mixer_a(hp, hs, g, w_in, w_out, phi, caches, page_table, dims, ropes):
    bsz, seq, dec_b, dec_t, past_len = dims
    (cos_p, sin_p), (cos_s, sin_s) = ropes
    cache_cmp, cache_sel, cache_win = caches
    heads, n_kv = 16, 2
    segs = [(0, 1024, 0, 1024, True), (1024, 1280, 0, 0, False), (1280, 1536, 0, 128, False),
            (1536, 1792, 0, 128, False), (1792, 1840, 0, 0, False)]
    w1p, b1p, w2p = phi
    q_cmp, q_rot, cmp_raw, sel_kv, win_kv, gates = norm_proj(hp, g, w_in, cos_p, sin_p, segs, 512)
    n = bsz * seq
    kvc = compress(cmp_raw.reshape(bsz, seq // CMP_STRIDE, CMP_STRIDE * 4 * HD), None, w1p, b1p, w2p)
    o_cmp, flags = cmp_attend_prompt(q_cmp.reshape(bsz, seq, -1), kvc, 256, n_kv)
    q3 = q_rot.reshape(bsz, seq, -1)
    o_sel = flash_attention(q3, sel_kv.reshape(bsz, seq, -1), 512, 1024, 1, 1, "sel", (flags,))
    o_win = band_attention(q3, win_kv.reshape(bsz, seq, -1), A_WINDOW - 1, 256, 1024, LANE, 1,
                           lambda c: 0, lambda c: 1)
    hp = merge_a(o_cmp.reshape(n, -1), o_sel.reshape(n, -1), o_win.reshape(n, -1), gates, hp, w_out)
    kv5 = lambda z, b, t: z.reshape(b, t, 2, n_kv, HD)
    w = min(A_WINDOW, seq)
    outs_p = (kv5(cmp_raw, bsz, seq), kv5(sel_kv, bsz, seq), kv5(win_kv, bsz, seq)[:, -w:])
    q_cmp, q_rot, cmp_raw, sel_kv, win_kv, gates = norm_proj(hs, g, w_in, cos_s, sin_s, segs, 512)
    ns = dec_b * dec_t
    pool = cache_cmp.reshape(cache_cmp.shape[0], PAGE // CMP_STRIDE, CMP_STRIDE * 4 * HD)
    kvc = compress(pool, page_table, w1p, b1p, w2p)
    o_cmp, flags = cmp_attend_sample(_block_diag_q(q_cmp, dec_t, heads, n_kv), kvc, dec_t, heads, n_kv, past_len)
    qbd = _block_diag_q(q_rot, dec_t, heads, n_kv)
    flag_rows = jnp.repeat(flags.reshape(dec_b, dec_t, n_kv, 1, -1), heads // n_kv, axis=3).reshape(dec_b, dec_t * heads, -1)
    o_sel = dec_paged(qbd, cache_sel.reshape(cache_sel.shape[0], PAGE, -1), page_table,
                      sel_kv.reshape(dec_b, dec_t, -1), flag_rows, dec_t, heads, n_kv, "sel", 16)
    o_win = dec_window(qbd, cache_win.reshape(dec_b, cache_win.shape[1], -1), win_kv.reshape(dec_b, dec_t, -1),
                       dec_t, heads, n_kv, A_WINDOW - 1, 1)
    hs = merge_a(o_cmp.reshape(ns, -1), o_sel.reshape(ns, -1), o_win.reshape(ns, -1), gates, hs, w_out)
    outs_s = (kv5(cmp_raw, dec_b, dec_t), kv5(sel_kv, dec_b, dec_t), kv5(win_kv, dec_b, dec_t))
    return hp, hs, outs_p, outs_s


def _mixer_b(hp, hs, g, w_in, sinks, w_out, cache_win, dims, ropes):
    bsz, seq, dec_b, dec_t, _ = dims
    (cos_p, sin_p), (cos_s, sin_s) = ropes
    heads, n_kv = 16, 2
    segs = [(0, 1024, 0, 1024, False), (1024, 1280, 0, 128, False)]
    q, kv = norm_proj(hp, g, w_in, cos_p, sin_p, segs, 512)
    o = band_attention(q.reshape(bsz, seq, -1), kv.reshape(bsz, seq, -1), B_WINDOW - 1, 128, 1024, LANE, 1,
                       lambda c: 0, lambda c: 1, sink=sinks)
    hp = out_proj(o.reshape(bsz * seq, -1), hp, w_out)
    w = min(B_WINDOW, seq)
    out_p = kv.reshape(bsz, seq, 2, n_kv, HD)[:, -w:]
    q, kv = norm_proj(hs, g, w_in, cos_s, sin_s, segs, 512)
    sink_rows = jnp.tile(sinks.astype(F32), dec_t).reshape(dec_t * heads, 1)
    o = dec_window(_block_diag_q(q, dec_t, heads, n_kv), cache_win.reshape(dec_b, cache_win.shape[1], -1),
                   kv.reshape(dec_b, dec_t, -1), dec_t, heads, n_kv, B_WINDOW - 1, 1, sink_rows=sink_rows)
    hs = out_proj(o.reshape(dec_b * dec_t, -1), hs, w_out)
    return hp, hs, out_p, kv.reshape(dec_b, dec_t, 2, n_kv, HD)


def _mixer_c(hp, hs, g, w_in, b_f, w_out, cache_kv, cache_logf, page_table, dims, ropes):
    bsz, seq, dec_b, dec_t, _ = dims
    (cos_p, sin_p), (cos_s, sin_s) = ropes
    heads, n_kv = 16, 4
    segs = [(0, 1024, 0, 0, False), (1024, 1536, 0, 0, False), (1536, 1552, 0, 0, False)]
    q, kv, f = norm_proj(hp, g, w_in, cos_p, sin_p, segs, 512)
    lf_t, cum_t = fox_gate(f.reshape(bsz, seq, heads).transpose(0, 2, 1), b_f)
    logf_p = lf_t.transpose(0, 2, 1)
    half = heads // 2
    cum_k = cum_t.reshape(bsz, 2, half, seq)
    cum_q = cum_k.transpose(0, 1, 3, 2)
    o = flash_attention(q.reshape(bsz, seq, -1), kv.reshape(bsz, seq, -1), 512, 512, 2, 2, "fox", (cum_q, cum_k))
    hp = out_proj(o.reshape(bsz * seq, -1), hp, w_out)
    out_p = kv.reshape(bsz, seq, 2, n_kv, HD)
    q, kv, f = norm_proj(hs, g, w_in, cos_s, sin_s, segs, 512)
    ns = dec_b * dec_t
    lf_t, _ = fox_gate(f.reshape(1, ns, heads).transpose(0, 2, 1), b_f)
    logf_s = lf_t.transpose(0, 2, 1).reshape(dec_b, dec_t, heads)
    f_new = logf_s.reshape(dec_b, dec_t * heads, 1)
    o = dec_paged(_block_diag_q(q, dec_t, heads, n_kv), cache_kv.reshape(cache_kv.shape[0], PAGE, -1), page_table,
                  kv.reshape(dec_b, dec_t, -1), f_new, dec_t, heads, n_kv, "fox", 16, lf_pool=cache_logf)
    hs = out_proj(o.reshape(ns, -1), hs, w_out)
    return hp, hs, (out_p, logf_p), (kv.reshape(dec_b, dec_t, 2, n_kv, HD), logf_s)


def _mixer_d(hp, hs, g, w_in, w_out, caches, dims, ropes):
    bsz, seq, dec_b, dec_t, _ = dims
    (cos_p, sin_p), (cos_s, sin_s) = ropes
    heads = 8
    c = heads * HD
    segs = []
    for grp in range(len(D_GROUPS)):
        base = 3 * grp * c
        segs += [(base, base + c, 0, c, False), (base + c, base + 3 * c, 0, c, False)]
    res = norm_proj(hp, g, w_in, cos_p, sin_p, segs, 256)
    outs, lses, states_p = [], [], []
    for grp, (window, dil) in enumerate(D_GROUPS):
        q, kv = res[2 * grp], res[2 * grp + 1]
        o, lse = band_attention(q.reshape(bsz, seq // dil, dil * c), kv.reshape(bsz, seq // dil, dil * 2 * c),
                                window // dil, 128, c, c, dil, lambda r: 2 * r, lambda r: 2 * r + 1, with_lse=True)
        outs.append(o.reshape(bsz * seq, c))
        lses.append(lse.reshape(bsz * seq, c))
        states_p.append(kv.reshape(bsz, seq, 2, heads, HD)[:, -min(window, seq):])
    hp = merge_d(outs, lses, hp, w_out)
    res = norm_proj(hs, g, w_in, cos_s, sin_s, segs, 256)
    outs, lses, states_s = [], [], []
    for grp, (window, dil) in enumerate(D_GROUPS):
        q, kv = res[2 * grp], res[2 * grp + 1]
        cache = caches[grp]
        o, lse = dec_window(_block_diag_q(q, dec_t, heads, heads), cache.reshape(dec_b, cache.shape[1], -1),
                            kv.reshape(dec_b, dec_t, -1), dec_t, heads, heads, window, dil, with_lse=True)
        outs.append(o.reshape(dec_b * dec_t, c))
        lses.append(lse.reshape(dec_b * dec_t, c))
        states_s.append(kv.reshape(dec_b, dec_t, 2, heads, HD))
    hs = merge_d(outs, lses, hs, w_out)
    return hp, hs, states_p, states_s


def kernel(x_prompt, x_sample, cache_a_cmp, cache_a_sel, cache_a_win, cache_b_win, cache_c_kv, cache_c_logf, cache_d1, cache_d2, cache_d3, page_table, norm_mix, norm_ffn, norm_final, a_w_in, a_phi_w1, a_phi_b1, a_phi_w2, a_w_out, b_w_in, b_sinks, b_w_out, c_w_in, c_b_f, c_w_out, d_w_in, d_w_out, ffn_w_gate, ffn_w_up, ffn_w_down):
    bsz, seq, d_model = x_prompt.shape
    dec_b, dec_t, _ = x_sample.shape
    past_len = page_table.shape[1] * PAGE
    dims = (bsz, seq, dec_b, dec_t, past_len)
    ropes = (_rope_tables(jnp.arange(seq)), _rope_tables(past_len + jnp.arange(dec_b * dec_t) % dec_t))
    hp = x_prompt.reshape(bsz * seq, d_model)
    hs = x_sample.reshape(dec_b * dec_t, d_model)
    bf = lambda z: z.astype(BF16)
    phi = _compress_weights(a_phi_w1, a_phi_b1, a_phi_w2, 2)
    depth = norm_mix.shape[0]
    for layer in range(depth):
        kind = layer % 4
        g = norm_mix[layer]
        if kind == 0:
            hp, hs, a_p, a_s = _mixer_a(hp, hs, g, bf(a_w_in), bf(a_w_out), phi,
                                        (cache_a_cmp, cache_a_sel, cache_a_win), page_table, dims, ropes)
        elif kind == 1:
            hp, hs, b_p, b_s = _mixer_b(hp, hs, g, bf(b_w_in), b_sinks, bf(b_w_out), cache_b_win, dims, ropes)
        elif kind == 2:
            hp, hs, c_p, c_s = _mixer_c(hp, hs, g, bf(c_w_in), c_b_f, bf(c_w_out), cache_c_kv, cache_c_logf,
                                        page_table, dims, ropes)
        else:
            hp, hs, d_p, d_s = _mixer_d(hp, hs, g, bf(d_w_in), bf(d_w_out), (cache_d1, cache_d2, cache_d3), dims, ropes)
        wg, wu, wd = bf(ffn_w_gate[layer]), bf(ffn_w_up[layer]), bf(ffn_w_down[layer])
        hp = ffn(hp, norm_ffn[layer], wg, wu, wd)
        hs = ffn(hs, norm_ffn[layer], wg, wu, wd)
    y_p = final_norm(hp, norm_final).reshape(bsz, seq, d_model)
    y_s = final_norm(hs, norm_final).reshape(dec_b, dec_t, d_model)
    return (y_p, y_s, a_p[0], a_s[0], a_p[1], a_s[1], a_p[2], a_s[2], b_p, b_s,
            c_p[0], c_s[0], c_p[1], c_s[1], d_p[0], d_s[0], d_p[1], d_s[1], d_p[2], d_s[2])
```

```python
import functools

import numpy as np
import jax
import jax.numpy as jnp
from jax import lax
from jax.experimental import pallas as pl
from jax.experimental.pallas import tpu as pltpu

F32 = jnp.float32
BF16 = jnp.bfloat16

HD = 64
SCALE = HD ** -0.5
RMS_EPS = 1e-6
ROPE_THETA = 10000.0
PAGE = 128
CMP_STRIDE = 16
CMP_BLOCK = 32
SEL_BLOCK = 64
N_SEL = 16
A_WINDOW = 512
B_WINDOW = 128
D_GROUPS = ((128, 1), (512, 4), (2048, 16))
FORCE_SCORE = 1e9
NEG = -1e30
DEAD = -3e38
LANE = 128
ONES_ROWS = 16
VMEM_LIMIT = 52 * 1024 * 1024


def _params(sem, vmem=VMEM_LIMIT):
    return pltpu.CompilerParams(dimension_semantics=sem, vmem_limit_bytes=vmem)


def _nt(a, b):
    return lax.dot_general(a, b, (((1,), (1,)), ((), ())), preferred_element_type=F32)


def _dot(a, b):
    return jnp.dot(a, b, preferred_element_type=F32)


def _split3(x):
    hi = x.astype(BF16)
    r1 = x - hi.astype(F32)
    mid = r1.astype(BF16)
    lo = (r1 - mid.astype(F32)).astype(BF16)
    return hi, mid, lo


def _rms(x, g):
    var = jnp.mean(x * x, axis=-1, keepdims=True)
    return x * lax.rsqrt(var + RMS_EPS) * g


def _rope_chunk(x, cos2, sin2):
    lane = lax.broadcasted_iota(jnp.int32, x.shape, 1)
    first = (lane % HD) < (HD // 2)
    swapped = jnp.where(first, pltpu.roll(x, LANE - HD // 2, axis=1), pltpu.roll(x, HD // 2, axis=1))
    return x * cos2 + swapped * sin2


def _rope_tables(pos):
    half = HD // 2
    inv = ROPE_THETA ** (-jnp.arange(half, dtype=F32) / half)
    ang = pos.astype(F32)[:, None] * inv[None, :]
    c, s = jnp.cos(ang), jnp.sin(ang)
    return jnp.tile(c, (1, 4)), jnp.tile(jnp.concatenate([-s, s], axis=1), (1, 2))


def _norm_proj_kernel(x_ref, g_ref, w_ref, cos_ref, sin_ref, *out_refs, segs):
    xn = _rms(x_ref[...], g_ref[...]).astype(BF16)
    outs = list(out_refs)
    for c0, c1, r0, r1, dup in segs:
        acc = _dot(xn, w_ref[:, c0:c1])
        if dup:
            outs.pop(0)[...] = acc
        o_ref = outs.pop(0)
        if r1 <= r0:
            o_ref[...] = acc
            continue
        if r0 > 0:
            o_ref[:, 0:r0] = acc[:, 0:r0]
        for a in range(r0, r1, LANE):
            o_ref[:, a:a + LANE] = _rope_chunk(acc[:, a:a + LANE], cos_ref[...], sin_ref[...])
        if r1 < c1 - c0:
            o_ref[:, r1:] = acc[:, r1:]


def norm_proj(x, g, w, cos2, sin2, segs, tm):
    n, d = x.shape
    tm = min(tm, n)
    out_shape, out_specs = [], []
    for c0, c1, _, _, dup in segs:
        for _ in range(2 if dup else 1):
            out_shape.append(jax.ShapeDtypeStruct((n, c1 - c0), F32))
            out_specs.append(pl.BlockSpec((tm, c1 - c0), lambda i: (i, 0)))
    return pl.pallas_call(
        functools.partial(_norm_proj_kernel, segs=tuple(segs)),
        grid=(n // tm,),
        in_specs=[pl.BlockSpec((tm, d), lambda i: (i, 0)),
                  pl.BlockSpec((1, d), lambda i: (0, 0)),
                  pl.BlockSpec(w.shape, lambda i: (0, 0)),
                  pl.BlockSpec((tm, LANE), lambda i: (i % (cos2.shape[0] // tm), 0)),
                  pl.BlockSpec((tm, LANE), lambda i: (i % (sin2.shape[0] // tm), 0))],
        out_specs=out_specs, out_shape=out_shape,
        compiler_params=_params(("parallel",)), name="norm_proj",
    )(x, g.reshape(1, d), w, cos2, sin2)


def _out_proj_kernel(o_ref, h_ref, w_ref, out_ref):
    out_ref[...] = h_ref[...] + _dot(o_ref[...].astype(BF16), w_ref[...])


def out_proj(o, h, w, tm=512):
    n, k = o.shape
    d = w.shape[1]
    tm = min(tm, n)
    return pl.pallas_call(
        _out_proj_kernel, grid=(n // tm,),
        in_specs=[pl.BlockSpec((tm, k), lambda i: (i, 0)), pl.BlockSpec((tm, d), lambda i: (i, 0)),
                  pl.BlockSpec(w.shape, lambda i: (0, 0))],
        out_specs=pl.BlockSpec((tm, d), lambda i: (i, 0)),
        out_shape=jax.ShapeDtypeStruct((n, d), F32),
        compiler_params=_params(("parallel",)), name="out_proj",
    )(o, h, w)


def _merge_a_kernel(oc_ref, os_ref, ow_ref, g_ref, h_ref, w_ref, out_ref, o_sc, *, heads):
    gate = jax.nn.sigmoid(g_ref[...])
    for h in range(heads):
        sl = slice(h * HD, (h + 1) * HD)
        o = (gate[:, 3 * h:3 * h + 1] * oc_ref[:, sl] + gate[:, 3 * h + 1:3 * h + 2] * os_ref[:, sl]
             + gate[:, 3 * h + 2:3 * h + 3] * ow_ref[:, sl])
        o_sc[:, sl] = o
    out_ref[...] = h_ref[...] + _dot(o_sc[...].astype(BF16), w_ref[...])


def merge_a(o_cmp, o_sel, o_win, gates, h, w, tm=512):
    n, k = o_cmp.shape
    d = w.shape[1]
    tm = min(tm, n)
    row = lambda c: pl.BlockSpec((tm, c), lambda i: (i, 0))
    return pl.pallas_call(
        functools.partial(_merge_a_kernel, heads=k // HD), grid=(n // tm,),
        in_specs=[row(k), row(k), row(k), row(gates.shape[1]), row(d), pl.BlockSpec(w.shape, lambda i: (0, 0))],
        out_specs=row(d), out_shape=jax.ShapeDtypeStruct((n, d), F32),
        scratch_shapes=[pltpu.VMEM((tm, k), F32)],
        compiler_params=_params(("parallel",)), name="merge_a",
    )(o_cmp, o_sel, o_win, gates, h, w)


def _merge_d_kernel(o1, o2, o3, l1, l2, l3, h_ref, w_ref, out_ref):
    a, b, c = l1[...], l2[...], l3[...]
    m = jnp.maximum(jnp.maximum(a, b), c)
    ea, eb, ec = jnp.exp(a - m), jnp.exp(b - m), jnp.exp(c - m)
    den = ea + eb + ec
    o = (ea / den) * o1[...] + (eb / den) * o2[...] + (ec / den) * o3[...]
    out_ref[...] = h_ref[...] + _dot(o.astype(BF16), w_ref[...])


def merge_d(outs, lses, h, w, tm=512):
    n, k = outs[0].shape
    d = w.shape[1]
    tm = min(tm, n)
    row = lambda c: pl.BlockSpec((tm, c), lambda i: (i, 0))
    return pl.pallas_call(
        _merge_d_kernel, grid=(n // tm,),
        in_specs=[row(k)] * 6 + [row(d), pl.BlockSpec(w.shape, lambda i: (0, 0))],
        out_specs=row(d), out_shape=jax.ShapeDtypeStruct((n, d), F32),
        compiler_params=_params(("parallel",)), name="merge_d",
    )(*outs, *lses, h, w)


def _ffn_kernel(h_ref, g_ref, wg_ref, wu_ref, wd_ref, out_ref, xn_sc, acc_sc):
    j = pl.program_id(1)

    @pl.when(j == 0)
    def _():
        x = h_ref[...]
        xn_sc[...] = _rms(x, g_ref[...]).astype(BF16)
        acc_sc[...] = x

    xn = xn_sc[...]
    a = _dot(xn, wg_ref[...])
    u = _dot(xn, wu_ref[...])
    mid = (a * jax.nn.sigmoid(a) * u).astype(BF16)
    acc_sc[...] += _dot(mid, wd_ref[...])

    @pl.when(j == pl.num_programs(1) - 1)
    def _():
        out_ref[...] = acc_sc[...]


def ffn(h, g, wg, wu, wd, tm=512, n_f=2):
    n, d = h.shape
    f = wg.shape[1]
    tm = min(tm, n)
    tf = f // n_f
    return pl.pallas_call(
        _ffn_kernel, grid=(n // tm, n_f),
        in_specs=[pl.BlockSpec((tm, d), lambda i, j: (i, 0)), pl.BlockSpec((1, d), lambda i, j: (0, 0)),
                  pl.BlockSpec((d, tf), lambda i, j: (0, j)), pl.BlockSpec((d, tf), lambda i, j: (0, j)),
                  pl.BlockSpec((tf, d), lambda i, j: (j, 0))],
        out_specs=pl.BlockSpec((tm, d), lambda i, j: (i, 0)),
        out_shape=jax.ShapeDtypeStruct((n, d), F32),
        scratch_shapes=[pltpu.VMEM((tm, d), BF16), pltpu.VMEM((tm, d), F32)],
        compiler_params=_params(("parallel", "arbitrary")), name="ffn",
    )(h, g.reshape(1, d), wg, wu, wd)


def _final_norm_kernel(x_ref, g_ref, o_ref):
    o_ref[...] = _rms(x_ref[...], g_ref[...])


def final_norm(x, g, tm=1024):
    n, d = x.shape
    tm = min(tm, n)
    return pl.pallas_call(
        _final_norm_kernel, grid=(n // tm,),
        in_specs=[pl.BlockSpec((tm, d), lambda i: (i, 0)), pl.BlockSpec((1, d), lambda i: (0, 0))],
        out_specs=pl.BlockSpec((tm, d), lambda i: (i, 0)),
        out_shape=jax.ShapeDtypeStruct((n, d), F32),
        compiler_params=_params(("parallel",)), name="final_norm",
    )(x, g.reshape(1, d))


def _fox_gate_kernel(f_ref, b_ref, lf_ref, cum_ref):
    x = f_ref[0] + b_ref[...]
    lf = jnp.minimum(x, 0.0) - jnp.log1p(jnp.exp(-jnp.abs(x)))
    lf_ref[0] = lf
    n = x.shape[1]
    lane = lax.broadcasted_iota(jnp.int32, x.shape, 1)
    c = lf
    s = 1
    while s < n:
        c = c + jnp.where(lane >= s, pltpu.roll(c, s, axis=1), 0.0)
        s *= 2
    cum_ref[0] = c


def fox_gate(f_t, b_f):
    b, hh, n = f_t.shape
    spec = pl.BlockSpec((1, hh, n), lambda i: (i, 0, 0))
    return pl.pallas_call(
        _fox_gate_kernel, grid=(b,),
        in_specs=[spec, pl.BlockSpec((hh, 1), lambda i: (0, 0))],
        out_specs=[spec, spec], out_shape=[jax.ShapeDtypeStruct(f_t.shape, F32)] * 2,
        compiler_params=_params(("parallel",)), name="fox_gate",
    )(f_t, b_f.reshape(hh, 1))


def _attn_init(m_sc, acc_sc):
    m_sc[...] = jnp.full(m_sc.shape, NEG, F32)
    acc_sc[...] = jnp.zeros(acc_sc.shape, F32)


def _value_rows(v_ref):
    vt = v_ref[0].T.astype(BF16)
    ones = jnp.ones((ONES_ROWS, vt.shape[1]), BF16)
    return [jnp.concatenate([vt[kv * HD:(kv + 1) * HD], ones], axis=0) for kv in range(vt.shape[0] // HD)]


def _attn_step(h, k, vt, q, bias, mask, m_sc, acc_sc):
    s = _nt(k, q)
    if bias is not None:
        s = s + bias[0] - bias[1]
    if mask is not None:
        s = jnp.where(mask, s, NEG)
    m_prev = m_sc[h]
    m_new = jnp.maximum(m_prev, jnp.max(s, axis=0, keepdims=True))
    acc_sc[h] = jnp.exp(m_prev - m_new) * acc_sc[h] + _dot(vt, jnp.exp(s - m_new).astype(BF16))
    m_sc[h] = m_new


def _attn_finish(hq, m_sc, acc_sc, ot_sc, o_ref, sink_ref=None, lset_sc=None, lse_ref=None):
    for h in range(hq):
        acc = acc_sc[h]
        num, l, m = acc[0:HD], acc[HD:HD + 1], m_sc[h]
        if sink_ref is not None:
            sk = sink_ref[:, h:h + 1]
            m_f = jnp.maximum(m, sk)
            w_old = jnp.exp(m - m_f)
            l = l * w_old + jnp.exp(sk - m_f)
            num = num * w_old
            m = m_f
        ot_sc[h * HD:(h + 1) * HD, :] = num / l
        if lset_sc is not None:
            lset_sc[h * HD:(h + 1) * HD, :] = jnp.broadcast_to(m + jnp.log(l), num.shape)
    o_ref[0] = ot_sc[...].T
    if lset_sc is not None:
        lse_ref[0] = lset_sc[...].T


def _q_head(q_ref, h):
    return (q_ref[0, :, h * HD:(h + 1) * HD] * SCALE).astype(BF16)


def _band_kernel(*refs, t, nb, n_back, dil, hq, group, with_sink, with_lse):
    refs = list(refs)
    q_ref, k_ref, v_ref = refs[:3]
    pos = 3
    sink_ref = None
    if with_sink:
        sink_ref = refs[pos]
        pos += 1
    o_ref = refs[pos]
    pos += 1
    lse_ref = None
    if with_lse:
        lse_ref = refs[pos]
        pos += 1
    m_sc, acc_sc, ot_sc = refs[pos:pos + 3]
    lset_sc = refs[pos + 3] if with_lse else None
    qi, j = pl.program_id(1), pl.program_id(2)
    kt = qi - nb + j

    @pl.when(j == 0)
    def _():
        _attn_init(m_sc, acc_sc)

    @pl.when(kt >= 0)
    def _():
        dist = (qi * t + lax.broadcasted_iota(jnp.int32, (t, t), 1)) - (kt * t + lax.broadcasted_iota(jnp.int32, (t, t), 0))
        band = jnp.where(dist >= 0, dist, n_back + 1)
        if dil > 1:
            band = jnp.where((dist & (dil - 1)) == 0, band, n_back + 1)
        mask = band <= n_back
        vts = _value_rows(v_ref)
        for h in range(hq):
            kv = h // group
            k = k_ref[0, :, kv * HD:(kv + 1) * HD].astype(BF16)
            _attn_step(h, k, vts[kv], _q_head(q_ref, h), None, mask, m_sc, acc_sc)

    @pl.when(j == nb)
    def _():
        _attn_finish(hq, m_sc, acc_sc, ot_sc, o_ref, sink_ref, lset_sc, lse_ref)


def band_attention(q, kv, n_back, dil, t, sink=None, with_lse=False):
    b, l, cq = q.shape
    ck = kv.shape[2] // 2
    nb = -(-n_back // t)
    hq, hkv = cq // HD, ck // HD
    kmap = lambda blk: (lambda bi, qi, j: (bi, jnp.maximum(qi - nb + j, 0), blk))
    in_specs = [pl.BlockSpec((1, t, cq), lambda bi, qi, j: (bi, qi, 0)),
                pl.BlockSpec((1, t, ck), kmap(0)), pl.BlockSpec((1, t, ck), kmap(1))]
    args = [q, kv, kv]
    if sink is not None:
        in_specs.append(pl.BlockSpec((1, hq), lambda bi, qi, j: (0, 0)))
        args.append(sink.reshape(1, hq))
    o_spec = pl.BlockSpec((1, t, cq), lambda bi, qi, j: (bi, qi, 0))
    o_shape = jax.ShapeDtypeStruct(q.shape, F32)
    scratch = [pltpu.VMEM((hq, 1, t), F32), pltpu.VMEM((hq, HD + ONES_ROWS, t), F32), pltpu.VMEM((cq, t), F32)]
    if with_lse:
        scratch.append(pltpu.VMEM((cq, t), F32))
    return pl.pallas_call(
        functools.partial(_band_kernel, t=t, nb=nb, n_back=n_back, dil=dil, hq=hq, group=hq // hkv,
                          with_sink=sink is not None, with_lse=with_lse),
        grid=(b, l // t, nb + 1),
        in_specs=in_specs,
        out_specs=[o_spec, o_spec] if with_lse else o_spec,
        out_shape=[o_shape, o_shape] if with_lse else o_shape,
        scratch_shapes=scratch,
        compiler_params=_params(("parallel", "parallel", "arbitrary")), name="band_attention",
    )(*args)


def _flash_kernel(qt_ref, kt_ref, *refs, t, hq, group, mode):
    if mode == "fox":
        q_ref, k_ref, v_ref, cq_ref, ck_ref, o_ref, m_sc, acc_sc, ot_sc = refs
    else:
        q_ref, k_ref, v_ref, sel_ref, o_ref, m_sc, acc_sc, ot_sc = refs
    step = pl.program_id(2)
    qi, kj = qt_ref[step], kt_ref[step]

    @pl.when(kj == 0)
    def _():
        _attn_init(m_sc, acc_sc)

    def body(diagonal):
        causal = None
        if diagonal:
            causal = lax.broadcasted_iota(jnp.int32, (t, t), 1) >= lax.broadcasted_iota(jnp.int32, (t, t), 0)
        masks = {}
        if mode == "sel":
            selb = sel_ref[0].astype(BF16)
            width = selb.shape[1]
            n_blk = width // (hq // group)
            key_blk = (kj * t + lax.broadcasted_iota(jnp.int32, (t, width), 0)) // SEL_BLOCK
            col = lax.broadcasted_iota(jnp.int32, (t, width), 1)
            for kv in range(hq // group):
                expand = jnp.where(col == kv * n_blk + key_blk, 1.0, 0.0).astype(BF16)
                flag = _nt(expand, selb)
                if diagonal:
                    flag = jnp.where(causal, flag, 0.0)
                masks[kv] = flag > 0.5
        vts = _value_rows(v_ref)
        for h in range(hq):
            kv = h // group
            k = k_ref[0, :, kv * HD:(kv + 1) * HD].astype(BF16)
            bias = None
            if mode == "fox":
                bias = (cq_ref[0, 0, h:h + 1, :], ck_ref[0, 0, :, h:h + 1])
                mask = causal
            else:
                mask = masks[kv]
            _attn_step(h, k, vts[kv], _q_head(q_ref, h), bias, mask, m_sc, acc_sc)

    @pl.when(kj < qi)
    def _():
        body(False)

    @pl.when(kj == qi)
    def _():
        body(True)
        _attn_finish(hq, m_sc, acc_sc, ot_sc, o_ref)


def flash_attention(q, kv, t, cq, n_col, mode, extra):
    b, l, _ = q.shape
    hq = cq // HD
    nq = l // t
    pairs = [(qi, kj) for qi in range(nq) for kj in range(qi + 1)]
    qt = jnp.asarray([p[0] for p in pairs], jnp.int32)
    kt = jnp.asarray([p[1] for p in pairs], jnp.int32)
    kmap = lambda off: (lambda bi, c, s, qt, kt: (bi, kt[s], off + c))
    in_specs = [pl.BlockSpec((1, t, cq), lambda bi, c, s, qt, kt: (bi, qt[s], c)),
                pl.BlockSpec((1, t, LANE), kmap(0)), pl.BlockSpec((1, t, LANE), kmap(n_col))]
    if mode == "fox":
        in_specs += [pl.BlockSpec((1, 1, hq, t), lambda bi, c, s, qt, kt: (bi, c, 0, qt[s])),
                     pl.BlockSpec((1, 1, t, hq), lambda bi, c, s, qt, kt: (bi, c, kt[s], 0))]
    else:
        in_specs += [pl.BlockSpec((1, t, extra[0].shape[2]), lambda bi, c, s, qt, kt: (bi, qt[s], 0))]
    return pl.pallas_call(
        functools.partial(_flash_kernel, t=t, hq=hq, group=hq // 2, mode=mode),
        grid_spec=pltpu.PrefetchScalarGridSpec(
            num_scalar_prefetch=2, grid=(b, n_col, len(pairs)), in_specs=in_specs,
            out_specs=pl.BlockSpec((1, t, cq), lambda bi, c, s, qt, kt: (bi, qt[s], c)),
            scratch_shapes=[pltpu.VMEM((hq, 1, t), F32), pltpu.VMEM((hq, HD + ONES_ROWS, t), F32),
                            pltpu.VMEM((cq, t), F32)]),
        out_shape=jax.ShapeDtypeStruct(q.shape, F32),
        compiler_params=_params(("parallel", "parallel", "arbitrary")), name="flash_" + mode,
    )(qt, kt, q, kv, kv, *extra)


def _compress_kernel(*refs, n_src, chunks, paged):
    refs = refs[1:] if paged else refs
    n_in = n_src if paged else 2
    src = refs[:n_in]
    w1_ref, b1_ref, w2_ref, out_ref, x_sc = refs[n_in:n_in + 5]
    total = n_src * chunks
    if paged:
        tok_sc = refs[n_in + 5]
    for i in range(n_src):
        for j in range(2):
            if paged:
                tok_sc[j] = src[i][0, j * LANE:(j + 1) * LANE, :].T
            for s in range(CMP_STRIDE):
                if paged:
                    rows = tok_sc[j, pl.ds(s, chunks, stride=CMP_STRIDE), :]
                else:
                    rows = src[j][0, pl.ds(s, chunks, stride=CMP_STRIDE), :]
                x_sc[j, i * chunks:(i + 1) * chunks, s * LANE:(s + 1) * LANE] = rows
    for j in range(2):
        parts = _dot(x_sc[j].astype(BF16), w1_ref[j])
        half = parts.shape[1] // 2
        pre = b1_ref[j] + parts[:, :half] + pltpu.roll(parts[:, half:], total - 1, axis=0)
        act = pre * jax.nn.sigmoid(pre)
        out_ref[0, :, j * LANE:(j + 1) * LANE] = _dot(act.astype(BF16), w2_ref[j])


def compress(x, page_table, w1p, b1p, w2p):
    const = lambda nd: (lambda *_: (0,) * nd)
    w_specs = [pl.BlockSpec(w1p.shape, const(3)), pl.BlockSpec(b1p.shape, const(3)), pl.BlockSpec(w2p.shape, const(3))]
    if page_table is None:
        b, l, width = x.shape
        n_src, chunks = 1, l // CMP_STRIDE
        scratch = [pltpu.VMEM((2, chunks, CMP_STRIDE * width // 2), F32)]
        halves = [pl.BlockSpec((1, l, LANE), lambda i: (i, 0, 0)), pl.BlockSpec((1, l, LANE), lambda i: (i, 0, 1))]
        grid_spec = pl.GridSpec(
            grid=(b,), in_specs=halves + w_specs,
            out_specs=pl.BlockSpec((1, chunks, width), lambda i: (i, 0, 0)), scratch_shapes=scratch)
        args = (x, x, w1p, b1p, w2p)
    else:
        b, n_src = page_table.shape
        _, width, rows = x.shape
        chunks = rows // CMP_STRIDE
        scratch = [pltpu.VMEM((2, n_src * chunks, CMP_STRIDE * width // 2), F32), pltpu.VMEM((2, rows, LANE), F32)]
        page_spec = lambda p: pl.BlockSpec((1, width, rows), lambda i, pt: (pt[i, p], 0, 0))
        grid_spec = pltpu.PrefetchScalarGridSpec(
            num_scalar_prefetch=1, grid=(b,), in_specs=[page_spec(p) for p in range(n_src)] + w_specs,
            out_specs=pl.BlockSpec((1, n_src * chunks, width), lambda i, pt: (i, 0, 0)), scratch_shapes=scratch)
        args = (page_table,) + (x,) * n_src + (w1p, b1p, w2p)
    return pl.pallas_call(
        functools.partial(_compress_kernel, n_src=n_src, chunks=chunks, paged=page_table is not None),
        grid_spec=grid_spec,
        out_shape=jax.ShapeDtypeStruct((b, n_src * chunks, width), F32),
        compiler_params=_params(("arbitrary",)), name="compress",
    )(*args)


def _compress_weights(phi_w1, phi_b1, phi_w2, n_kv):
    hidden = phi_w1.shape[-1]
    parts = CMP_BLOCK // CMP_STRIDE
    eye = jnp.eye(n_kv, dtype=F32)
    w1 = phi_w1.reshape(2, parts, CMP_STRIDE, HD, hidden)
    w1p = jnp.einsum("jrsdh,kl->jskdrlh", w1, eye).reshape(2, CMP_STRIDE * n_kv * HD, parts * n_kv * hidden)
    b1p = jnp.tile(phi_b1[:, None, :], (1, 1, n_kv))
    w2p = jnp.einsum("jhd,kl->jkhld", phi_w2, eye).reshape(2, n_kv * hidden, n_kv * HD)
    return w1p.astype(BF16), b1p, w2p.astype(BF16)


def _top_blocks(imp, n_take):
    lane = lax.broadcasted_iota(jnp.int32, imp.shape, 1).astype(F32)
    sel = jnp.zeros(imp.shape, F32)
    cur = imp
    for _ in range(n_take):
        mx = jnp.max(cur, axis=1, keepdims=True)
        first = jnp.min(jnp.where(cur == mx, lane, 1e9), axis=1, keepdims=True)
        pick = lane == first
        sel = jnp.where(pick, 1.0, sel)
        cur = jnp.where(pick, DEAD, cur)
    return sel


def _cmp_prompt_kernel(q_ref, kvc_ref, cover_ref, o_ref, sel_ref, *, t, n_cmp, n_kv, group, n_blk):
    qi = pl.program_id(1)
    rows = kvc_ref.shape[1]
    qpos = qi * t + lax.broadcasted_iota(jnp.int32, (t, 1), 0)
    ci = lax.broadcasted_iota(jnp.int32, (1, rows), 1)
    valid = (ci * CMP_STRIDE + CMP_BLOCK - 1 <= qpos) & (ci < n_cmp)
    blk = lax.broadcasted_iota(jnp.int32, (1, n_blk), 1)
    cur = qpos // SEL_BLOCK
    forced = (blk == 0) | (blk == cur) | (blk == cur - 1)
    cover = cover_ref[...]
    for kv in range(n_kv):
        kc = kvc_ref[0, :, kv * HD:(kv + 1) * HD].astype(BF16)
        vc = kvc_ref[0, :, (n_kv + kv) * HD:(n_kv + kv + 1) * HD].astype(BF16)
        imp = jnp.zeros((t, n_blk), F32)
        for g in range(group):
            h = kv * group + g
            s = jnp.where(valid, _nt(q_ref[0, :, h * HD:(h + 1) * HD].astype(BF16), kc) * SCALE, NEG)
            m = jnp.max(s, axis=1, keepdims=True)
            e = jnp.where(valid, jnp.exp(s - m), 0.0)
            den = jnp.sum(e, axis=1, keepdims=True)
            p = (e / jnp.where(den > 0, den, 1.0)).astype(BF16)
            o_ref[0, :, h * HD:(h + 1) * HD] = _dot(p, vc)
            imp = imp + _dot(p, cover)
        imp = jnp.where(blk * SEL_BLOCK <= qpos, jnp.where(forced, FORCE_SCORE, imp), NEG)
        sel_ref[0, :, kv * n_blk:(kv + 1) * n_blk] = _top_blocks(imp, min(N_SEL, n_blk))


def _cover_matrix(rows, n_cmp, n_blk, width):
    c_start = np.arange(rows)[:, None] * CMP_STRIDE
    s_start = np.arange(width)[None, :] * SEL_BLOCK
    cov = (c_start < s_start + SEL_BLOCK) & (c_start + CMP_BLOCK > s_start)
    cov &= (np.arange(rows)[:, None] < n_cmp) & (np.arange(width)[None, :] < n_blk)
    return jnp.asarray(cov, dtype=BF16)


def cmp_attend_prompt(q, kvc, t, n_kv):
    b, l, cq = q.shape
    rows = kvc.shape[1]
    n_blk = l // SEL_BLOCK
    cover = _cover_matrix(rows, rows - 1, n_blk, n_blk)
    return pl.pallas_call(
        functools.partial(_cmp_prompt_kernel, t=t, n_cmp=rows - 1, n_kv=n_kv, group=cq // HD // n_kv, n_blk=n_blk),
        grid=(b, l // t),
        in_specs=[pl.BlockSpec((1, t, cq), lambda bi, qi: (bi, qi, 0)),
                  pl.BlockSpec((1, rows, kvc.shape[2]), lambda bi, qi: (bi, 0, 0)),
                  pl.BlockSpec(cover.shape, lambda bi, qi: (0, 0))],
        out_specs=[pl.BlockSpec((1, t, cq), lambda bi, qi: (bi, qi, 0)),
                   pl.BlockSpec((1, t, n_kv * n_blk), lambda bi, qi: (bi, qi, 0))],
        out_shape=[jax.ShapeDtypeStruct(q.shape, F32), jax.ShapeDtypeStruct((b, l, n_kv * n_blk), F32)],
        compiler_params=_params(("parallel", "parallel")), name="cmp_attend_prompt",
    )(q, kvc, cover)


def _diag_heads(x, rows_kv, n_kv):
    out = jnp.zeros((x.shape[0], HD), F32)
    for kv in range(n_kv):
        out = out + jnp.where(rows_kv == kv, x[:, kv * HD:(kv + 1) * HD], 0.0)
    return out


def _cmp_sample_kernel(q_ref, kvc_ref, cover_ref, o_ref, sel_ref, *, n_tok, heads, n_kv, n_cmp, past_len, n_blk):
    rows = kvc_ref.shape[1]
    r_all = n_tok * heads
    c = n_kv * HD
    width = cover_ref.shape[1]
    group = heads // n_kv
    ridx = lax.broadcasted_iota(jnp.int32, (r_all, 1), 0)
    qpos = past_len + ridx // heads
    ci = lax.broadcasted_iota(jnp.int32, (1, rows), 1)
    valid = (ci * CMP_STRIDE + CMP_BLOCK - 1 <= qpos) & (ci < n_cmp)
    kc = kvc_ref[0, :, 0:c].astype(BF16)
    vc = kvc_ref[0, :, c:2 * c].astype(BF16)
    s = jnp.where(valid, _nt(q_ref[0].astype(BF16), kc) * SCALE, NEG)
    m = jnp.max(s, axis=1, keepdims=True)
    e = jnp.where(valid, jnp.exp(s - m), 0.0)
    den = jnp.sum(e, axis=1, keepdims=True)
    p = (e / jnp.where(den > 0, den, 1.0)).astype(BF16)
    o_ref[0] = _diag_heads(_dot(p, vc), (ridx % heads) // group, n_kv)
    imp = jnp.sum(_dot(p, cover_ref[...]).reshape(r_all // group, group, width), axis=1)
    g_rows = r_all // group
    gpos = past_len + lax.broadcasted_iota(jnp.int32, (g_rows, 1), 0) // n_kv
    blk = lax.broadcasted_iota(jnp.int32, (1, width), 1)
    cur = gpos // SEL_BLOCK
    forced = (blk == 0) | (blk == cur) | (blk == cur - 1)
    imp = jnp.where(blk * SEL_BLOCK <= gpos, jnp.where(forced, FORCE_SCORE, imp), NEG)
    imp = jnp.where(blk < n_blk, imp, DEAD)
    sel_ref[0] = _top_blocks(imp, min(N_SEL, n_blk))


def cmp_attend_sample(qbd, kvc, n_tok, heads, n_kv, past_len):
    b, r_all, c = qbd.shape
    rows = kvc.shape[1]
    n_blk = past_len // SEL_BLOCK + -(-n_tok // SEL_BLOCK)
    width = -(-n_blk // LANE) * LANE
    cover = _cover_matrix(rows, rows - 1, n_blk, width)
    g_rows = r_all // (heads // n_kv)
    return pl.pallas_call(
        functools.partial(_cmp_sample_kernel, n_tok=n_tok, heads=heads, n_kv=n_kv, n_cmp=rows - 1,
                          past_len=past_len, n_blk=n_blk),
        grid=(b,),
        in_specs=[pl.BlockSpec((1, r_all, c), lambda i: (i, 0, 0)),
                  pl.BlockSpec((1, rows, 2 * c), lambda i: (i, 0, 0)),
                  pl.BlockSpec(cover.shape, lambda i: (0, 0))],
        out_specs=[pl.BlockSpec((1, r_all, HD), lambda i: (i, 0, 0)),
                   pl.BlockSpec((1, g_rows, width), lambda i: (i, 0, 0))],
        out_shape=[jax.ShapeDtypeStruct((b, r_all, HD), F32), jax.ShapeDtypeStruct((b, g_rows, width), F32)],
        compiler_params=_params(("parallel",)), name="cmp_attend_sample",
    )(qbd, kvc, cover)


def _dec_init(m_sc, l_sc, acc_sc):
    m_sc[...] = jnp.full(m_sc.shape, NEG, F32)
    l_sc[...] = jnp.zeros(l_sc.shape, F32)
    acc_sc[...] = jnp.zeros(acc_sc.shape, F32)


def _dec_update(qb, kt, vt, bias, mask, m_sc, l_sc, acc_sc):
    s = _dot(qb, kt) * SCALE
    if bias is not None:
        s = s + bias
    if mask is not None:
        s = jnp.where(mask, s, NEG)
    m_prev = m_sc[...]
    m_new = jnp.maximum(m_prev, jnp.max(s, axis=1, keepdims=True))
    alpha = jnp.exp(m_prev - m_new)
    p = jnp.exp(s - m_new)
    if mask is not None:
        p = jnp.where(mask, p, 0.0)
    l_sc[...] = alpha * l_sc[...] + jnp.sum(p, axis=1, keepdims=True)
    acc_sc[...] = alpha * acc_sc[...] + _nt(p.astype(BF16), vt)
    m_sc[...] = m_new


def _dec_new_tokens(q, knew, vnew, biases, masks, m_sc, l_sc, acc_sc):
    n_new = knew.shape[0]
    scores = []
    for u in range(n_new):
        s = jnp.sum(q * knew[u:u + 1, :], axis=1, keepdims=True) * SCALE
        if biases is not None:
            s = s + biases[u]
        scores.append(jnp.where(masks[u], s, NEG))
    m_prev = m_sc[...]
    m_new = m_prev
    for s in scores:
        m_new = jnp.maximum(m_new, s)
    alpha = jnp.exp(m_prev - m_new)
    l = alpha * l_sc[...]
    acc = alpha * acc_sc[...]
    for u in range(n_new):
        p = jnp.where(masks[u], jnp.exp(scores[u] - m_new), 0.0)
        l = l + p
        acc = acc + p * vnew[u:u + 1, :]
    l_sc[...] = l
    acc_sc[...] = acc
    m_sc[...] = m_new


def _token_cumsum(x, n_tok, heads):
    parts, run = [], None
    for u in range(n_tok):
        seg = x[u * heads:(u + 1) * heads]
        run = seg if run is None else run + seg
        parts.append(run)
    return jnp.concatenate(parts, axis=0), parts


def _dec_window_kernel(*refs, n_tok, heads, n_kv, lb, n_back, dil, chunk, with_sink, with_lse):
    refs = list(refs)
    q_ref, cache_ref, new_ref = refs[:3]
    pos = 3
    sink_ref = None
    if with_sink:
        sink_ref = refs[pos]
        pos += 1
    o_ref = refs[pos]
    pos += 1
    lse_ref = None
    if with_lse:
        lse_ref = refs[pos]
        pos += 1
    m_sc, l_sc, acc_sc = refs[pos:pos + 3]
    r_all = n_tok * heads
    c = n_kv * HD
    group = heads // n_kv
    ridx = lax.broadcasted_iota(jnp.int32, (r_all, 1), 0)
    tok = ridx // heads
    _dec_init(m_sc, l_sc, acc_sc)
    q = q_ref[0]
    qb = q.astype(BF16)

    def ok(dist):
        good = (dist >= 0) & (dist <= n_back)
        if dil > 1:
            good = good & (dist % dil == 0)
        return good

    for c0 in range(0, lb, chunk):
        kt = cache_ref[0, 0:c, c0:c0 + chunk].astype(BF16)
        vt = cache_ref[0, c:2 * c, c0:c0 + chunk].astype(BF16)
        dist = (lb + tok) - (c0 + lax.broadcasted_iota(jnp.int32, (1, chunk), 1))
        _dec_update(qb, kt, vt, None, ok(dist), m_sc, l_sc, acc_sc)
    new = new_ref[0]
    _dec_new_tokens(q, new[:, 0:c], new[:, c:2 * c], None, [ok(tok - u) for u in range(n_tok)], m_sc, l_sc, acc_sc)
    m, l, acc = m_sc[...], l_sc[...], acc_sc[...]
    if with_sink:
        sk = sink_ref[...]
        m_f = jnp.maximum(m, sk)
        scale_old = jnp.exp(m - m_f)
        l = l * scale_old + jnp.exp(sk - m_f)
        acc = acc * scale_old
        m = m_f
    o_ref[0] = _diag_heads(acc / l, (ridx % heads) // group, n_kv)
    if with_lse:
        lse_ref[0] = jnp.broadcast_to(m + jnp.log(l), (r_all, HD))


def dec_window(qbd, cache_t, new, n_tok, heads, n_kv, n_back, dil, sink_rows=None, with_lse=False):
    b, r_all, c = qbd.shape
    lb = cache_t.shape[2]
    in_specs = [pl.BlockSpec((1, r_all, c), lambda i: (i, 0, 0)),
                pl.BlockSpec((1, 2 * c, lb), lambda i: (i, 0, 0)),
                pl.BlockSpec((1, n_tok, 2 * c), lambda i: (i, 0, 0))]
    args = [qbd, cache_t, new]
    if sink_rows is not None:
        in_specs.append(pl.BlockSpec((r_all, 1), lambda i: (0, 0)))
        args.append(sink_rows)
    o_spec = pl.BlockSpec((1, r_all, HD), lambda i: (i, 0, 0))
    o_shape = jax.ShapeDtypeStruct((b, r_all, HD), F32)
    return pl.pallas_call(
        functools.partial(_dec_window_kernel, n_tok=n_tok, heads=heads, n_kv=n_kv, lb=lb, n_back=n_back, dil=dil,
                          chunk=min(lb, 512), with_sink=sink_rows is not None, with_lse=with_lse),
        grid=(b,), in_specs=in_specs,
        out_specs=[o_spec, o_spec] if with_lse else o_spec,
        out_shape=[o_shape, o_shape] if with_lse else o_shape,
        scratch_shapes=[pltpu.VMEM((r_all, 1), F32), pltpu.VMEM((r_all, 1), F32), pltpu.VMEM((r_all, c), F32)],
        compiler_params=_params(("parallel",)), name="dec_window",
    )(*args)


def _dec_paged_kernel(*refs, n_pg, n_tok, heads, n_kv, mode):
    pages = refs[1:1 + n_pg]
    pos = 1 + n_pg
    lf_pages = None
    if mode == "fox":
        lf_pages = refs[pos:pos + n_pg]
        pos += n_pg
    q_ref, new_ref, x_ref = refs[pos:pos + 3]
    pos += 3
    if mode == "fox":
        tri_ref = refs[pos]
        pos += 1
    o_ref = refs[pos]
    pos += 1
    k_sc, v_sc, m_sc, l_sc, acc_sc = refs[pos:pos + 5]
    pos += 5
    if mode == "fox":
        c_sc, carry_sc = refs[pos:pos + 2]
    gi = pl.program_id(1)
    r_all = n_tok * heads
    c = n_kv * HD
    group = heads // n_kv
    tk = n_pg * PAGE
    ridx = lax.broadcasted_iota(jnp.int32, (r_all, 1), 0)
    tok = ridx // heads
    head = ridx % heads

    @pl.when(gi == 0)
    def _():
        _dec_init(m_sc, l_sc, acc_sc)
        if mode == "fox":
            carry_sc[...] = jnp.zeros(carry_sc.shape, F32)

    for i in range(n_pg):
        k_sc[:, i * PAGE:(i + 1) * PAGE] = pages[i][0, 0:c, :].astype(BF16)
        v_sc[:, i * PAGE:(i + 1) * PAGE] = pages[i][0, c:2 * c, :].astype(BF16)
    q = q_ref[0]
    qb = q.astype(BF16)

    if mode == "fox":
        tri = tri_ref[...]
        carry = carry_sc[...]
        for i in range(n_pg):
            hi, mid, lo = _split3(lf_pages[i][0])
            cs = _dot(hi, tri) + _dot(mid, tri) + _dot(lo, tri) + carry
            c_sc[:, i * PAGE:(i + 1) * PAGE] = cs
            carry = cs[:, PAGE - 1:PAGE]
        carry_sc[...] = carry
        bias = _token_cumsum(x_ref[0], n_tok, heads)[0] - jnp.concatenate([c_sc[...]] * n_tok, axis=0)
        _dec_update(qb, k_sc[...], v_sc[...], bias, None, m_sc, l_sc, acc_sc)
    else:
        flags = x_ref[0].astype(BF16)
        width = flags.shape[1]
        key_blk = (gi * tk + lax.broadcasted_iota(jnp.int32, (width, tk), 1)) // SEL_BLOCK
        expand = jnp.where(lax.broadcasted_iota(jnp.int32, (width, tk), 0) == key_blk, 1.0, 0.0).astype(BF16)
        _dec_update(qb, k_sc[...], v_sc[...], None, _dot(flags, expand) > 0.5, m_sc, l_sc, acc_sc)

    @pl.when(gi == pl.num_programs(1) - 1)
    def _():
        new = new_ref[0]
        causal = [tok >= u for u in range(n_tok)]
        if mode == "fox":
            fq, per_tok = _token_cumsum(x_ref[0], n_tok, heads)
            total = jnp.concatenate([carry_sc[...]] * n_tok, axis=0)
            biases = [fq - jnp.concatenate([per_tok[u]] * n_tok, axis=0) - total for u in range(n_tok)]
            _dec_new_tokens(q, new[:, 0:c], new[:, c:2 * c], biases, causal, m_sc, l_sc, acc_sc)
        else:
            n_past_blk = (pl.num_programs(1) * tk) // SEL_BLOCK
            flags = x_ref[0]
            lane = lax.broadcasted_iota(jnp.int32, flags.shape, 1)
            new_on = jnp.sum(jnp.where(lane == n_past_blk, flags, 0.0), axis=1, keepdims=True) > 0.5
            _dec_new_tokens(q, new[:, 0:c], new[:, c:2 * c], None, [m & new_on for m in causal], m_sc, l_sc, acc_sc)
        o_ref[0] = _diag_heads(acc_sc[...] / l_sc[...], head // group, n_kv)


def dec_paged(qbd, pool_t, page_table, new, x, n_tok, heads, n_kv, mode, n_pg, lf_pool_t=None):
    b, r_all, c = qbd.shape
    n_pages = page_table.shape[1]
    page_spec = lambda p, rows: pl.BlockSpec((1, rows, PAGE), lambda i, g, pt: (pt[i, g * n_pg + p], 0, 0))
    in_specs = [page_spec(p, 2 * c) for p in range(n_pg)]
    args = [pool_t] * n_pg
    if mode == "fox":
        in_specs += [page_spec(p, heads) for p in range(n_pg)]
        args += [lf_pool_t] * n_pg
    in_specs += [pl.BlockSpec((1, r_all, c), lambda i, g, pt: (i, 0, 0)),
                 pl.BlockSpec((1, n_tok, 2 * c), lambda i, g, pt: (i, 0, 0)),
                 pl.BlockSpec((1, r_all, x.shape[2]), lambda i, g, pt: (i, 0, 0))]
    args += [qbd, new, x]
    scratch = [pltpu.VMEM((c, n_pg * PAGE), BF16), pltpu.VMEM((c, n_pg * PAGE), BF16),
               pltpu.VMEM((r_all, 1), F32), pltpu.VMEM((r_all, 1), F32), pltpu.VMEM((r_all, c), F32)]
    if mode == "fox":
        tri = jnp.asarray(np.triu(np.ones((PAGE, PAGE), np.float32)), dtype=BF16)
        in_specs.append(pl.BlockSpec((PAGE, PAGE), lambda i, g, pt: (0, 0)))
        args.append(tri)
        scratch += [pltpu.VMEM((heads, n_pg * PAGE), F32), pltpu.VMEM((heads, 1), F32)]
    return pl.pallas_call(
        functools.partial(_dec_paged_kernel, n_pg=n_pg, n_tok=n_tok, heads=heads, n_kv=n_kv, mode=mode),
        grid_spec=pltpu.PrefetchScalarGridSpec(
            num_scalar_prefetch=1, grid=(b, n_pages // n_pg), in_specs=in_specs,
            out_specs=pl.BlockSpec((1, r_all, HD), lambda i, g, pt: (i, 0, 0)), scratch_shapes=scratch),
        out_shape=jax.ShapeDtypeStruct((b, r_all, HD), F32),
        compiler_params=_params(("parallel", "arbitrary")), name="dec_paged_" + mode,
    )(page_table, *args)


def _block_diag_q(q, n_tok, heads, n_kv):
    b = q.shape[0] // n_tok
    own = jnp.asarray(np.arange(heads)[:, None] // (heads // n_kv) == np.arange(n_kv)[None, :], dtype=F32)
    qb = q.reshape(b, n_tok, heads, 1, HD) * own[None, None, :, :, None]
    return qb.reshape(b, n_tok * heads, n_kv * HD)


def _rows_last(cache):
    n, rows = cache.shape[:2]
    return cache.transpose(0, 2, 3, 4, 1).reshape(n, -1, rows)


def _mixer_a(hp, hs, g, w_in, w_out, phi, caches, page_table, dims, ropes):
    bsz, seq, dec_b, dec_t, past_len = dims
    (cos_p, sin_p), (cos_s, sin_s) = ropes
    cache_cmp, cache_sel, cache_win = caches
    heads, n_kv = 16, 2
    segs = [(0, 1024, 0, 1024, True), (1024, 1280, 0, 0, False), (1280, 1536, 0, 128, False),
            (1536, 1792, 0, 128, False), (1792, 1840, 0, 0, False)]
    w1p, b1p, w2p = phi
    q_cmp, q_rot, cmp_raw, sel_kv, win_kv, gates = norm_proj(hp, g, w_in, cos_p, sin_p, segs, 512)
    n = bsz * seq
    kvc = compress(cmp_raw.reshape(bsz, seq, -1), None, w1p, b1p, w2p)
    o_cmp, flags = cmp_attend_prompt(q_cmp.reshape(bsz, seq, -1), kvc, 256, n_kv)
    q3 = q_rot.reshape(bsz, seq, -1)
    o_sel = flash_attention(q3, sel_kv.reshape(bsz, seq, -1), 512, 1024, 1, "sel", (flags,))
    o_win = band_attention(q3, win_kv.reshape(bsz, seq, -1), A_WINDOW - 1, 1, 256)
    hp = merge_a(o_cmp.reshape(n, -1), o_sel.reshape(n, -1), o_win.reshape(n, -1), gates, hp, w_out)
    kv5 = lambda z, b, t: z.reshape(b, t, 2, n_kv, HD)
    w = min(A_WINDOW, seq)
    outs_p = (kv5(cmp_raw, bsz, seq), kv5(sel_kv, bsz, seq), kv5(win_kv, bsz, seq)[:, -w:])
    q_cmp, q_rot, cmp_raw, sel_kv, win_kv, gates = norm_proj(hs, g, w_in, cos_s, sin_s, segs, 512)
    ns = dec_b * dec_t
    kvc = compress(_rows_last(cache_cmp), page_table, w1p, b1p, w2p)
    o_cmp, flags = cmp_attend_sample(_block_diag_q(q_cmp, dec_t, heads, n_kv), kvc, dec_t, heads, n_kv, past_len)
    qbd = _block_diag_q(q_rot, dec_t, heads, n_kv)
    flag_rows = jnp.repeat(flags.reshape(dec_b, dec_t, n_kv, 1, -1), heads // n_kv, axis=3).reshape(dec_b, dec_t * heads, -1)
    o_sel = dec_paged(qbd, _rows_last(cache_sel), page_table, sel_kv.reshape(dec_b, dec_t, -1), flag_rows,
                      dec_t, heads, n_kv, "sel", 16)
    o_win = dec_window(qbd, _rows_last(cache_win), win_kv.reshape(dec_b, dec_t, -1), dec_t, heads, n_kv, A_WINDOW - 1, 1)
    hs = merge_a(o_cmp.reshape(ns, -1), o_sel.reshape(ns, -1), o_win.reshape(ns, -1), gates, hs, w_out)
    outs_s = (kv5(cmp_raw, dec_b, dec_t), kv5(sel_kv, dec_b, dec_t), kv5(win_kv, dec_b, dec_t))
    return hp, hs, outs_p, outs_s


def _mixer_b(hp, hs, g, w_in, sinks, w_out, cache_win, dims, ropes):
    bsz, seq, dec_b, dec_t, _ = dims
    (cos_p, sin_p), (cos_s, sin_s) = ropes
    heads, n_kv = 16, 2
    segs = [(0, 1024, 0, 1024, False), (1024, 1280, 0, 128, False)]
    q, kv = norm_proj(hp, g, w_in, cos_p, sin_p, segs, 512)
    o = band_attention(q.reshape(bsz, seq, -1), kv.reshape(bsz, seq, -1), B_WINDOW - 1, 1, 256, sink=sinks)
    hp = out_proj(o.reshape(bsz * seq, -1), hp, w_out)
    w = min(B_WINDOW, seq)
    out_p = kv.reshape(bsz, seq, 2, n_kv, HD)[:, -w:]
    q, kv = norm_proj(hs, g, w_in, cos_s, sin_s, segs, 512)
    sink_rows = jnp.tile(sinks.astype(F32), dec_t).reshape(dec_t * heads, 1)
    o = dec_window(_block_diag_q(q, dec_t, heads, n_kv), _rows_last(cache_win), kv.reshape(dec_b, dec_t, -1),
                   dec_t, heads, n_kv, B_WINDOW - 1, 1, sink_rows=sink_rows)
    hs = out_proj(o.reshape(dec_b * dec_t, -1), hs, w_out)
    return hp, hs, out_p, kv.reshape(dec_b, dec_t, 2, n_kv, HD)


def _mixer_c(hp, hs, g, w_in, b_f, w_out, cache_kv, cache_logf, page_table, dims, ropes):
    bsz, seq, dec_b, dec_t, _ = dims
    (cos_p, sin_p), (cos_s, sin_s) = ropes
    heads, n_kv = 16, 4
    segs = [(0, 1024, 0, 0, False), (1024, 1536, 0, 0, False), (1536, 1552, 0, 0, False)]
    q, kv, f = norm_proj(hp, g, w_in, cos_p, sin_p, segs, 512)
    lf_t, cum_t = fox_gate(f.reshape(bsz, seq, heads).transpose(0, 2, 1), b_f)
    logf_p = lf_t.transpose(0, 2, 1)
    half = heads // 2
    cum_q = cum_t.reshape(bsz, 2, half, seq)
    cum_k = cum_q.transpose(0, 1, 3, 2)
    o = flash_attention(q.reshape(bsz, seq, -1), kv.reshape(bsz, seq, -1), 512, 512, 2, "fox", (cum_q, cum_k))
    hp = out_proj(o.reshape(bsz * seq, -1), hp, w_out)
    out_p = kv.reshape(bsz, seq, 2, n_kv, HD)
    q, kv, f = norm_proj(hs, g, w_in, cos_s, sin_s, segs, 512)
    ns = dec_b * dec_t
    lf_t, _ = fox_gate(f.reshape(1, ns, heads).transpose(0, 2, 1), b_f)
    logf_s = lf_t.transpose(0, 2, 1).reshape(dec_b, dec_t, heads)
    o = dec_paged(_block_diag_q(q, dec_t, heads, n_kv), _rows_last(cache_kv), page_table, kv.reshape(dec_b, dec_t, -1),
                  logf_s.reshape(dec_b, dec_t * heads, 1), dec_t, heads, n_kv, "fox", 16,
                  lf_pool_t=cache_logf.transpose(0, 2, 1))
    hs = out_proj(o.reshape(ns, -1), hs, w_out)
    return hp, hs, (out_p, logf_p), (kv.reshape(dec_b, dec_t, 2, n_kv, HD), logf_s)


def _mixer_d(hp, hs, g, w_in, w_out, caches, dims, ropes):
    bsz, seq, dec_b, dec_t, _ = dims
    (cos_p, sin_p), (cos_s, sin_s) = ropes
    heads = 8
    c = heads * HD
    segs = []
    for grp in range(len(D_GROUPS)):
        base = 3 * grp * c
        segs += [(base, base + c, 0, c, False), (base + c, base + 3 * c, 0, c, False)]
    res = norm_proj(hp, g, w_in, cos_p, sin_p, segs, 256)
    outs, lses, states_p = [], [], []
    for grp, (window, dil) in enumerate(D_GROUPS):
        q, kv = res[2 * grp], res[2 * grp + 1]
        o, lse = band_attention(q.reshape(bsz, seq, c), kv.reshape(bsz, seq, 2 * c), window, dil, 256, with_lse=True)
        outs.append(o.reshape(bsz * seq, c))
        lses.append(lse.reshape(bsz * seq, c))
        states_p.append(kv.reshape(bsz, seq, 2, heads, HD)[:, -min(window, seq):])
    hp = merge_d(outs, lses, hp, w_out)
    res = norm_proj(hs, g, w_in, cos_s, sin_s, segs, 256)
    outs, lses, states_s = [], [], []
    for grp, (window, dil) in enumerate(D_GROUPS):
        q, kv = res[2 * grp], res[2 * grp + 1]
        o, lse = dec_window(_block_diag_q(q, dec_t, heads, heads), _rows_last(caches[grp]),
                            kv.reshape(dec_b, dec_t, -1), dec_t, heads, heads, window, dil, with_lse=True)
        outs.append(o.reshape(dec_b * dec_t, c))
        lses.append(lse.reshape(dec_b * dec_t, c))
        states_s.append(kv.reshape(dec_b, dec_t, 2, heads, HD))
    hs = merge_d(outs, lses, hs, w_out)
    return hp, hs, states_p, states_s


def kernel(x_prompt, x_sample, cache_a_cmp, cache_a_sel, cache_a_win, cache_b_win, cache_c_kv, cache_c_logf, cache_d1, cache_d2, cache_d3, page_table, norm_mix, norm_ffn, norm_final, a_w_in, a_phi_w1, a_phi_b1, a_phi_w2, a_w_out, b_w_in, b_sinks, b_w_out, c_w_in, c_b_f, c_w_out, d_w_in, d_w_out, ffn_w_gate, ffn_w_up, ffn_w_down):
    bsz, seq, d_model = x_prompt.shape
    dec_b, dec_t, _ = x_sample.shape
    past_len = page_table.shape[1] * PAGE
    dims = (bsz, seq, dec_b, dec_t, past_len)
    ropes = (_rope_tables(jnp.arange(seq)), _rope_tables(past_len + jnp.arange(dec_b * dec_t) % dec_t))
    hp = x_prompt.reshape(bsz * seq, d_model)
    hs = x_sample.reshape(dec_b * dec_t, d_model)
    bf = lambda z: z.astype(BF16)
    phi = _compress_weights(a_phi_w1, a_phi_b1, a_phi_w2, 2)
    depth = norm_mix.shape[0]
    for layer in range(depth):
        kind = layer % 4
        g = norm_mix[layer]
        if kind == 0:
            hp, hs, a_p, a_s = _mixer_a(hp, hs, g, bf(a_w_in), bf(a_w_out), phi,
                                        (cache_a_cmp, cache_a_sel, cache_a_win), page_table, dims, ropes)
        elif kind == 1:
            hp, hs, b_p, b_s = _mixer_b(hp, hs, g, bf(b_w_in), b_sinks, bf(b_w_out), cache_b_win, dims, ropes)
        elif kind == 2:
            hp, hs, c_p, c_s = _mixer_c(hp, hs, g, bf(c_w_in), c_b_f, bf(c_w_out), cache_c_kv, cache_c_logf,
                                        page_table, dims, ropes)
        else:
            hp, hs, d_p, d_s = _mixer_d(hp, hs, g, bf(d_w_in), bf(d_w_out), (cache_d1, cache_d2, cache_d3), dims, ropes)
        wg, wu, wd = bf(ffn_w_gate[layer]), bf(ffn_w_up[layer]), bf(ffn_w_down[layer])
        hp = ffn(hp, norm_ffn[layer], wg, wu, wd)
        hs = ffn(hs, norm_ffn[layer], wg, wu, wd)
    y_p = final_norm(hp, norm_final).reshape(bsz, seq, d_model)
    y_s = final_norm(hs, norm_final).reshape(dec_b, dec_t, d_model)
    return (y_p, y_s, a_p[0], a_s[0], a_p[1], a_s[1], a_p[2], a_s[2], b_p, b_s,
            c_p[0], c_s[0], c_p[1], c_s[1], d_p[0], d_s[0], d_p[1], d_s[1], d_p[2], d_s[2])
```

```python
import functools

import numpy as np
import jax
import jax.numpy as jnp
from jax import lax
from jax.experimental import pallas as pl
from jax.experimental.pallas import tpu as pltpu

F32 = jnp.float32
BF16 = jnp.bfloat16

HD = 64
SCALE = HD ** -0.5
LOG2E = 1.4426950408889634
LN2 = 0.6931471805599453
RMS_EPS = 1e-6
ROPE_THETA = 10000.0
PAGE = 128
CMP_STRIDE = 16
CMP_BLOCK = 32
SEL_BLOCK = 64
N_SEL = 16
A_WINDOW = 512
B_WINDOW = 128
D_GROUPS = ((128, 1), (512, 4), (2048, 16))
FORCE_SCORE = 1e9
NEG = -1e30
DEAD = -3e38
LANE = 128
ONES_ROWS = 16
VMEM_LIMIT = 52 * 1024 * 1024


def _params(sem, vmem=VMEM_LIMIT):
    return pltpu.CompilerParams(dimension_semantics=sem, vmem_limit_bytes=vmem)


def _nt(a, b):
    return lax.dot_general(a, b, (((1,), (1,)), ((), ())), preferred_element_type=F32)


def _dot(a, b):
    return jnp.dot(a, b, preferred_element_type=F32)


def _split3(x):
    hi = x.astype(BF16)
    r1 = x - hi.astype(F32)
    mid = r1.astype(BF16)
    lo = (r1 - mid.astype(F32)).astype(BF16)
    return hi, mid, lo


def _rms(x, g):
    var = jnp.mean(x * x, axis=-1, keepdims=True)
    return x * lax.rsqrt(var + RMS_EPS) * g


def _rope_chunk(x, cos2, sin2):
    lane = lax.broadcasted_iota(jnp.int32, x.shape, 1)
    first = (lane % HD) < (HD // 2)
    swapped = jnp.where(first, pltpu.roll(x, LANE - HD // 2, axis=1), pltpu.roll(x, HD // 2, axis=1))
    return x * cos2 + swapped * sin2


def _rope_tables(pos):
    half = HD // 2
    inv = ROPE_THETA ** (-jnp.arange(half, dtype=F32) / half)
    ang = pos.astype(F32)[:, None] * inv[None, :]
    c, s = jnp.cos(ang), jnp.sin(ang)
    return jnp.tile(c, (1, 4)), jnp.tile(jnp.concatenate([-s, s], axis=1), (1, 2))


def _norm_proj_kernel(x_ref, g_ref, w_ref, cos_ref, sin_ref, *out_refs, segs):
    xn = _rms(x_ref[...], g_ref[...]).astype(BF16)
    outs = list(out_refs)
    for c0, c1, r0, r1, dup in segs:
        acc = _dot(xn, w_ref[:, c0:c1])
        if dup:
            outs.pop(0)[...] = acc
        o_ref = outs.pop(0)
        if r1 <= r0:
            o_ref[...] = acc
            continue
        if r0 > 0:
            o_ref[:, 0:r0] = acc[:, 0:r0]
        for a in range(r0, r1, LANE):
            o_ref[:, a:a + LANE] = _rope_chunk(acc[:, a:a + LANE], cos_ref[...], sin_ref[...])
        if r1 < c1 - c0:
            o_ref[:, r1:] = acc[:, r1:]


def norm_proj(x, g, w, cos2, sin2, segs, tm):
    n, d = x.shape
    tm = min(tm, n)
    out_shape, out_specs = [], []
    for c0, c1, _, _, dup in segs:
        for _ in range(2 if dup else 1):
            out_shape.append(jax.ShapeDtypeStruct((n, c1 - c0), F32))
            out_specs.append(pl.BlockSpec((tm, c1 - c0), lambda i: (i, 0)))
    return pl.pallas_call(
        functools.partial(_norm_proj_kernel, segs=tuple(segs)),
        grid=(n // tm,),
        in_specs=[pl.BlockSpec((tm, d), lambda i: (i, 0)),
                  pl.BlockSpec((1, d), lambda i: (0, 0)),
                  pl.BlockSpec(w.shape, lambda i: (0, 0)),
                  pl.BlockSpec((tm, LANE), lambda i: (i % (cos2.shape[0] // tm), 0)),
                  pl.BlockSpec((tm, LANE), lambda i: (i % (sin2.shape[0] // tm), 0))],
        out_specs=out_specs, out_shape=out_shape,
        compiler_params=_params(("parallel",)), name="norm_proj",
    )(x, g.reshape(1, d), w, cos2, sin2)


def _out_proj_kernel(o_ref, h_ref, w_ref, out_ref):
    out_ref[...] = h_ref[...] + _dot(o_ref[...].astype(BF16), w_ref[...])


def out_proj(o, h, w, tm=512):
    n, k = o.shape
    d = w.shape[1]
    tm = min(tm, n)
    return pl.pallas_call(
        _out_proj_kernel, grid=(n // tm,),
        in_specs=[pl.BlockSpec((tm, k), lambda i: (i, 0)), pl.BlockSpec((tm, d), lambda i: (i, 0)),
                  pl.BlockSpec(w.shape, lambda i: (0, 0))],
        out_specs=pl.BlockSpec((tm, d), lambda i: (i, 0)),
        out_shape=jax.ShapeDtypeStruct((n, d), F32),
        compiler_params=_params(("parallel",)), name="out_proj",
    )(o, h, w)


def _merge_a_kernel(oc_ref, os_ref, ow_ref, g_ref, h_ref, w_ref, out_ref, o_sc, *, heads):
    gate = jax.nn.sigmoid(g_ref[...])
    for h in range(heads):
        sl = slice(h * HD, (h + 1) * HD)
        o = (gate[:, 3 * h:3 * h + 1] * oc_ref[:, sl] + gate[:, 3 * h + 1:3 * h + 2] * os_ref[:, sl]
             + gate[:, 3 * h + 2:3 * h + 3] * ow_ref[:, sl])
        o_sc[:, sl] = o
    out_ref[...] = h_ref[...] + _dot(o_sc[...].astype(BF16), w_ref[...])


def merge_a(o_cmp, o_sel, o_win, gates, h, w, tm=512):
    n, k = o_cmp.shape
    d = w.shape[1]
    tm = min(tm, n)
    row = lambda c: pl.BlockSpec((tm, c), lambda i: (i, 0))
    return pl.pallas_call(
        functools.partial(_merge_a_kernel, heads=k // HD), grid=(n // tm,),
        in_specs=[row(k), row(k), row(k), row(gates.shape[1]), row(d), pl.BlockSpec(w.shape, lambda i: (0, 0))],
        out_specs=row(d), out_shape=jax.ShapeDtypeStruct((n, d), F32),
        scratch_shapes=[pltpu.VMEM((tm, k), F32)],
        compiler_params=_params(("parallel",)), name="merge_a",
    )(o_cmp, o_sel, o_win, gates, h, w)


def _merge_d_kernel(o1, o2, o3, l1, l2, l3, h_ref, w_ref, out_ref):
    a, b, c = l1[...], l2[...], l3[...]
    m = jnp.maximum(jnp.maximum(a, b), c)
    ea, eb, ec = jnp.exp(a - m), jnp.exp(b - m), jnp.exp(c - m)
    den = ea + eb + ec
    o = (ea / den) * o1[...] + (eb / den) * o2[...] + (ec / den) * o3[...]
    out_ref[...] = h_ref[...] + _dot(o.astype(BF16), w_ref[...])


def merge_d(outs, lses, h, w, tm=512):
    n, k = outs[0].shape
    d = w.shape[1]
    tm = min(tm, n)
    row = lambda c: pl.BlockSpec((tm, c), lambda i: (i, 0))
    return pl.pallas_call(
        _merge_d_kernel, grid=(n // tm,),
        in_specs=[row(k)] * 6 + [row(d), pl.BlockSpec(w.shape, lambda i: (0, 0))],
        out_specs=row(d), out_shape=jax.ShapeDtypeStruct((n, d), F32),
        compiler_params=_params(("parallel",)), name="merge_d",
    )(*outs, *lses, h, w)


def _ffn_kernel(h_ref, g_ref, wg_ref, wu_ref, wd_ref, out_ref, xn_sc, acc_sc):
    j = pl.program_id(1)

    @pl.when(j == 0)
    def _():
        x = h_ref[...]
        xn_sc[...] = _rms(x, g_ref[...]).astype(BF16)
        acc_sc[...] = x

    xn = xn_sc[...]
    a = _dot(xn, wg_ref[...])
    u = _dot(xn, wu_ref[...])
    mid = (a * jax.nn.sigmoid(a) * u).astype(BF16)
    acc_sc[...] += _dot(mid, wd_ref[...])

    @pl.when(j == pl.num_programs(1) - 1)
    def _():
        out_ref[...] = acc_sc[...]


def ffn(h, g, wg, wu, wd, tm=512, n_f=2):
    n, d = h.shape
    f = wg.shape[1]
    tm = min(tm, n)
    tf = f // n_f
    return pl.pallas_call(
        _ffn_kernel, grid=(n // tm, n_f),
        in_specs=[pl.BlockSpec((tm, d), lambda i, j: (i, 0)), pl.BlockSpec((1, d), lambda i, j: (0, 0)),
                  pl.BlockSpec((d, tf), lambda i, j: (0, j)), pl.BlockSpec((d, tf), lambda i, j: (0, j)),
                  pl.BlockSpec((tf, d), lambda i, j: (j, 0))],
        out_specs=pl.BlockSpec((tm, d), lambda i, j: (i, 0)),
        out_shape=jax.ShapeDtypeStruct((n, d), F32),
        scratch_shapes=[pltpu.VMEM((tm, d), BF16), pltpu.VMEM((tm, d), F32)],
        compiler_params=_params(("parallel", "arbitrary")), name="ffn",
    )(h, g.reshape(1, d), wg, wu, wd)


def _final_norm_kernel(x_ref, g_ref, o_ref):
    o_ref[...] = _rms(x_ref[...], g_ref[...])


def final_norm(x, g, tm=1024):
    n, d = x.shape
    tm = min(tm, n)
    return pl.pallas_call(
        _final_norm_kernel, grid=(n // tm,),
        in_specs=[pl.BlockSpec((tm, d), lambda i: (i, 0)), pl.BlockSpec((1, d), lambda i: (0, 0))],
        out_specs=pl.BlockSpec((tm, d), lambda i: (i, 0)),
        out_shape=jax.ShapeDtypeStruct((n, d), F32),
        compiler_params=_params(("parallel",)), name="final_norm",
    )(x, g.reshape(1, d))


def _fox_gate_kernel(f_ref, b_ref, lf_ref, cum_ref):
    x = f_ref[0] + b_ref[...]
    lf = jnp.minimum(x, 0.0) - jnp.log1p(jnp.exp(-jnp.abs(x)))
    lf_ref[0] = lf
    n = x.shape[1]
    lane = lax.broadcasted_iota(jnp.int32, x.shape, 1)
    c = lf
    s = 1
    while s < n:
        c = c + jnp.where(lane >= s, pltpu.roll(c, s, axis=1), 0.0)
        s *= 2
    cum_ref[0] = c


def fox_gate(f_t, b_f):
    b, hh, n = f_t.shape
    spec = pl.BlockSpec((1, hh, n), lambda i: (i, 0, 0))
    return pl.pallas_call(
        _fox_gate_kernel, grid=(b,),
        in_specs=[spec, pl.BlockSpec((hh, 1), lambda i: (0, 0))],
        out_specs=[spec, spec], out_shape=[jax.ShapeDtypeStruct(f_t.shape, F32)] * 2,
        compiler_params=_params(("parallel",)), name="fox_gate",
    )(f_t, b_f.reshape(hh, 1))


def _attn_init(m_sc, acc_sc):
    m_sc[...] = jnp.full(m_sc.shape, NEG, F32)
    acc_sc[...] = jnp.zeros(acc_sc.shape, F32)


def _value_rows(v_ref):
    vt = v_ref[0].T.astype(BF16)
    ones = jnp.ones((ONES_ROWS, vt.shape[1]), BF16)
    return [jnp.concatenate([vt[kv * HD:(kv + 1) * HD], ones], axis=0) for kv in range(vt.shape[0] // HD)]


def _attn_step(h, k, vt, q, addends, tq, m_sc, acc_sc):
    s = _nt(k, q)
    if addends is not None:
        pieces = []
        for g, terms in enumerate(addends):
            piece = s[:, g * tq:(g + 1) * tq]
            for term in terms:
                piece = piece + term
            pieces.append(piece)
        s = pieces[0] if len(pieces) == 1 else jnp.concatenate(pieces, axis=1)
    m_prev = m_sc[h]
    m_new = jnp.maximum(m_prev, jnp.max(s, axis=0, keepdims=True))
    acc_sc[h] = jnp.exp2(m_prev - m_new) * acc_sc[h] + _dot(vt, jnp.exp2(s - m_new).astype(BF16))
    m_sc[h] = m_new


def _attn_finish(hq, group, tq, m_sc, acc_sc, ot_sc, o_ref, sink_ref=None, lset_sc=None, lse_ref=None):
    for h in range(hq):
        kv, g = divmod(h, group)
        cols = slice(g * tq, (g + 1) * tq)
        num, l, m = acc_sc[kv, 0:HD, cols], acc_sc[kv, HD:HD + 1, cols], m_sc[kv, :, cols]
        if sink_ref is not None:
            sk = sink_ref[:, h:h + 1] * LOG2E
            m_f = jnp.maximum(m, sk)
            w_old = jnp.exp2(m - m_f)
            l = l * w_old + jnp.exp2(sk - m_f)
            num = num * w_old
            m = m_f
        ot_sc[h * HD:(h + 1) * HD, :] = num / l
        if lset_sc is not None:
            lset_sc[h * HD:(h + 1) * HD, :] = jnp.broadcast_to(m * LN2 + jnp.log(l), num.shape)
    o_ref[0] = ot_sc[...].T
    if lset_sc is not None:
        lse_ref[0] = lset_sc[...].T


def _q_group(q_ref, kv, group):
    heads = [(q_ref[0, :, h * HD:(h + 1) * HD] * (SCALE * LOG2E)).astype(BF16)
             for h in range(kv * group, (kv + 1) * group)]
    return heads[0] if group == 1 else jnp.concatenate(heads, axis=0)


def _band_kernel(*refs, t, nb, n_back, dil, hq, group, with_sink, with_lse):
    refs = list(refs)
    q_ref, k_ref, v_ref = refs[:3]
    pos = 3
    sink_ref = None
    if with_sink:
        sink_ref = refs[pos]
        pos += 1
    o_ref = refs[pos]
    pos += 1
    lse_ref = None
    if with_lse:
        lse_ref = refs[pos]
        pos += 1
    m_sc, acc_sc, ot_sc = refs[pos:pos + 3]
    lset_sc = refs[pos + 3] if with_lse else None
    qi, j = pl.program_id(1), pl.program_id(2)
    kt = qi - nb + j

    @pl.when(j == 0)
    def _():
        _attn_init(m_sc, acc_sc)

    @pl.when(kt >= 0)
    def _():
        dist = (qi * t + lax.broadcasted_iota(jnp.int32, (t, t), 1)) - (kt * t + lax.broadcasted_iota(jnp.int32, (t, t), 0))
        band = jnp.where(dist >= 0, dist, n_back + 1)
        if dil > 1:
            band = jnp.where((dist & (dil - 1)) == 0, band, n_back + 1)
        addends = [[jnp.where(band <= n_back, 0.0, NEG)]] * group
        vts = _value_rows(v_ref)
        for kv in range(hq // group):
            k = k_ref[0, :, kv * HD:(kv + 1) * HD].astype(BF16)
            _attn_step(kv, k, vts[kv], _q_group(q_ref, kv, group), addends, t, m_sc, acc_sc)

    @pl.when(j == nb)
    def _():
        _attn_finish(hq, group, t, m_sc, acc_sc, ot_sc, o_ref, sink_ref, lset_sc, lse_ref)


def band_attention(q, kv, n_back, dil, t, sink=None, with_lse=False):
    b, l, cq = q.shape
    ck = kv.shape[2] // 2
    nb = -(-n_back // t)
    hq, hkv = cq // HD, ck // HD
    kmap = lambda blk: (lambda bi, qi, j: (bi, jnp.maximum(qi - nb + j, 0), blk))
    in_specs = [pl.BlockSpec((1, t, cq), lambda bi, qi, j: (bi, qi, 0)),
                pl.BlockSpec((1, t, ck), kmap(0)), pl.BlockSpec((1, t, ck), kmap(1))]
    args = [q, kv, kv]
    if sink is not None:
        in_specs.append(pl.BlockSpec((1, hq), lambda bi, qi, j: (0, 0)))
        args.append(sink.reshape(1, hq))
    o_spec = pl.BlockSpec((1, t, cq), lambda bi, qi, j: (bi, qi, 0))
    o_shape = jax.ShapeDtypeStruct(q.shape, F32)
    wide = hq // hkv * t
    scratch = [pltpu.VMEM((hkv, 1, wide), F32), pltpu.VMEM((hkv, HD + ONES_ROWS, wide), F32), pltpu.VMEM((cq, t), F32)]
    if with_lse:
        scratch.append(pltpu.VMEM((cq, t), F32))
    return pl.pallas_call(
        functools.partial(_band_kernel, t=t, nb=nb, n_back=n_back, dil=dil, hq=hq, group=hq // hkv,
                          with_sink=sink is not None, with_lse=with_lse),
        grid=(b, l // t, nb + 1),
        in_specs=in_specs,
        out_specs=[o_spec, o_spec] if with_lse else o_spec,
        out_shape=[o_shape, o_shape] if with_lse else o_shape,
        scratch_shapes=scratch,
        compiler_params=_params(("parallel", "parallel", "arbitrary")), name="band_attention",
    )(*args)


def _dilated_kernel(q_ref, k_ref, v_ref, o_ref, lse_ref, m_sc, acc_sc, *, dil, sub, n_back):
    qi, j = pl.program_id(2), pl.program_id(3)
    kt = qi - 1 + j

    @pl.when(j == 0)
    def _():
        _attn_init(m_sc, acc_sc)

    @pl.when(kt >= 0)
    def _():
        dist = (qi * sub + lax.broadcasted_iota(jnp.int32, (sub, sub), 1)) - (kt * sub + lax.broadcasted_iota(jnp.int32, (sub, sub), 0))
        addends = [[jnp.where(jnp.where(dist >= 0, dist, n_back + 1) <= n_back, 0.0, NEG)]]
        ones = jnp.ones((ONES_ROWS, sub), BF16)
        for r in range(dil):
            rows = pl.ds(r, sub, stride=dil)
            q = (q_ref[0, rows, :] * (SCALE * LOG2E)).astype(BF16)
            k = k_ref[0, rows, :].astype(BF16)
            vt = v_ref[0, rows, :].T.astype(BF16)
            for h in range(LANE // HD):
                cols = slice(h * HD, (h + 1) * HD)
                _attn_step(r * (LANE // HD) + h, k[:, cols], jnp.concatenate([vt[cols], ones], axis=0), q[:, cols],
                           addends, sub, m_sc, acc_sc)

    @pl.when(j == 1)
    def _():
        for r in range(dil):
            outs, lses = [], []
            for h in range(LANE // HD):
                slot = r * (LANE // HD) + h
                num, l = acc_sc[slot, 0:HD], acc_sc[slot, HD:HD + 1]
                outs.append(num / l)
                lses.append(jnp.broadcast_to(m_sc[slot] * LN2 + jnp.log(l), num.shape))
            rows = pl.ds(r, sub, stride=dil)
            o_ref[0, rows, :] = jnp.concatenate(outs, axis=0).T
            lse_ref[0, rows, :] = jnp.concatenate(lses, axis=0).T


def dilated_attention(q, kv, window, dil):
    b, l, cq = q.shape
    sub = window // dil
    n_pair = cq // LANE
    kmap = lambda off: (lambda bi, c, qi, j: (bi, jnp.maximum(qi - 1 + j, 0), off + c))
    spec = pl.BlockSpec((1, window, LANE), lambda bi, c, qi, j: (bi, qi, c))
    slots = dil * (LANE // HD)
    return pl.pallas_call(
        functools.partial(_dilated_kernel, dil=dil, sub=sub, n_back=sub),
        grid=(b, n_pair, l // window, 2),
        in_specs=[spec, pl.BlockSpec((1, window, LANE), kmap(0)), pl.BlockSpec((1, window, LANE), kmap(n_pair))],
        out_specs=[spec, spec], out_shape=[jax.ShapeDtypeStruct(q.shape, F32)] * 2,
        scratch_shapes=[pltpu.VMEM((slots, 1, sub), F32), pltpu.VMEM((slots, HD + ONES_ROWS, sub), F32)],
        compiler_params=_params(("parallel", "parallel", "parallel", "arbitrary")), name="dilated_attention",
    )(q, kv, kv)


def _flash_kernel(qt_ref, kt_ref, *refs, t, hq, group, mode):
    if mode == "fox":
        q_ref, k_ref, v_ref, cq_ref, ck_ref, o_ref, m_sc, acc_sc, ot_sc = refs
    else:
        q_ref, k_ref, v_ref, sel_ref, o_ref, m_sc, acc_sc, ot_sc = refs
    step = pl.program_id(2)
    qi, kj = qt_ref[step], kt_ref[step]

    @pl.when(kj == 0)
    def _():
        _attn_init(m_sc, acc_sc)

    def body(diagonal):
        vts = _value_rows(v_ref)
        causal = lax.broadcasted_iota(jnp.int32, (t, t), 1) >= lax.broadcasted_iota(jnp.int32, (t, t), 0)
        if mode == "sel":
            selb = sel_ref[0].astype(BF16)
            width = selb.shape[1]
            n_blk = width // (hq // group)
            key_blk = (kj * t + lax.broadcasted_iota(jnp.int32, (t, width), 0)) // SEL_BLOCK
            col = lax.broadcasted_iota(jnp.int32, (t, width), 1)
        for kv in range(hq // group):
            k = k_ref[0, :, kv * HD:(kv + 1) * HD].astype(BF16)
            if mode == "fox":
                tail = [jnp.where(causal, 0.0, NEG)] if diagonal else []
                addends = [[cq_ref[0, 0, h:h + 1, :] * LOG2E, ck_ref[0, 0, :, h:h + 1] * (-LOG2E)] + tail
                           for h in range(kv * group, (kv + 1) * group)]
            else:
                expand = jnp.where(col == kv * n_blk + key_blk, 1.0, 0.0).astype(BF16)
                flag = _nt(expand, selb)
                if diagonal:
                    flag = jnp.where(causal, flag, 0.0)
                addends = [[jnp.where(flag > 0.5, 0.0, NEG)]] * group
            _attn_step(kv, k, vts[kv], _q_group(q_ref, kv, group), addends, t, m_sc, acc_sc)

    @pl.when(kj < qi)
    def _():
        body(False)

    @pl.when(kj == qi)
    def _():
        body(True)
        _attn_finish(hq, group, t, m_sc, acc_sc, ot_sc, o_ref)


def flash_attention(q, kv, t, cq, n_col, mode, extra):
    b, l, _ = q.shape
    hq = cq // HD
    nq = l // t
    pairs = [(qi, kj) for qi in range(nq) for kj in range(qi + 1)]
    qt = jnp.asarray([p[0] for p in pairs], jnp.int32)
    kt = jnp.asarray([p[1] for p in pairs], jnp.int32)
    kmap = lambda off: (lambda bi, c, s, qt, kt: (bi, kt[s], off + c))
    in_specs = [pl.BlockSpec((1, t, cq), lambda bi, c, s, qt, kt: (bi, qt[s], c)),
                pl.BlockSpec((1, t, LANE), kmap(0)), pl.BlockSpec((1, t, LANE), kmap(n_col))]
    if mode == "fox":
        in_specs += [pl.BlockSpec((1, 1, hq, t), lambda bi, c, s, qt, kt: (bi, c, 0, qt[s])),
                     pl.BlockSpec((1, 1, t, hq), lambda bi, c, s, qt, kt: (bi, c, kt[s], 0))]
    else:
        in_specs += [pl.BlockSpec((1, t, extra[0].shape[2]), lambda bi, c, s, qt, kt: (bi, qt[s], 0))]
    return pl.pallas_call(
        functools.partial(_flash_kernel, t=t, hq=hq, group=hq // 2, mode=mode),
        grid_spec=pltpu.PrefetchScalarGridSpec(
            num_scalar_prefetch=2, grid=(b, n_col, len(pairs)), in_specs=in_specs,
            out_specs=pl.BlockSpec((1, t, cq), lambda bi, c, s, qt, kt: (bi, qt[s], c)),
            scratch_shapes=[pltpu.VMEM((2, 1, hq // 2 * t), F32), pltpu.VMEM((2, HD + ONES_ROWS, hq // 2 * t), F32),
                            pltpu.VMEM((cq, t), F32)]),
        out_shape=jax.ShapeDtypeStruct(q.shape, F32),
        compiler_params=_params(("parallel", "parallel", "arbitrary")), name="flash_" + mode,
    )(qt, kt, q, kv, kv, *extra)


def _compress_kernel(*refs, n_src, chunks, paged):
    refs = refs[1:] if paged else refs
    n_in = n_src if paged else 2
    src = refs[:n_in]
    w1_ref, b1_ref, w2_ref, out_ref, x_sc = refs[n_in:n_in + 5]
    total = n_src * chunks
    if paged:
        tok_sc = refs[n_in + 5]
    for i in range(n_src):
        for j in range(2):
            if paged:
                tok_sc[j] = src[i][0, j * LANE:(j + 1) * LANE, :].T
            for s in range(CMP_STRIDE):
                if paged:
                    rows = tok_sc[j, pl.ds(s, chunks, stride=CMP_STRIDE), :]
                else:
                    rows = src[j][0, pl.ds(s, chunks, stride=CMP_STRIDE), :]
                x_sc[j, i * chunks:(i + 1) * chunks, s * LANE:(s + 1) * LANE] = rows
    for j in range(2):
        parts = _dot(x_sc[j].astype(BF16), w1_ref[j])
        half = parts.shape[1] // 2
        pre = b1_ref[j] + parts[:, :half] + pltpu.roll(parts[:, half:], total - 1, axis=0)
        act = pre * jax.nn.sigmoid(pre)
        out_ref[0, :, j * LANE:(j + 1) * LANE] = _dot(act.astype(BF16), w2_ref[j])


def compress(x, page_table, w1p, b1p, w2p):
    const = lambda nd: (lambda *_: (0,) * nd)
    w_specs = [pl.BlockSpec(w1p.shape, const(3)), pl.BlockSpec(b1p.shape, const(3)), pl.BlockSpec(w2p.shape, const(3))]
    if page_table is None:
        b, l, width = x.shape
        n_src, chunks = 1, l // CMP_STRIDE
        scratch = [pltpu.VMEM((2, chunks, CMP_STRIDE * width // 2), F32)]
        halves = [pl.BlockSpec((1, l, LANE), lambda i: (i, 0, 0)), pl.BlockSpec((1, l, LANE), lambda i: (i, 0, 1))]
        grid_spec = pl.GridSpec(
            grid=(b,), in_specs=halves + w_specs,
            out_specs=pl.BlockSpec((1, chunks, width), lambda i: (i, 0, 0)), scratch_shapes=scratch)
        args = (x, x, w1p, b1p, w2p)
    else:
        b, n_src = page_table.shape
        _, width, rows = x.shape
        chunks = rows // CMP_STRIDE
        scratch = [pltpu.VMEM((2, n_src * chunks, CMP_STRIDE * width // 2), F32), pltpu.VMEM((2, rows, LANE), F32)]
        page_spec = lambda p: pl.BlockSpec((1, width, rows), lambda i, pt: (pt[i, p], 0, 0))
        grid_spec = pltpu.PrefetchScalarGridSpec(
            num_scalar_prefetch=1, grid=(b,), in_specs=[page_spec(p) for p in range(n_src)] + w_specs,
            out_specs=pl.BlockSpec((1, n_src * chunks, width), lambda i, pt: (i, 0, 0)), scratch_shapes=scratch)
        args = (page_table,) + (x,) * n_src + (w1p, b1p, w2p)
    return pl.pallas_call(
        functools.partial(_compress_kernel, n_src=n_src, chunks=chunks, paged=page_table is not None),
        grid_spec=grid_spec,
        out_shape=jax.ShapeDtypeStruct((b, n_src * chunks, width), F32),
        compiler_params=_params(("arbitrary",)), name="compress",
    )(*args)


def _compress_weights(phi_w1, phi_b1, phi_w2, n_kv):
    hidden = phi_w1.shape[-1]
    parts = CMP_BLOCK // CMP_STRIDE
    eye = jnp.eye(n_kv, dtype=F32)
    w1 = phi_w1.reshape(2, parts, CMP_STRIDE, HD, hidden)
    w1p = jnp.einsum("jrsdh,kl->jskdrlh", w1, eye).reshape(2, CMP_STRIDE * n_kv * HD, parts * n_kv * hidden)
    b1p = jnp.tile(phi_b1[:, None, :], (1, 1, n_kv))
    w2p = jnp.einsum("jhd,kl->jkhld", phi_w2, eye).reshape(2, n_kv * hidden, n_kv * HD)
    return w1p.astype(BF16), b1p, w2p.astype(BF16)


def _top_blocks(imp, n_take):
    lane = lax.broadcasted_iota(jnp.int32, imp.shape, 1).astype(F32)
    sel = jnp.zeros(imp.shape, F32)
    cur = imp
    for _ in range(n_take):
        mx = jnp.max(cur, axis=1, keepdims=True)
        first = jnp.min(jnp.where(cur == mx, lane, 1e9), axis=1, keepdims=True)
        pick = lane == first
        sel = jnp.where(pick, 1.0, sel)
        cur = jnp.where(pick, DEAD, cur)
    return sel


def _cmp_prompt_kernel(q_ref, kvc_ref, cover_ref, o_ref, sel_ref, *, t, n_cmp, n_kv, group, n_blk):
    qi = pl.program_id(1)
    rows = kvc_ref.shape[1]
    qpos = qi * t + lax.broadcasted_iota(jnp.int32, (t, 1), 0)
    ci = lax.broadcasted_iota(jnp.int32, (1, rows), 1)
    valid = (ci * CMP_STRIDE + CMP_BLOCK - 1 <= qpos) & (ci < n_cmp)
    blk = lax.broadcasted_iota(jnp.int32, (1, n_blk), 1)
    cur = qpos // SEL_BLOCK
    forced = (blk == 0) | (blk == cur) | (blk == cur - 1)
    cover = cover_ref[...]
    for kv in range(n_kv):
        kc = kvc_ref[0, :, kv * HD:(kv + 1) * HD].astype(BF16)
        vc = kvc_ref[0, :, (n_kv + kv) * HD:(n_kv + kv + 1) * HD].astype(BF16)
        imp = jnp.zeros((t, n_blk), F32)
        for g in range(group):
            h = kv * group + g
            s = jnp.where(valid, _nt(q_ref[0, :, h * HD:(h + 1) * HD].astype(BF16), kc) * SCALE, NEG)
            m = jnp.max(s, axis=1, keepdims=True)
            e = jnp.where(valid, jnp.exp(s - m), 0.0)
            den = jnp.sum(e, axis=1, keepdims=True)
            p = (e / jnp.where(den > 0, den, 1.0)).astype(BF16)
            o_ref[0, :, h * HD:(h + 1) * HD] = _dot(p, vc)
            imp = imp + _dot(p, cover)
        imp = jnp.where(blk * SEL_BLOCK <= qpos, jnp.where(forced, FORCE_SCORE, imp), NEG)
        sel_ref[0, :, kv * n_blk:(kv + 1) * n_blk] = _top_blocks(imp, min(N_SEL, n_blk))


def _cover_matrix(rows, n_cmp, n_blk, width):
    c_start = np.arange(rows)[:, None] * CMP_STRIDE
    s_start = np.arange(width)[None, :] * SEL_BLOCK
    cov = (c_start < s_start + SEL_BLOCK) & (c_start + CMP_BLOCK > s_start)
    cov &= (np.arange(rows)[:, None] < n_cmp) & (np.arange(width)[None, :] < n_blk)
    return jnp.asarray(cov, dtype=BF16)


def cmp_attend_prompt(q, kvc, t, n_kv):
    b, l, cq = q.shape
    rows = kvc.shape[1]
    n_blk = l // SEL_BLOCK
    cover = _cover_matrix(rows, rows - 1, n_blk, n_blk)
    return pl.pallas_call(
        functools.partial(_cmp_prompt_kernel, t=t, n_cmp=rows - 1, n_kv=n_kv, group=cq // HD // n_kv, n_blk=n_blk),
        grid=(b, l // t),
        in_specs=[pl.BlockSpec((1, t, cq), lambda bi, qi: (bi, qi, 0)),
                  pl.BlockSpec((1, rows, kvc.shape[2]), lambda bi, qi: (bi, 0, 0)),
                  pl.BlockSpec(cover.shape, lambda bi, qi: (0, 0))],
        out_specs=[pl.BlockSpec((1, t, cq), lambda bi, qi: (bi, qi, 0)),
                   pl.BlockSpec((1, t, n_kv * n_blk), lambda bi, qi: (bi, qi, 0))],
        out_shape=[jax.ShapeDtypeStruct(q.shape, F32), jax.ShapeDtypeStruct((b, l, n_kv * n_blk), F32)],
        compiler_params=_params(("parallel", "parallel")), name="cmp_attend_prompt",
    )(q, kvc, cover)


def _diag_heads(x, rows_kv, n_kv):
    out = jnp.zeros((x.shape[0], HD), F32)
    for kv in range(n_kv):
        out = out + jnp.where(rows_kv == kv, x[:, kv * HD:(kv + 1) * HD], 0.0)
    return out


def _cmp_sample_kernel(q_ref, kvc_ref, cover_ref, o_ref, sel_ref, *, n_tok, heads, n_kv, n_cmp, past_len, n_blk):
    rows = kvc_ref.shape[1]
    r_all = n_tok * heads
    c = n_kv * HD
    width = cover_ref.shape[1]
    group = heads // n_kv
    ridx = lax.broadcasted_iota(jnp.int32, (r_all, 1), 0)
    qpos = past_len + ridx // heads
    ci = lax.broadcasted_iota(jnp.int32, (1, rows), 1)
    valid = (ci * CMP_STRIDE + CMP_BLOCK - 1 <= qpos) & (ci < n_cmp)
    g_rows = r_all // group
    n_seq = q_ref.shape[0]
    imps = []
    for i in range(n_seq):
        kc = kvc_ref[i, :, 0:c].astype(BF16)
        vc = kvc_ref[i, :, c:2 * c].astype(BF16)
        s = jnp.where(valid, _nt(q_ref[i].astype(BF16), kc) * SCALE, NEG)
        m = jnp.max(s, axis=1, keepdims=True)
        e = jnp.where(valid, jnp.exp(s - m), 0.0)
        den = jnp.sum(e, axis=1, keepdims=True)
        p = (e / jnp.where(den > 0, den, 1.0)).astype(BF16)
        o_ref[i] = _diag_heads(_dot(p, vc), (ridx % heads) // group, n_kv)
        imps.append(jnp.sum(_dot(p, cover_ref[...]).reshape(g_rows, group, width), axis=1))
    imp = jnp.concatenate(imps, axis=0)
    gpos = past_len + (lax.broadcasted_iota(jnp.int32, (n_seq * g_rows, 1), 0) % g_rows) // n_kv
    blk = lax.broadcasted_iota(jnp.int32, (1, width), 1)
    cur = gpos // SEL_BLOCK
    forced = (blk == 0) | (blk == cur) | (blk == cur - 1)
    imp = jnp.where(blk * SEL_BLOCK <= gpos, jnp.where(forced, FORCE_SCORE, imp), NEG)
    imp = jnp.where(blk < n_blk, imp, DEAD)
    sel = _top_blocks(imp, min(N_SEL, n_blk))
    for i in range(n_seq):
        sel_ref[i] = sel[i * g_rows:(i + 1) * g_rows]


def cmp_attend_sample(qbd, kvc, n_tok, heads, n_kv, past_len):
    b, r_all, c = qbd.shape
    rows = kvc.shape[1]
    n_blk = past_len // SEL_BLOCK + -(-n_tok // SEL_BLOCK)
    width = -(-n_blk // LANE) * LANE
    cover = _cover_matrix(rows, rows - 1, n_blk, width)
    g_rows = r_all // (heads // n_kv)
    n_seq = next(n for n in (8, 4, 2, 1) if b % n == 0)
    return pl.pallas_call(
        functools.partial(_cmp_sample_kernel, n_tok=n_tok, heads=heads, n_kv=n_kv, n_cmp=rows - 1,
                          past_len=past_len, n_blk=n_blk),
        grid=(b // n_seq,),
        in_specs=[pl.BlockSpec((n_seq, r_all, c), lambda i: (i, 0, 0)),
                  pl.BlockSpec((n_seq, rows, 2 * c), lambda i: (i, 0, 0)),
                  pl.BlockSpec(cover.shape, lambda i: (0, 0))],
        out_specs=[pl.BlockSpec((n_seq, r_all, HD), lambda i: (i, 0, 0)),
                   pl.BlockSpec((n_seq, g_rows, width), lambda i: (i, 0, 0))],
        out_shape=[jax.ShapeDtypeStruct((b, r_all, HD), F32), jax.ShapeDtypeStruct((b, g_rows, width), F32)],
        compiler_params=_params(("parallel",)), name="cmp_attend_sample",
    )(qbd, kvc, cover)


def _dec_init(m_sc, l_sc, acc_sc):
    m_sc[...] = jnp.full(m_sc.shape, NEG, F32)
    l_sc[...] = jnp.zeros(l_sc.shape, F32)
    acc_sc[...] = jnp.zeros(acc_sc.shape, F32)


def _dec_update(qb, kt, vt, bias, mask, m_sc, l_sc, acc_sc):
    s = _dot(qb, kt) * SCALE
    if bias is not None:
        s = s + bias
    if mask is not None:
        s = jnp.where(mask, s, NEG)
    m_prev = m_sc[...]
    m_new = jnp.maximum(m_prev, jnp.max(s, axis=1, keepdims=True))
    alpha = jnp.exp(m_prev - m_new)
    p = jnp.exp(s - m_new)
    if mask is not None:
        p = jnp.where(mask, p, 0.0)
    l_sc[...] = alpha * l_sc[...] + jnp.sum(p, axis=1, keepdims=True)
    acc_sc[...] = alpha * acc_sc[...] + _nt(p.astype(BF16), vt)
    m_sc[...] = m_new


def _dec_new_tokens(q, knew, vnew, biases, masks, m_sc, l_sc, acc_sc):
    n_new = knew.shape[0]
    scores = []
    for u in range(n_new):
        s = jnp.sum(q * knew[u:u + 1, :], axis=1, keepdims=True) * SCALE
        if biases is not None:
            s = s + biases[u]
        scores.append(jnp.where(masks[u], s, NEG))
    m_prev = m_sc[...]
    m_new = m_prev
    for s in scores:
        m_new = jnp.maximum(m_new, s)
    alpha = jnp.exp(m_prev - m_new)
    l = alpha * l_sc[...]
    acc = alpha * acc_sc[...]
    for u in range(n_new):
        p = jnp.where(masks[u], jnp.exp(scores[u] - m_new), 0.0)
        l = l + p
        acc = acc + p * vnew[u:u + 1, :]
    l_sc[...] = l
    acc_sc[...] = acc
    m_sc[...] = m_new


def _token_cumsum(x, n_tok, heads):
    parts, run = [], None
    for u in range(n_tok):
        seg = x[u * heads:(u + 1) * heads]
        run = seg if run is None else run + seg
        parts.append(run)
    return jnp.concatenate(parts, axis=0), parts


def _dec_window_kernel(*refs, n_tok, heads, n_kv, lb, n_back, dil, chunk, with_sink, with_lse):
    refs = list(refs)
    q_ref, cache_ref, new_ref = refs[:3]
    pos = 3
    sink_ref = None
    if with_sink:
        sink_ref = refs[pos]
        pos += 1
    o_ref = refs[pos]
    pos += 1
    lse_ref = None
    if with_lse:
        lse_ref = refs[pos]
        pos += 1
    m_sc, l_sc, acc_sc = refs[pos:pos + 3]
    r_all = n_tok * heads
    c = n_kv * HD
    group = heads // n_kv
    ridx = lax.broadcasted_iota(jnp.int32, (r_all, 1), 0)
    tok = ridx // heads
    _dec_init(m_sc, l_sc, acc_sc)
    q = q_ref[0]
    qb = q.astype(BF16)

    def ok(dist):
        good = (dist >= 0) & (dist <= n_back)
        if dil > 1:
            good = good & (dist % dil == 0)
        return good

    for c0 in range(0, lb, chunk):
        kt = cache_ref[0, 0:c, c0:c0 + chunk].astype(BF16)
        vt = cache_ref[0, c:2 * c, c0:c0 + chunk].astype(BF16)
        dist = (lb + tok) - (c0 + lax.broadcasted_iota(jnp.int32, (1, chunk), 1))
        _dec_update(qb, kt, vt, None, ok(dist), m_sc, l_sc, acc_sc)
    new = new_ref[0]
    _dec_new_tokens(q, new[:, 0:c], new[:, c:2 * c], None, [ok(tok - u) for u in range(n_tok)], m_sc, l_sc, acc_sc)
    m, l, acc = m_sc[...], l_sc[...], acc_sc[...]
    if with_sink:
        sk = sink_ref[...]
        m_f = jnp.maximum(m, sk)
        scale_old = jnp.exp(m - m_f)
        l = l * scale_old + jnp.exp(sk - m_f)
        acc = acc * scale_old
        m = m_f
    o_ref[0] = _diag_heads(acc / l, (ridx % heads) // group, n_kv)
    if with_lse:
        lse_ref[0] = jnp.broadcast_to(m + jnp.log(l), (r_all, HD))


def dec_window(qbd, cache_t, new, n_tok, heads, n_kv, n_back, dil, sink_rows=None, with_lse=False):
    b, r_all, c = qbd.shape
    lb = cache_t.shape[2]
    in_specs = [pl.BlockSpec((1, r_all, c), lambda i: (i, 0, 0)),
                pl.BlockSpec((1, 2 * c, lb), lambda i: (i, 0, 0)),
                pl.BlockSpec((1, n_tok, 2 * c), lambda i: (i, 0, 0))]
    args = [qbd, cache_t, new]
    if sink_rows is not None:
        in_specs.append(pl.BlockSpec((r_all, 1), lambda i: (0, 0)))
        args.append(sink_rows)
    o_spec = pl.BlockSpec((1, r_all, HD), lambda i: (i, 0, 0))
    o_shape = jax.ShapeDtypeStruct((b, r_all, HD), F32)
    return pl.pallas_call(
        functools.partial(_dec_window_kernel, n_tok=n_tok, heads=heads, n_kv=n_kv, lb=lb, n_back=n_back, dil=dil,
                          chunk=min(lb, 512), with_sink=sink_rows is not None, with_lse=with_lse),
        grid=(b,), in_specs=in_specs,
        out_specs=[o_spec, o_spec] if with_lse else o_spec,
        out_shape=[o_shape, o_shape] if with_lse else o_shape,
        scratch_shapes=[pltpu.VMEM((r_all, 1), F32), pltpu.VMEM((r_all, 1), F32), pltpu.VMEM((r_all, c), F32)],
        compiler_params=_params(("parallel",)), name="dec_window",
    )(*args)


def _dec_paged_kernel(*refs, n_pg, n_tok, heads, n_kv, mode):
    pages = refs[1:1 + n_pg]
    pos = 1 + n_pg
    lf_pages = None
    if mode == "fox":
        lf_pages = refs[pos:pos + n_pg]
        pos += n_pg
    q_ref, new_ref, x_ref = refs[pos:pos + 3]
    pos += 3
    if mode == "fox":
        tri_ref = refs[pos]
        pos += 1
    o_ref = refs[pos]
    pos += 1
    k_sc, v_sc, m_sc, l_sc, acc_sc = refs[pos:pos + 5]
    pos += 5
    if mode == "fox":
        c_sc, carry_sc = refs[pos:pos + 2]
    gi = pl.program_id(1)
    r_all = n_tok * heads
    c = n_kv * HD
    group = heads // n_kv
    tk = n_pg * PAGE
    ridx = lax.broadcasted_iota(jnp.int32, (r_all, 1), 0)
    tok = ridx // heads
    head = ridx % heads

    @pl.when(gi == 0)
    def _():
        _dec_init(m_sc, l_sc, acc_sc)
        if mode == "fox":
            carry_sc[...] = jnp.zeros(carry_sc.shape, F32)

    for i in range(n_pg):
        k_sc[:, i * PAGE:(i + 1) * PAGE] = pages[i][0, 0:c, :].astype(BF16)
        v_sc[:, i * PAGE:(i + 1) * PAGE] = pages[i][0, c:2 * c, :].astype(BF16)
    q = q_ref[0]
    qb = q.astype(BF16)

    if mode == "fox":
        tri = tri_ref[...]
        hi, mid, lo = _split3(jnp.concatenate([lf_pages[i][0] for i in range(n_pg)], axis=0))
        within = _dot(hi, tri) + _dot(mid, tri) + _dot(lo, tri)
        carry = carry_sc[...]
        for i in range(n_pg):
            page = within[i * heads:(i + 1) * heads]
            c_sc[:, i * PAGE:(i + 1) * PAGE] = page + carry
            carry = carry + page[:, PAGE - 1:PAGE]
        carry_sc[...] = carry
        bias = _token_cumsum(x_ref[0], n_tok, heads)[0] - jnp.concatenate([c_sc[...]] * n_tok, axis=0)
        _dec_update(qb, k_sc[...], v_sc[...], bias, None, m_sc, l_sc, acc_sc)
    else:
        flags = x_ref[0].astype(BF16)
        width = flags.shape[1]
        key_blk = (gi * tk + lax.broadcasted_iota(jnp.int32, (width, tk), 1)) // SEL_BLOCK
        expand = jnp.where(lax.broadcasted_iota(jnp.int32, (width, tk), 0) == key_blk, 1.0, 0.0).astype(BF16)
        _dec_update(qb, k_sc[...], v_sc[...], None, _dot(flags, expand) > 0.5, m_sc, l_sc, acc_sc)

    @pl.when(gi == pl.num_programs(1) - 1)
    def _():
        new = new_ref[0]
        causal = [tok >= u for u in range(n_tok)]
        if mode == "fox":
            fq, per_tok = _token_cumsum(x_ref[0], n_tok, heads)
            total = jnp.concatenate([carry_sc[...]] * n_tok, axis=0)
            biases = [fq - jnp.concatenate([per_tok[u]] * n_tok, axis=0) - total for u in range(n_tok)]
            _dec_new_tokens(q, new[:, 0:c], new[:, c:2 * c], biases, causal, m_sc, l_sc, acc_sc)
        else:
            n_past_blk = (pl.num_programs(1) * tk) // SEL_BLOCK
            flags = x_ref[0]
            lane = lax.broadcasted_iota(jnp.int32, flags.shape, 1)
            new_on = jnp.sum(jnp.where(lane == n_past_blk, flags, 0.0), axis=1, keepdims=True) > 0.5
            _dec_new_tokens(q, new[:, 0:c], new[:, c:2 * c], None, [m & new_on for m in causal], m_sc, l_sc, acc_sc)
        o_ref[0] = _diag_heads(acc_sc[...] / l_sc[...], head // group, n_kv)


def dec_paged(qbd, pool_t, page_table, new, x, n_tok, heads, n_kv, mode, n_pg, lf_pool_t=None):
    b, r_all, c = qbd.shape
    n_pages = page_table.shape[1]
    page_spec = lambda p, rows: pl.BlockSpec((1, rows, PAGE), lambda i, g, pt: (pt[i, g * n_pg + p], 0, 0))
    in_specs = [page_spec(p, 2 * c) for p in range(n_pg)]
    args = [pool_t] * n_pg
    if mode == "fox":
        in_specs += [page_spec(p, heads) for p in range(n_pg)]
        args += [lf_pool_t] * n_pg
    in_specs += [pl.BlockSpec((1, r_all, c), lambda i, g, pt: (i, 0, 0)),
                 pl.BlockSpec((1, n_tok, 2 * c), lambda i, g, pt: (i, 0, 0)),
                 pl.BlockSpec((1, r_all, x.shape[2]), lambda i, g, pt: (i, 0, 0))]
    args += [qbd, new, x]
    scratch = [pltpu.VMEM((c, n_pg * PAGE), BF16), pltpu.VMEM((c, n_pg * PAGE), BF16),
               pltpu.VMEM((r_all, 1), F32), pltpu.VMEM((r_all, 1), F32), pltpu.VMEM((r_all, c), F32)]
    if mode == "fox":
        tri = jnp.asarray(np.triu(np.ones((PAGE, PAGE), np.float32)), dtype=BF16)
        in_specs.append(pl.BlockSpec((PAGE, PAGE), lambda i, g, pt: (0, 0)))
        args.append(tri)
        scratch += [pltpu.VMEM((heads, n_pg * PAGE), F32), pltpu.VMEM((heads, 1), F32)]
    return pl.pallas_call(
        functools.partial(_dec_paged_kernel, n_pg=n_pg, n_tok=n_tok, heads=heads, n_kv=n_kv, mode=mode),
        grid_spec=pltpu.PrefetchScalarGridSpec(
            num_scalar_prefetch=1, grid=(b, n_pages // n_pg), in_specs=in_specs,
            out_specs=pl.BlockSpec((1, r_all, HD), lambda i, g, pt: (i, 0, 0)), scratch_shapes=scratch),
        out_shape=jax.ShapeDtypeStruct((b, r_all, HD), F32),
        compiler_params=_params(("parallel", "arbitrary")), name="dec_paged_" + mode,
    )(page_table, *args)


def _block_diag_q(q, n_tok, heads, n_kv):
    b = q.shape[0] // n_tok
    own = jnp.asarray(np.arange(heads)[:, None] // (heads // n_kv) == np.arange(n_kv)[None, :], dtype=F32)
    qb = q.reshape(b, n_tok, heads, 1, HD) * own[None, None, :, :, None]
    return qb.reshape(b, n_tok * heads, n_kv * HD)


def _rows_last(cache):
    n, rows = cache.shape[:2]
    return cache.transpose(0, 2, 3, 4, 1).reshape(n, -1, rows)


def _mixer_a(hp, hs, g, w_in, w_out, phi, caches, page_table, dims, ropes):
    bsz, seq, dec_b, dec_t, past_len = dims
    (cos_p, sin_p), (cos_s, sin_s) = ropes
    cache_cmp, cache_sel, cache_win = caches
    heads, n_kv = 16, 2
    segs = [(0, 1024, 0, 1024, True), (1024, 1280, 0, 0, False), (1280, 1536, 0, 128, False),
            (1536, 1792, 0, 128, False), (1792, 1840, 0, 0, False)]
    w1p, b1p, w2p = phi
    q_cmp, q_rot, cmp_raw, sel_kv, win_kv, gates = norm_proj(hp, g, w_in, cos_p, sin_p, segs, 512)
    n = bsz * seq
    kvc = compress(cmp_raw.reshape(bsz, seq, -1), None, w1p, b1p, w2p)
    o_cmp, flags = cmp_attend_prompt(q_cmp.reshape(bsz, seq, -1), kvc, 512, n_kv)
    q3 = q_rot.reshape(bsz, seq, -1)
    o_sel = flash_attention(q3, sel_kv.reshape(bsz, seq, -1), 256, 1024, 1, "sel", (flags,))
    o_win = band_attention(q3, win_kv.reshape(bsz, seq, -1), A_WINDOW - 1, 1, 256)
    hp = merge_a(o_cmp.reshape(n, -1), o_sel.reshape(n, -1), o_win.reshape(n, -1), gates, hp, w_out)
    kv5 = lambda z, b, t: z.reshape(b, t, 2, n_kv, HD)
    w = min(A_WINDOW, seq)
    outs_p = (kv5(cmp_raw, bsz, seq), kv5(sel_kv, bsz, seq), kv5(win_kv, bsz, seq)[:, -w:])
    q_cmp, q_rot, cmp_raw, sel_kv, win_kv, gates = norm_proj(hs, g, w_in, cos_s, sin_s, segs, 512)
    ns = dec_b * dec_t
    kvc = compress(_rows_last(cache_cmp), page_table, w1p, b1p, w2p)
    o_cmp, flags = cmp_attend_sample(_block_diag_q(q_cmp, dec_t, heads, n_kv), kvc, dec_t, heads, n_kv, past_len)
    qbd = _block_diag_q(q_rot, dec_t, heads, n_kv)
    flag_rows = jnp.repeat(flags.reshape(dec_b, dec_t, n_kv, 1, -1), heads // n_kv, axis=3).reshape(dec_b, dec_t * heads, -1)
    o_sel = dec_paged(qbd, _rows_last(cache_sel), page_table, sel_kv.reshape(dec_b, dec_t, -1), flag_rows,
                      dec_t, heads, n_kv, "sel", 16)
    o_win = dec_window(qbd, _rows_last(cache_win), win_kv.reshape(dec_b, dec_t, -1), dec_t, heads, n_kv, A_WINDOW - 1, 1)
    hs = merge_a(o_cmp.reshape(ns, -1), o_sel.reshape(ns, -1), o_win.reshape(ns, -1), gates, hs, w_out)
    outs_s = (kv5(cmp_raw, dec_b, dec_t), kv5(sel_kv, dec_b, dec_t), kv5(win_kv, dec_b, dec_t))
    return hp, hs, outs_p, outs_s


def _mixer_b(hp, hs, g, w_in, sinks, w_out, cache_win, dims, ropes):
    bsz, seq, dec_b, dec_t, _ = dims
    (cos_p, sin_p), (cos_s, sin_s) = ropes
    heads, n_kv = 16, 2
    segs = [(0, 1024, 0, 1024, False), (1024, 1280, 0, 128, False)]
    q, kv = norm_proj(hp, g, w_in, cos_p, sin_p, segs, 512)
    o = band_attention(q.reshape(bsz, seq, -1), kv.reshape(bsz, seq, -1), B_WINDOW - 1, 1, 256, sink=sinks)
    hp = out_proj(o.reshape(bsz * seq, -1), hp, w_out)
    w = min(B_WINDOW, seq)
    out_p = kv.reshape(bsz, seq, 2, n_kv, HD)[:, -w:]
    q, kv = norm_proj(hs, g, w_in, cos_s, sin_s, segs, 512)
    sink_rows = jnp.tile(sinks.astype(F32), dec_t).reshape(dec_t * heads, 1)
    o = dec_window(_block_diag_q(q, dec_t, heads, n_kv), _rows_last(cache_win), kv.reshape(dec_b, dec_t, -1),
                   dec_t, heads, n_kv, B_WINDOW - 1, 1, sink_rows=sink_rows)
    hs = out_proj(o.reshape(dec_b * dec_t, -1), hs, w_out)
    return hp, hs, out_p, kv.reshape(dec_b, dec_t, 2, n_kv, HD)


def _mixer_c(hp, hs, g, w_in, b_f, w_out, cache_kv, cache_logf, page_table, dims, ropes):
    bsz, seq, dec_b, dec_t, _ = dims
    (cos_p, sin_p), (cos_s, sin_s) = ropes
    heads, n_kv = 16, 4
    segs = [(0, 1024, 0, 0, False), (1024, 1536, 0, 0, False), (1536, 1552, 0, 0, False)]
    q, kv, f = norm_proj(hp, g, w_in, cos_p, sin_p, segs, 512)
    lf_t, cum_t = fox_gate(f.reshape(bsz, seq, heads).transpose(0, 2, 1), b_f)
    logf_p = lf_t.transpose(0, 2, 1)
    half = heads // 2
    cum_q = cum_t.reshape(bsz, 2, half, seq)
    cum_k = cum_q.transpose(0, 1, 3, 2)
    o = flash_attention(q.reshape(bsz, seq, -1), kv.reshape(bsz, seq, -1), 512, 512, 2, "fox", (cum_q, cum_k))
    hp = out_proj(o.reshape(bsz * seq, -1), hp, w_out)
    out_p = kv.reshape(bsz, seq, 2, n_kv, HD)
    q, kv, f = norm_proj(hs, g, w_in, cos_s, sin_s, segs, 512)
    ns = dec_b * dec_t
    lf_t, _ = fox_gate(f.reshape(1, ns, heads).transpose(0, 2, 1), b_f)
    logf_s = lf_t.transpose(0, 2, 1).reshape(dec_b, dec_t, heads)
    o = dec_paged(_block_diag_q(q, dec_t, heads, n_kv), _rows_last(cache_kv), page_table, kv.reshape(dec_b, dec_t, -1),
                  logf_s.reshape(dec_b, dec_t * heads, 1), dec_t, heads, n_kv, "fox", 16,
                  lf_pool_t=cache_logf.transpose(0, 2, 1))
    hs = out_proj(o.reshape(ns, -1), hs, w_out)
    return hp, hs, (out_p, logf_p), (kv.reshape(dec_b, dec_t, 2, n_kv, HD), logf_s)


def _mixer_d(hp, hs, g, w_in, w_out, caches, dims, ropes):
    bsz, seq, dec_b, dec_t, _ = dims
    (cos_p, sin_p), (cos_s, sin_s) = ropes
    heads = 8
    c = heads * HD
    segs = []
    for grp in range(len(D_GROUPS)):
        base = 3 * grp * c
        segs += [(base, base + c, 0, c, False), (base + c, base + 3 * c, 0, c, False)]
    res = norm_proj(hp, g, w_in, cos_p, sin_p, segs, 256)
    outs, lses, states_p = [], [], []
    for grp, (window, dil) in enumerate(D_GROUPS):
        q, kv = res[2 * grp], res[2 * grp + 1]
        if dil == 1:
            o, lse = band_attention(q.reshape(bsz, seq, c), kv.reshape(bsz, seq, 2 * c), window, 1, 256, with_lse=True)
        else:
            o, lse = dilated_attention(q.reshape(bsz, seq, c), kv.reshape(bsz, seq, 2 * c), window, dil)
        outs.append(o.reshape(bsz * seq, c))
        lses.append(lse.reshape(bsz * seq, c))
        states_p.append(kv.reshape(bsz, seq, 2, heads, HD)[:, -min(window, seq):])
    hp = merge_d(outs, lses, hp, w_out)
    res = norm_proj(hs, g, w_in, cos_s, sin_s, segs, 256)
    outs, lses, states_s = [], [], []
    for grp, (window, dil) in enumerate(D_GROUPS):
        q, kv = res[2 * grp], res[2 * grp + 1]
        o, lse = dec_window(_block_diag_q(q, dec_t, heads, heads), _rows_last(caches[grp]),
                            kv.reshape(dec_b, dec_t, -1), dec_t, heads, heads, window, dil, with_lse=True)
        outs.append(o.reshape(dec_b * dec_t, c))
        lses.append(lse.reshape(dec_b * dec_t, c))
        states_s.append(kv.reshape(dec_b, dec_t, 2, heads, HD))
    hs = merge_d(outs, lses, hs, w_out)
    return hp, hs, states_p, states_s


def kernel(x_prompt, x_sample, cache_a_cmp, cache_a_sel, cache_a_win, cache_b_win, cache_c_kv, cache_c_logf, cache_d1, cache_d2, cache_d3, page_table, norm_mix, norm_ffn, norm_final, a_w_in, a_phi_w1, a_phi_b1, a_phi_w2, a_w_out, b_w_in, b_sinks, b_w_out, c_w_in, c_b_f, c_w_out, d_w_in, d_w_out, ffn_w_gate, ffn_w_up, ffn_w_down):
    bsz, seq, d_model = x_prompt.shape
    dec_b, dec_t, _ = x_sample.shape
    past_len = page_table.shape[1] * PAGE
    dims = (bsz, seq, dec_b, dec_t, past_len)
    ropes = (_rope_tables(jnp.arange(seq)), _rope_tables(past_len + jnp.arange(dec_b * dec_t) % dec_t))
    hp = x_prompt.reshape(bsz * seq, d_model)
    hs = x_sample.reshape(dec_b * dec_t, d_model)
    bf = lambda z: z.astype(BF16)
    phi = _compress_weights(a_phi_w1, a_phi_b1, a_phi_w2, 2)
    depth = norm_mix.shape[0]
    for layer in range(depth):
        kind = layer % 4
        g = norm_mix[layer]
        if kind == 0:
            hp, hs, a_p, a_s = _mixer_a(hp, hs, g, bf(a_w_in), bf(a_w_out), phi,
                                        (cache_a_cmp, cache_a_sel, cache_a_win), page_table, dims, ropes)
        elif kind == 1:
            hp, hs, b_p, b_s = _mixer_b(hp, hs, g, bf(b_w_in), b_sinks, bf(b_w_out), cache_b_win, dims, ropes)
        elif kind == 2:
            hp, hs, c_p, c_s = _mixer_c(hp, hs, g, bf(c_w_in), c_b_f, bf(c_w_out), cache_c_kv, cache_c_logf,
                                        page_table, dims, ropes)
        else:
            hp, hs, d_p, d_s = _mixer_d(hp, hs, g, bf(d_w_in), bf(d_w_out), (cache_d1, cache_d2, cache_d3), dims, ropes)
        wg, wu, wd = bf(ffn_w_gate[layer]), bf(ffn_w_up[layer]), bf(ffn_w_down[layer])
        hp = ffn(hp, norm_ffn[layer], wg, wu, wd)
        hs = ffn(hs, norm_ffn[layer], wg, wu, wd)
    y_p = final_norm(hp, norm_final).reshape(bsz, seq, d_model)
    y_s = final_norm(hs, norm_final).reshape(dec_b, dec_t, d_model)
    return (y_p, y_s, a_p[0], a_s[0], a_p[1], a_s[1], a_p[2], a_s[2], b_p, b_s,
            c_p[0], c_s[0], c_p[1], c_s[1], d_p[0], d_s[0], d_p[1], d_s[1], d_p[2], d_s[2])
```

```python
import functools

import numpy as np
import jax
import jax.numpy as jnp
from jax import lax
from jax.experimental import pallas as pl
from jax.experimental.pallas import tpu as pltpu

F32 = jnp.float32
BF16 = jnp.bfloat16

HD = 64
SCALE = HD ** -0.5
LOG2E = 1.4426950408889634
LN2 = 0.6931471805599453
RMS_EPS = 1e-6
ROPE_THETA = 10000.0
PAGE = 128
CMP_STRIDE = 16
CMP_BLOCK = 32
SEL_BLOCK = 64
N_SEL = 16
A_WINDOW = 512
B_WINDOW = 128
D_GROUPS = ((128, 1), (512, 4), (2048, 16))
FORCE_SCORE = 1e9
NEG = -1e30
DEAD = -3e38
LANE = 128
SUBLANE = 8
ONES_ROWS = 16
VMEM_LIMIT = 52 * 1024 * 1024


def _params(sem, vmem=VMEM_LIMIT):
    return pltpu.CompilerParams(dimension_semantics=sem, vmem_limit_bytes=vmem)


def _nt(a, b):
    return lax.dot_general(a, b, (((1,), (1,)), ((), ())), preferred_element_type=F32)


def _dot(a, b):
    return jnp.dot(a, b, preferred_element_type=F32)


def _split3(x):
    hi = x.astype(BF16)
    r1 = x - hi.astype(F32)
    mid = r1.astype(BF16)
    lo = (r1 - mid.astype(F32)).astype(BF16)
    return hi, mid, lo


def _rms(x, g):
    var = jnp.mean(x * x, axis=-1, keepdims=True)
    return x * lax.rsqrt(var + RMS_EPS) * g


def _rope_chunk(x, cos2, sin2):
    lane = lax.broadcasted_iota(jnp.int32, x.shape, 1)
    first = (lane % HD) < (HD // 2)
    swapped = jnp.where(first, pltpu.roll(x, LANE - HD // 2, axis=1), pltpu.roll(x, HD // 2, axis=1))
    return x * cos2 + swapped * sin2


def _rope_tables(pos):
    half = HD // 2
    inv = ROPE_THETA ** (-jnp.arange(half, dtype=F32) / half)
    ang = pos.astype(F32)[:, None] * inv[None, :]
    c, s = jnp.cos(ang), jnp.sin(ang)
    return jnp.tile(c, (1, 4)), jnp.tile(jnp.concatenate([-s, s], axis=1), (1, 2))


def _norm_proj_kernel(x_ref, g_ref, w_ref, cos_ref, sin_ref, *out_refs, segs):
    xn = _rms(x_ref[...], g_ref[...]).astype(BF16)
    outs = list(out_refs)
    for c0, c1, r0, r1, dup in segs:
        acc = _dot(xn, w_ref[:, c0:c1])
        if dup:
            outs.pop(0)[...] = acc
        o_ref = outs.pop(0)
        if r1 <= r0:
            o_ref[...] = acc
            continue
        if r0 > 0:
            o_ref[:, 0:r0] = acc[:, 0:r0]
        for a in range(r0, r1, LANE):
            o_ref[:, a:a + LANE] = _rope_chunk(acc[:, a:a + LANE], cos_ref[...], sin_ref[...])
        if r1 < c1 - c0:
            o_ref[:, r1:] = acc[:, r1:]


def norm_proj(x, g, w, cos2, sin2, segs, tm):
    n, d = x.shape
    tm = min(tm, n)
    out_shape, out_specs = [], []
    for c0, c1, _, _, dup in segs:
        for _ in range(2 if dup else 1):
            out_shape.append(jax.ShapeDtypeStruct((n, c1 - c0), F32))
            out_specs.append(pl.BlockSpec((tm, c1 - c0), lambda i: (i, 0)))
    return pl.pallas_call(
        functools.partial(_norm_proj_kernel, segs=tuple(segs)),
        grid=(n // tm,),
        in_specs=[pl.BlockSpec((tm, d), lambda i: (i, 0)),
                  pl.BlockSpec((1, d), lambda i: (0, 0)),
                  pl.BlockSpec(w.shape, lambda i: (0, 0)),
                  pl.BlockSpec((tm, LANE), lambda i: (i % (cos2.shape[0] // tm), 0)),
                  pl.BlockSpec((tm, LANE), lambda i: (i % (sin2.shape[0] // tm), 0))],
        out_specs=out_specs, out_shape=out_shape,
        compiler_params=_params(("parallel",)), name="norm_proj",
    )(x, g.reshape(1, d), w, cos2, sin2)


def _out_proj_kernel(o_ref, h_ref, w_ref, out_ref):
    out_ref[...] = h_ref[...] + _dot(o_ref[...].astype(BF16), w_ref[...])


def out_proj(o, h, w, tm=512):
    n, k = o.shape
    d = w.shape[1]
    tm = min(tm, n)
    return pl.pallas_call(
        _out_proj_kernel, grid=(n // tm,),
        in_specs=[pl.BlockSpec((tm, k), lambda i: (i, 0)), pl.BlockSpec((tm, d), lambda i: (i, 0)),
                  pl.BlockSpec(w.shape, lambda i: (0, 0))],
        out_specs=pl.BlockSpec((tm, d), lambda i: (i, 0)),
        out_shape=jax.ShapeDtypeStruct((n, d), F32),
        compiler_params=_params(("parallel",)), name="out_proj",
    )(o, h, w)


def _merge_a_kernel(oc_ref, os_ref, ow_ref, g_ref, h_ref, w_ref, out_ref, o_sc, *, heads):
    gate = jax.nn.sigmoid(g_ref[...])
    for h in range(heads):
        sl = slice(h * HD, (h + 1) * HD)
        o = (gate[:, 3 * h:3 * h + 1] * oc_ref[:, sl] + gate[:, 3 * h + 1:3 * h + 2] * os_ref[:, sl]
             + gate[:, 3 * h + 2:3 * h + 3] * ow_ref[:, sl])
        o_sc[:, sl] = o
    out_ref[...] = h_ref[...] + _dot(o_sc[...].astype(BF16), w_ref[...])


def merge_a(o_cmp, o_sel, o_win, gates, h, w, tm=512):
    n, k = o_cmp.shape
    d = w.shape[1]
    tm = min(tm, n)
    row = lambda c: pl.BlockSpec((tm, c), lambda i: (i, 0))
    return pl.pallas_call(
        functools.partial(_merge_a_kernel, heads=k // HD), grid=(n // tm,),
        in_specs=[row(k), row(k), row(k), row(gates.shape[1]), row(d), pl.BlockSpec(w.shape, lambda i: (0, 0))],
        out_specs=row(d), out_shape=jax.ShapeDtypeStruct((n, d), F32),
        scratch_shapes=[pltpu.VMEM((tm, k), F32)],
        compiler_params=_params(("parallel",)), name="merge_a",
    )(o_cmp, o_sel, o_win, gates, h, w)


def _merge_d_kernel(o1, o2, o3, l1, l2, l3, h_ref, w_ref, out_ref):
    a, b, c = l1[...], l2[...], l3[...]
    m = jnp.maximum(jnp.maximum(a, b), c)
    ea, eb, ec = jnp.exp(a - m), jnp.exp(b - m), jnp.exp(c - m)
    den = ea + eb + ec
    o = (ea / den) * o1[...] + (eb / den) * o2[...] + (ec / den) * o3[...]
    out_ref[...] = h_ref[...] + _dot(o.astype(BF16), w_ref[...])


def merge_d(outs, lses, h, w, tm=512):
    n, k = outs[0].shape
    d = w.shape[1]
    tm = min(tm, n)
    row = lambda c: pl.BlockSpec((tm, c), lambda i: (i, 0))
    return pl.pallas_call(
        _merge_d_kernel, grid=(n // tm,),
        in_specs=[row(k)] * 6 + [row(d), pl.BlockSpec(w.shape, lambda i: (0, 0))],
        out_specs=row(d), out_shape=jax.ShapeDtypeStruct((n, d), F32),
        compiler_params=_params(("parallel",)), name="merge_d",
    )(*outs, *lses, h, w)


def _ffn_kernel(h_ref, g_ref, wg_ref, wu_ref, wd_ref, out_ref, xn_sc, acc_sc):
    j = pl.program_id(1)

    @pl.when(j == 0)
    def _():
        x = h_ref[...]
        xn_sc[...] = _rms(x, g_ref[...]).astype(BF16)
        acc_sc[...] = x

    xn = xn_sc[...]
    a = _dot(xn, wg_ref[...])
    u = _dot(xn, wu_ref[...])
    mid = (a * jax.nn.sigmoid(a) * u).astype(BF16)
    acc_sc[...] += _dot(mid, wd_ref[...])

    @pl.when(j == pl.num_programs(1) - 1)
    def _():
        out_ref[...] = acc_sc[...]


def ffn(h, g, wg, wu, wd, tm=512, n_f=2):
    n, d = h.shape
    f = wg.shape[1]
    tm = min(tm, n)
    tf = f // n_f
    return pl.pallas_call(
        _ffn_kernel, grid=(n // tm, n_f),
        in_specs=[pl.BlockSpec((tm, d), lambda i, j: (i, 0)), pl.BlockSpec((1, d), lambda i, j: (0, 0)),
                  pl.BlockSpec((d, tf), lambda i, j: (0, j)), pl.BlockSpec((d, tf), lambda i, j: (0, j)),
                  pl.BlockSpec((tf, d), lambda i, j: (j, 0))],
        out_specs=pl.BlockSpec((tm, d), lambda i, j: (i, 0)),
        out_shape=jax.ShapeDtypeStruct((n, d), F32),
        scratch_shapes=[pltpu.VMEM((tm, d), BF16), pltpu.VMEM((tm, d), F32)],
        compiler_params=_params(("parallel", "arbitrary")), name="ffn",
    )(h, g.reshape(1, d), wg, wu, wd)


def _final_norm_kernel(x_ref, g_ref, o_ref):
    o_ref[...] = _rms(x_ref[...], g_ref[...])


def final_norm(x, g, tm=1024):
    n, d = x.shape
    tm = min(tm, n)
    return pl.pallas_call(
        _final_norm_kernel, grid=(n // tm,),
        in_specs=[pl.BlockSpec((tm, d), lambda i: (i, 0)), pl.BlockSpec((1, d), lambda i: (0, 0))],
        out_specs=pl.BlockSpec((tm, d), lambda i: (i, 0)),
        out_shape=jax.ShapeDtypeStruct((n, d), F32),
        compiler_params=_params(("parallel",)), name="final_norm",
    )(x, g.reshape(1, d))


def _fox_gate_kernel(f_ref, b_ref, lf_ref, cum_ref):
    x = f_ref[0] + b_ref[...]
    lf = jnp.minimum(x, 0.0) - jnp.log1p(jnp.exp(-jnp.abs(x)))
    lf_ref[0] = lf
    n = x.shape[1]
    lane = lax.broadcasted_iota(jnp.int32, x.shape, 1)
    c = lf
    s = 1
    while s < n:
        c = c + jnp.where(lane >= s, pltpu.roll(c, s, axis=1), 0.0)
        s *= 2
    cum_ref[0] = c


def fox_gate(f_t, b_f):
    b, hh, n = f_t.shape
    spec = pl.BlockSpec((1, hh, n), lambda i: (i, 0, 0))
    return pl.pallas_call(
        _fox_gate_kernel, grid=(b,),
        in_specs=[spec, pl.BlockSpec((hh, 1), lambda i: (0, 0))],
        out_specs=[spec, spec], out_shape=[jax.ShapeDtypeStruct(f_t.shape, F32)] * 2,
        compiler_params=_params(("parallel",)), name="fox_gate",
    )(f_t, b_f.reshape(hh, 1))


def _attn_init(m_sc, acc_sc):
    m_sc[...] = jnp.full(m_sc.shape, NEG, F32)
    acc_sc[...] = jnp.zeros(acc_sc.shape, F32)


def _value_rows(v_ref):
    vt = v_ref[0].T.astype(BF16)
    ones = jnp.ones((ONES_ROWS, vt.shape[1]), BF16)
    return [jnp.concatenate([vt[kv * HD:(kv + 1) * HD], ones], axis=0) for kv in range(vt.shape[0] // HD)]


def _attn_step(h, k, vt, q, addends, tq, m_sc, acc_sc):
    s = _nt(k, q)
    tk = s.shape[0]
    per_query = None
    if addends is not None and any(t.shape[0] == 1 for terms in addends for t in terms):
        per_query = [[t for t in terms if t.shape[0] == 1] for terms in addends]
        per_query = [sum(ts[1:], ts[0]) if ts else jnp.zeros((1, tq), F32) for ts in per_query]
        per_query = per_query[0] if len(per_query) == 1 else jnp.concatenate(per_query, axis=1)

    def rows(r0, r1):
        blk = s[r0:r1]
        if addends is None:
            return blk
        pieces = []
        for g, terms in enumerate(addends):
            piece = blk[:, g * tq:(g + 1) * tq]
            for term in terms:
                if term.shape[0] != 1:
                    piece = piece + term[r0:r1]
            pieces.append(piece)
        return pieces[0] if len(pieces) == 1 else jnp.concatenate(pieces, axis=1)

    mx = rows(0, SUBLANE)
    for r0 in range(SUBLANE, tk, SUBLANE):
        mx = jnp.maximum(mx, rows(r0, r0 + SUBLANE))
    mx = jnp.max(mx, axis=0, keepdims=True)
    m_prev = m_sc[h]
    m_new = jnp.maximum(m_prev, mx if per_query is None else mx + per_query)
    shift = m_new if per_query is None else m_new - per_query
    strip = 2 * SUBLANE
    p = jnp.concatenate([jnp.exp2(rows(r0, r0 + strip) - shift).astype(BF16) for r0 in range(0, tk, strip)], axis=0)
    acc_sc[h] = jnp.exp2(m_prev - m_new) * acc_sc[h] + _dot(vt, p)
    m_sc[h] = m_new


def _attn_finish(hq, group, tq, m_sc, acc_sc, ot_sc, o_ref, sink_ref=None, lset_sc=None, lse_ref=None):
    for h in range(hq):
        kv, g = divmod(h, group)
        cols = slice(g * tq, (g + 1) * tq)
        num, l, m = acc_sc[kv, 0:HD, cols], acc_sc[kv, HD:HD + 1, cols], m_sc[kv, :, cols]
        if sink_ref is not None:
            sk = sink_ref[:, h:h + 1] * LOG2E
            m_f = jnp.maximum(m, sk)
            w_old = jnp.exp2(m - m_f)
            l = l * w_old + jnp.exp2(sk - m_f)
            num = num * w_old
            m = m_f
        ot_sc[h * HD:(h + 1) * HD, :] = num / l
        if lset_sc is not None:
            lset_sc[h * HD:(h + 1) * HD, :] = jnp.broadcast_to(m * LN2 + jnp.log(l), num.shape)
    o_ref[0] = ot_sc[...].T
    if lset_sc is not None:
        lse_ref[0] = lset_sc[...].T


def _q_group(q_ref, kv, group):
    heads = [(q_ref[0, :, h * HD:(h + 1) * HD] * (SCALE * LOG2E)).astype(BF16)
             for h in range(kv * group, (kv + 1) * group)]
    return heads[0] if group == 1 else jnp.concatenate(heads, axis=0)


def _band_kernel(*refs, t, nb, n_back, dil, hq, group, with_sink, with_lse):
    refs = list(refs)
    q_ref, k_ref, v_ref = refs[:3]
    pos = 3
    sink_ref = None
    if with_sink:
        sink_ref = refs[pos]
        pos += 1
    o_ref = refs[pos]
    pos += 1
    lse_ref = None
    if with_lse:
        lse_ref = refs[pos]
        pos += 1
    m_sc, acc_sc, ot_sc = refs[pos:pos + 3]
    lset_sc = refs[pos + 3] if with_lse else None
    qi, j = pl.program_id(1), pl.program_id(2)
    kt = qi - nb + j

    @pl.when(j == 0)
    def _():
        _attn_init(m_sc, acc_sc)

    @pl.when(kt >= 0)
    def _():
        dist = (qi * t + lax.broadcasted_iota(jnp.int32, (t, t), 1)) - (kt * t + lax.broadcasted_iota(jnp.int32, (t, t), 0))
        band = jnp.where(dist >= 0, dist, n_back + 1)
        if dil > 1:
            band = jnp.where((dist & (dil - 1)) == 0, band, n_back + 1)
        addends = [[jnp.where(band <= n_back, 0.0, NEG)]] * group
        vts = _value_rows(v_ref)
        for kv in range(hq // group):
            k = k_ref[0, :, kv * HD:(kv + 1) * HD].astype(BF16)
            _attn_step(kv, k, vts[kv], _q_group(q_ref, kv, group), addends, t, m_sc, acc_sc)

    @pl.when(j == nb)
    def _():
        _attn_finish(hq, group, t, m_sc, acc_sc, ot_sc, o_ref, sink_ref, lset_sc, lse_ref)


def band_attention(q, kv, n_back, dil, t, sink=None, with_lse=False):
    b, l, cq = q.shape
    ck = kv.shape[2] // 2
    nb = -(-n_back // t)
    hq, hkv = cq // HD, ck // HD
    kmap = lambda blk: (lambda bi, qi, j: (bi, jnp.maximum(qi - nb + j, 0), blk))
    in_specs = [pl.BlockSpec((1, t, cq), lambda bi, qi, j: (bi, qi, 0)),
                pl.BlockSpec((1, t, ck), kmap(0)), pl.BlockSpec((1, t, ck), kmap(1))]
    args = [q, kv, kv]
    if sink is not None:
        in_specs.append(pl.BlockSpec((1, hq), lambda bi, qi, j: (0, 0)))
        args.append(sink.reshape(1, hq))
    o_spec = pl.BlockSpec((1, t, cq), lambda bi, qi, j: (bi, qi, 0))
    o_shape = jax.ShapeDtypeStruct(q.shape, F32)
    wide = hq // hkv * t
    scratch = [pltpu.VMEM((hkv, 1, wide), F32), pltpu.VMEM((hkv, HD + ONES_ROWS, wide), F32), pltpu.VMEM((cq, t), F32)]
    if with_lse:
        scratch.append(pltpu.VMEM((cq, t), F32))
    return pl.pallas_call(
        functools.partial(_band_kernel, t=t, nb=nb, n_back=n_back, dil=dil, hq=hq, group=hq // hkv,
                          with_sink=sink is not None, with_lse=with_lse),
        grid=(b, l // t, nb + 1),
        in_specs=in_specs,
        out_specs=[o_spec, o_spec] if with_lse else o_spec,
        out_shape=[o_shape, o_shape] if with_lse else o_shape,
        scratch_shapes=scratch,
        compiler_params=_params(("parallel", "parallel", "arbitrary")), name="band_attention",
    )(*args)


def _dilated_kernel(q_ref, k_ref, v_ref, o_ref, lse_ref, m_sc, acc_sc, *, dil, sub, n_back):
    qi, j = pl.program_id(2), pl.program_id(3)
    kt = qi - 1 + j

    @pl.when(j == 0)
    def _():
        _attn_init(m_sc, acc_sc)

    @pl.when(kt >= 0)
    def _():
        dist = (qi * sub + lax.broadcasted_iota(jnp.int32, (sub, sub), 1)) - (kt * sub + lax.broadcasted_iota(jnp.int32, (sub, sub), 0))
        addends = [[jnp.where(jnp.where(dist >= 0, dist, n_back + 1) <= n_back, 0.0, NEG)]]
        ones = jnp.ones((ONES_ROWS, sub), BF16)
        for r in range(dil):
            rows = pl.ds(r, sub, stride=dil)
            q = (q_ref[0, rows, :] * (SCALE * LOG2E)).astype(BF16)
            k = k_ref[0, rows, :].astype(BF16)
            vt = v_ref[0, rows, :].T.astype(BF16)
            for h in range(LANE // HD):
                cols = slice(h * HD, (h + 1) * HD)
                _attn_step(r * (LANE // HD) + h, k[:, cols], jnp.concatenate([vt[cols], ones], axis=0), q[:, cols],
                           addends, sub, m_sc, acc_sc)

    @pl.when(j == 1)
    def _():
        for r in range(dil):
            outs, lses = [], []
            for h in range(LANE // HD):
                slot = r * (LANE // HD) + h
                num, l = acc_sc[slot, 0:HD], acc_sc[slot, HD:HD + 1]
                outs.append(num / l)
                lses.append(jnp.broadcast_to(m_sc[slot] * LN2 + jnp.log(l), num.shape))
            rows = pl.ds(r, sub, stride=dil)
            o_ref[0, rows, :] = jnp.concatenate(outs, axis=0).T
            lse_ref[0, rows, :] = jnp.concatenate(lses, axis=0).T


def dilated_attention(q, kv, window, dil):
    b, l, cq = q.shape
    sub = window // dil
    n_pair = cq // LANE
    kmap = lambda off: (lambda bi, c, qi, j: (bi, jnp.maximum(qi - 1 + j, 0), off + c))
    spec = pl.BlockSpec((1, window, LANE), lambda bi, c, qi, j: (bi, qi, c))
    slots = dil * (LANE // HD)
    return pl.pallas_call(
        functools.partial(_dilated_kernel, dil=dil, sub=sub, n_back=sub),
        grid=(b, n_pair, l // window, 2),
        in_specs=[spec, pl.BlockSpec((1, window, LANE), kmap(0)), pl.BlockSpec((1, window, LANE), kmap(n_pair))],
        out_specs=[spec, spec], out_shape=[jax.ShapeDtypeStruct(q.shape, F32)] * 2,
        scratch_shapes=[pltpu.VMEM((slots, 1, sub), F32), pltpu.VMEM((slots, HD + ONES_ROWS, sub), F32)],
        compiler_params=_params(("parallel", "parallel", "parallel", "arbitrary")), name="dilated_attention",
    )(q, kv, kv)


def _flash_kernel(qt_ref, kt_ref, *refs, t, hq, group, mode):
    if mode == "fox":
        q_ref, k_ref, v_ref, cq_ref, ck_ref, o_ref, m_sc, acc_sc, ot_sc = refs
    else:
        q_ref, k_ref, v_ref, sel_ref, o_ref, m_sc, acc_sc, ot_sc = refs
    step = pl.program_id(2)
    qi, kj = qt_ref[step], kt_ref[step]

    @pl.when(kj == 0)
    def _():
        _attn_init(m_sc, acc_sc)

    def body(diagonal):
        vts = _value_rows(v_ref)
        causal = lax.broadcasted_iota(jnp.int32, (t, t), 1) >= lax.broadcasted_iota(jnp.int32, (t, t), 0)
        if mode == "sel":
            selb = sel_ref[0].astype(BF16)
            width = selb.shape[1]
            n_blk = width // (hq // group)
            key_blk = (kj * t + lax.broadcasted_iota(jnp.int32, (t, width), 0)) // SEL_BLOCK
            col = lax.broadcasted_iota(jnp.int32, (t, width), 1)
        for kv in range(hq // group):
            k = k_ref[0, :, kv * HD:(kv + 1) * HD].astype(BF16)
            if mode == "fox":
                tail = [jnp.where(causal, 0.0, NEG)] if diagonal else []
                addends = [[cq_ref[0, 0, h:h + 1, :] * LOG2E, ck_ref[0, 0, :, h:h + 1] * (-LOG2E)] + tail
                           for h in range(kv * group, (kv + 1) * group)]
            else:
                expand = jnp.where(col == kv * n_blk + key_blk, 1.0, 0.0).astype(BF16)
                flag = _nt(expand, selb)
                if diagonal:
                    flag = jnp.where(causal, flag, 0.0)
                addends = [[jnp.where(flag > 0.5, 0.0, NEG)]] * group
            _attn_step(kv, k, vts[kv], _q_group(q_ref, kv, group), addends, t, m_sc, acc_sc)

    @pl.when(kj < qi)
    def _():
        body(False)

    @pl.when(kj == qi)
    def _():
        body(True)
        _attn_finish(hq, group, t, m_sc, acc_sc, ot_sc, o_ref)


def flash_attention(q, kv, t, cq, n_col, mode, extra):
    b, l, _ = q.shape
    hq = cq // HD
    nq = l // t
    pairs = [(qi, kj) for qi in range(nq) for kj in range(qi + 1)]
    qt = jnp.asarray([p[0] for p in pairs], jnp.int32)
    kt = jnp.asarray([p[1] for p in pairs], jnp.int32)
    kmap = lambda off: (lambda bi, c, s, qt, kt: (bi, kt[s], off + c))
    in_specs = [pl.BlockSpec((1, t, cq), lambda bi, c, s, qt, kt: (bi, qt[s], c)),
                pl.BlockSpec((1, t, LANE), kmap(0)), pl.BlockSpec((1, t, LANE), kmap(n_col))]
    if mode == "fox":
        in_specs += [pl.BlockSpec((1, 1, hq, t), lambda bi, c, s, qt, kt: (bi, c, 0, qt[s])),
                     pl.BlockSpec((1, 1, t, hq), lambda bi, c, s, qt, kt: (bi, c, kt[s], 0))]
    else:
        in_specs += [pl.BlockSpec((1, t, extra[0].shape[2]), lambda bi, c, s, qt, kt: (bi, qt[s], 0))]
    return pl.pallas_call(
        functools.partial(_flash_kernel, t=t, hq=hq, group=hq // 2, mode=mode),
        grid_spec=pltpu.PrefetchScalarGridSpec(
            num_scalar_prefetch=2, grid=(b, n_col, len(pairs)), in_specs=in_specs,
            out_specs=pl.BlockSpec((1, t, cq), lambda bi, c, s, qt, kt: (bi, qt[s], c)),
            scratch_shapes=[pltpu.VMEM((2, 1, hq // 2 * t), F32), pltpu.VMEM((2, HD + ONES_ROWS, hq // 2 * t), F32),
                            pltpu.VMEM((cq, t), F32)]),
        out_shape=jax.ShapeDtypeStruct(q.shape, F32),
        compiler_params=_params(("parallel", "parallel", "arbitrary")), name="flash_" + mode,
    )(qt, kt, q, kv, kv, *extra)


def _compress_kernel(*refs, n_src, chunks, paged):
    refs = refs[1:] if paged else refs
    n_in = n_src if paged else 2
    src = refs[:n_in]
    w1_ref, b1_ref, w2_ref, out_ref, x_sc = refs[n_in:n_in + 5]
    total = n_src * chunks
    if paged:
        tok_sc = refs[n_in + 5]
    fold = max(1, 2 * SUBLANE // chunks)

    def token_rows(i0, j, s):
        pick = pl.ds(s, chunks, stride=CMP_STRIDE)
        if paged:
            return jnp.concatenate([tok_sc[f, j, pick, :] for f in range(fold)], axis=0)
        return src[j][0, pick, :]

    for i0 in range(0, n_src, fold):
        r0, r1 = i0 * chunks, (i0 + fold) * chunks
        for j in range(2):
            if paged:
                for f in range(fold):
                    tok_sc[f, j] = src[i0 + f][0, j * LANE:(j + 1) * LANE, :].T
            for s in range(0, CMP_STRIDE, 2):
                a, b = token_rows(i0, j, s), token_rows(i0, j, s + 1)
                low = lax.broadcasted_iota(jnp.int32, a.shape, 1) < HD
                cols = slice(s * HD, (s + 2) * HD)
                x_sc[j, r0:r1, cols] = jnp.where(low, a, pltpu.roll(b, HD, axis=1)).astype(BF16)
                x_sc[j, total + r0:total + r1, cols] = jnp.where(low, pltpu.roll(a, HD, axis=1), b).astype(BF16)
    for j in range(2):
        parts = _dot(x_sc[j], w1_ref[j])
        half = parts.shape[1] // 2
        pre = b1_ref[j] + parts[:, :half] + pltpu.roll(parts[:, half:], 2 * total - 1, axis=0)
        act = pre * jax.nn.sigmoid(pre)
        res = _dot(act.astype(BF16), w2_ref[j])
        for kv in range(2):
            out_ref[0, :, j * LANE + kv * HD:j * LANE + (kv + 1) * HD] = res[kv * total:(kv + 1) * total]


def compress(x, page_table, w1p, b1p, w2p):
    const = lambda nd: (lambda *_: (0,) * nd)
    w_specs = [pl.BlockSpec(w1p.shape, const(3)), pl.BlockSpec(b1p.shape, const(3)), pl.BlockSpec(w2p.shape, const(3))]
    if page_table is None:
        b, l, width = x.shape
        n_src, chunks = 1, l // CMP_STRIDE
        scratch = [pltpu.VMEM((2, 2 * chunks, CMP_STRIDE * HD), BF16)]
        halves = [pl.BlockSpec((1, l, LANE), lambda i: (i, 0, 0)), pl.BlockSpec((1, l, LANE), lambda i: (i, 0, 1))]
        grid_spec = pl.GridSpec(
            grid=(b,), in_specs=halves + w_specs,
            out_specs=pl.BlockSpec((1, chunks, width), lambda i: (i, 0, 0)), scratch_shapes=scratch)
        args = (x, x, w1p, b1p, w2p)
    else:
        b, n_src = page_table.shape
        _, width, rows = x.shape
        chunks = rows // CMP_STRIDE
        fold = max(1, 2 * SUBLANE // chunks)
        scratch = [pltpu.VMEM((2, 2 * n_src * chunks, CMP_STRIDE * HD), BF16),
                   pltpu.VMEM((fold, 2, rows, LANE), F32)]
        page_spec = lambda p: pl.BlockSpec((1, width, rows), lambda i, pt: (pt[i, p], 0, 0))
        grid_spec = pltpu.PrefetchScalarGridSpec(
            num_scalar_prefetch=1, grid=(b,), in_specs=[page_spec(p) for p in range(n_src)] + w_specs,
            out_specs=pl.BlockSpec((1, n_src * chunks, width), lambda i, pt: (i, 0, 0)), scratch_shapes=scratch)
        args = (page_table,) + (x,) * n_src + (w1p, b1p, w2p)
    return pl.pallas_call(
        functools.partial(_compress_kernel, n_src=n_src, chunks=chunks, paged=page_table is not None),
        grid_spec=grid_spec,
        out_shape=jax.ShapeDtypeStruct((b, n_src * chunks, width), F32),
        compiler_params=_params(("arbitrary",)), name="compress",
    )(*args)


def _compress_weights(phi_w1, phi_b1, phi_w2):
    hidden = phi_w1.shape[-1]
    parts = CMP_BLOCK // CMP_STRIDE
    w1p = phi_w1.reshape(2, parts, CMP_STRIDE * HD, hidden).transpose(0, 2, 1, 3).reshape(2, CMP_STRIDE * HD, parts * hidden)
    return w1p.astype(BF16), phi_b1[:, None, :], phi_w2.astype(BF16)


def _top_blocks(imp, n_take):
    lane = lax.broadcasted_iota(jnp.int32, imp.shape, 1).astype(F32)
    sel = jnp.zeros(imp.shape, F32)
    cur = imp
    for _ in range(n_take):
        mx = jnp.max(cur, axis=1, keepdims=True)
        first = jnp.min(jnp.where(cur == mx, lane, 1e9), axis=1, keepdims=True)
        pick = lane == first
        sel = jnp.where(pick, 1.0, sel)
        cur = jnp.where(pick, DEAD, cur)
    return sel


def _cmp_prompt_kernel(q_ref, kvc_ref, cover_ref, o_ref, sel_ref, *, t, n_cmp, n_kv, group, n_blk):
    qi = pl.program_id(1)
    rows = kvc_ref.shape[1]
    qpos = qi * t + lax.broadcasted_iota(jnp.int32, (t, 1), 0)
    ci = lax.broadcasted_iota(jnp.int32, (1, rows), 1)
    valid = (ci * CMP_STRIDE + CMP_BLOCK - 1 <= qpos) & (ci < n_cmp)
    blk = lax.broadcasted_iota(jnp.int32, (1, n_blk), 1)
    cur = qpos // SEL_BLOCK
    forced = (blk == 0) | (blk == cur) | (blk == cur - 1)
    cover = cover_ref[...]
    for kv in range(n_kv):
        kc = kvc_ref[0, :, kv * HD:(kv + 1) * HD].astype(BF16)
        vc = kvc_ref[0, :, (n_kv + kv) * HD:(n_kv + kv + 1) * HD].astype(BF16)
        imp = jnp.zeros((t, n_blk), F32)
        for g in range(group):
            h = kv * group + g
            s = jnp.where(valid, _nt(q_ref[0, :, h * HD:(h + 1) * HD].astype(BF16), kc) * SCALE, NEG)
            m = jnp.max(s, axis=1, keepdims=True)
            e = jnp.where(valid, jnp.exp(s - m), 0.0)
            den = jnp.sum(e, axis=1, keepdims=True)
            p = (e / jnp.where(den > 0, den, 1.0)).astype(BF16)
            o_ref[0, :, h * HD:(h + 1) * HD] = _dot(p, vc)
            imp = imp + _dot(p, cover)
        imp = jnp.where(blk * SEL_BLOCK <= qpos, jnp.where(forced, FORCE_SCORE, imp), NEG)
        sel_ref[0, :, kv * n_blk:(kv + 1) * n_blk] = _top_blocks(imp, min(N_SEL, n_blk))


def _cover_matrix(rows, n_cmp, n_blk, width):
    c_start = np.arange(rows)[:, None] * CMP_STRIDE
    s_start = np.arange(width)[None, :] * SEL_BLOCK
    cov = (c_start < s_start + SEL_BLOCK) & (c_start + CMP_BLOCK > s_start)
    cov &= (np.arange(rows)[:, None] < n_cmp) & (np.arange(width)[None, :] < n_blk)
    return jnp.asarray(cov, dtype=BF16)


def cmp_attend_prompt(q, kvc, t, n_kv):
    b, l, cq = q.shape
    rows = kvc.shape[1]
    n_blk = l // SEL_BLOCK
    cover = _cover_matrix(rows, rows - 1, n_blk, n_blk)
    return pl.pallas_call(
        functools.partial(_cmp_prompt_kernel, t=t, n_cmp=rows - 1, n_kv=n_kv, group=cq // HD // n_kv, n_blk=n_blk),
        grid=(b, l // t),
        in_specs=[pl.BlockSpec((1, t, cq), lambda bi, qi: (bi, qi, 0)),
                  pl.BlockSpec((1, rows, kvc.shape[2]), lambda bi, qi: (bi, 0, 0)),
                  pl.BlockSpec(cover.shape, lambda bi, qi: (0, 0))],
        out_specs=[pl.BlockSpec((1, t, cq), lambda bi, qi: (bi, qi, 0)),
                   pl.BlockSpec((1, t, n_kv * n_blk), lambda bi, qi: (bi, qi, 0))],
        out_shape=[jax.ShapeDtypeStruct(q.shape, F32), jax.ShapeDtypeStruct((b, l, n_kv * n_blk), F32)],
        compiler_params=_params(("parallel", "parallel")), name="cmp_attend_prompt",
    )(q, kvc, cover)


def _diag_heads(x, rows_kv, n_kv):
    out = jnp.zeros((x.shape[0], HD), F32)
    for kv in range(n_kv):
        out = out + jnp.where(rows_kv == kv, x[:, kv * HD:(kv + 1) * HD], 0.0)
    return out


def _cmp_sample_kernel(q_ref, kvc_ref, cover_ref, o_ref, sel_ref, *, n_tok, heads, n_kv, n_cmp, past_len, n_blk):
    rows = kvc_ref.shape[1]
    r_all = n_tok * heads
    c = n_kv * HD
    width = cover_ref.shape[1]
    group = heads // n_kv
    ridx = lax.broadcasted_iota(jnp.int32, (r_all, 1), 0)
    qpos = past_len + ridx // heads
    ci = lax.broadcasted_iota(jnp.int32, (1, rows), 1)
    valid = (ci * CMP_STRIDE + CMP_BLOCK - 1 <= qpos) & (ci < n_cmp)
    g_rows = r_all // group
    n_seq = q_ref.shape[0]
    imps = []
    for i in range(n_seq):
        kc = kvc_ref[i, :, 0:c].astype(BF16)
        vc = kvc_ref[i, :, c:2 * c].astype(BF16)
        s = jnp.where(valid, _nt(q_ref[i].astype(BF16), kc) * SCALE, NEG)
        m = jnp.max(s, axis=1, keepdims=True)
        e = jnp.where(valid, jnp.exp(s - m), 0.0)
        den = jnp.sum(e, axis=1, keepdims=True)
        p = (e / jnp.where(den > 0, den, 1.0)).astype(BF16)
        o_ref[i] = _diag_heads(_dot(p, vc), (ridx % heads) // group, n_kv)
        imps.append(jnp.sum(_dot(p, cover_ref[...]).reshape(g_rows, group, width), axis=1))
    imp = jnp.concatenate(imps, axis=0)
    gpos = past_len + (lax.broadcasted_iota(jnp.int32, (n_seq * g_rows, 1), 0) % g_rows) // n_kv
    blk = lax.broadcasted_iota(jnp.int32, (1, width), 1)
    cur = gpos // SEL_BLOCK
    forced = (blk == 0) | (blk == cur) | (blk == cur - 1)
    imp = jnp.where(blk * SEL_BLOCK <= gpos, jnp.where(forced, FORCE_SCORE, imp), NEG)
    imp = jnp.where(blk < n_blk, imp, DEAD)
    sel = _top_blocks(imp, min(N_SEL, n_blk))
    for i in range(n_seq):
        sel_ref[i] = sel[i * g_rows:(i + 1) * g_rows]


def cmp_attend_sample(qbd, kvc, n_tok, heads, n_kv, past_len):
    b, r_all, c = qbd.shape
    rows = kvc.shape[1]
    n_blk = past_len // SEL_BLOCK + -(-n_tok // SEL_BLOCK)
    width = -(-n_blk // LANE) * LANE
    cover = _cover_matrix(rows, rows - 1, n_blk, width)
    g_rows = r_all // (heads // n_kv)
    n_seq = next(n for n in (8, 4, 2, 1) if b % n == 0)
    return pl.pallas_call(
        functools.partial(_cmp_sample_kernel, n_tok=n_tok, heads=heads, n_kv=n_kv, n_cmp=rows - 1,
                          past_len=past_len, n_blk=n_blk),
        grid=(b // n_seq,),
        in_specs=[pl.BlockSpec((n_seq, r_all, c), lambda i: (i, 0, 0)),
                  pl.BlockSpec((n_seq, rows, 2 * c), lambda i: (i, 0, 0)),
                  pl.BlockSpec(cover.shape, lambda i: (0, 0))],
        out_specs=[pl.BlockSpec((n_seq, r_all, HD), lambda i: (i, 0, 0)),
                   pl.BlockSpec((n_seq, g_rows, width), lambda i: (i, 0, 0))],
        out_shape=[jax.ShapeDtypeStruct((b, r_all, HD), F32), jax.ShapeDtypeStruct((b, g_rows, width), F32)],
        compiler_params=_params(("parallel",)), name="cmp_attend_sample",
    )(qbd, kvc, cover)


def _dec_init(m_sc, l_sc, acc_sc):
    m_sc[...] = jnp.full(m_sc.shape, NEG, F32)
    l_sc[...] = jnp.zeros(l_sc.shape, F32)
    acc_sc[...] = jnp.zeros(acc_sc.shape, F32)


def _dec_update(qb, kt, vt, bias, mask, m_sc, l_sc, acc_sc):
    s = _dot(qb, kt) * SCALE
    if bias is not None:
        s = s + bias
    if mask is not None:
        s = jnp.where(mask, s, NEG)
    m_prev = m_sc[...]
    m_new = jnp.maximum(m_prev, jnp.max(s, axis=1, keepdims=True))
    alpha = jnp.exp(m_prev - m_new)
    p = jnp.exp(s - m_new)
    if mask is not None:
        p = jnp.where(mask, p, 0.0)
    l_sc[...] = alpha * l_sc[...] + jnp.sum(p, axis=1, keepdims=True)
    acc_sc[...] = alpha * acc_sc[...] + _nt(p.astype(BF16), vt)
    m_sc[...] = m_new


def _dec_new_tokens(q, knew, vnew, biases, masks, m_sc, l_sc, acc_sc):
    n_new = knew.shape[0]
    scores = []
    for u in range(n_new):
        s = jnp.sum(q * knew[u:u + 1, :], axis=1, keepdims=True) * SCALE
        if biases is not None:
            s = s + biases[u]
        scores.append(jnp.where(masks[u], s, NEG))
    m_prev = m_sc[...]
    m_new = m_prev
    for s in scores:
        m_new = jnp.maximum(m_new, s)
    alpha = jnp.exp(m_prev - m_new)
    l = alpha * l_sc[...]
    acc = alpha * acc_sc[...]
    for u in range(n_new):
        p = jnp.where(masks[u], jnp.exp(scores[u] - m_new), 0.0)
        l = l + p
        acc = acc + p * vnew[u:u + 1, :]
    l_sc[...] = l
    acc_sc[...] = acc
    m_sc[...] = m_new


def _token_cumsum(x, n_tok, heads):
    parts, run = [], None
    for u in range(n_tok):
        seg = x[u * heads:(u + 1) * heads]
        run = seg if run is None else run + seg
        parts.append(run)
    return jnp.concatenate(parts, axis=0), parts


def _dec_window_kernel(*refs, n_tok, heads, n_kv, lb, n_back, dil, chunk, with_sink, with_lse):
    refs = list(refs)
    q_ref, cache_ref, new_ref = refs[:3]
    pos = 3
    sink_ref = None
    if with_sink:
        sink_ref = refs[pos]
        pos += 1
    o_ref = refs[pos]
    pos += 1
    lse_ref = None
    if with_lse:
        lse_ref = refs[pos]
        pos += 1
    m_sc, l_sc, acc_sc = refs[pos:pos + 3]
    r_all = n_tok * heads
    c = n_kv * HD
    group = heads // n_kv
    ridx = lax.broadcasted_iota(jnp.int32, (r_all, 1), 0)
    tok = ridx // heads
    _dec_init(m_sc, l_sc, acc_sc)
    q = q_ref[0]
    qb = q.astype(BF16)

    def ok(dist):
        good = (dist >= 0) & (dist <= n_back)
        if dil > 1:
            good = good & (dist % dil == 0)
        return good

    for c0 in range(0, lb, chunk):
        kt = cache_ref[0, 0:c, c0:c0 + chunk].astype(BF16)
        vt = cache_ref[0, c:2 * c, c0:c0 + chunk].astype(BF16)
        dist = (lb + tok) - (c0 + lax.broadcasted_iota(jnp.int32, (1, chunk), 1))
        _dec_update(qb, kt, vt, None, ok(dist), m_sc, l_sc, acc_sc)
    new = new_ref[0]
    _dec_new_tokens(q, new[:, 0:c], new[:, c:2 * c], None, [ok(tok - u) for u in range(n_tok)], m_sc, l_sc, acc_sc)
    m, l, acc = m_sc[...], l_sc[...], acc_sc[...]
    if with_sink:
        sk = sink_ref[...]
        m_f = jnp.maximum(m, sk)
        scale_old = jnp.exp(m - m_f)
        l = l * scale_old + jnp.exp(sk - m_f)
        acc = acc * scale_old
        m = m_f
    o_ref[0] = _diag_heads(acc / l, (ridx % heads) // group, n_kv)
    if with_lse:
        lse_ref[0] = jnp.broadcast_to(m + jnp.log(l), (r_all, HD))


def dec_window(qbd, cache_t, new, n_tok, heads, n_kv, n_back, dil, sink_rows=None, with_lse=False):
    b, r_all, c = qbd.shape
    lb = cache_t.shape[2]
    in_specs = [pl.BlockSpec((1, r_all, c), lambda i: (i, 0, 0)),
                pl.BlockSpec((1, 2 * c, lb), lambda i: (i, 0, 0)),
                pl.BlockSpec((1, n_tok, 2 * c), lambda i: (i, 0, 0))]
    args = [qbd, cache_t, new]
    if sink_rows is not None:
        in_specs.append(pl.BlockSpec((r_all, 1), lambda i: (0, 0)))
        args.append(sink_rows)
    o_spec = pl.BlockSpec((1, r_all, HD), lambda i: (i, 0, 0))
    o_shape = jax.ShapeDtypeStruct((b, r_all, HD), F32)
    return pl.pallas_call(
        functools.partial(_dec_window_kernel, n_tok=n_tok, heads=heads, n_kv=n_kv, lb=lb, n_back=n_back, dil=dil,
                          chunk=min(lb, 512), with_sink=sink_rows is not None, with_lse=with_lse),
        grid=(b,), in_specs=in_specs,
        out_specs=[o_spec, o_spec] if with_lse else o_spec,
        out_shape=[o_shape, o_shape] if with_lse else o_shape,
        scratch_shapes=[pltpu.VMEM((r_all, 1), F32), pltpu.VMEM((r_all, 1), F32), pltpu.VMEM((r_all, c), F32)],
        compiler_params=_params(("parallel",)), name="dec_window",
    )(*args)


def _dec_paged_kernel(*refs, n_pg, n_tok, heads, n_kv, mode):
    pages = refs[1:1 + n_pg]
    pos = 1 + n_pg
    lf_pages = None
    if mode == "fox":
        lf_pages = refs[pos:pos + n_pg]
        pos += n_pg
    q_ref, new_ref, x_ref = refs[pos:pos + 3]
    pos += 3
    if mode == "fox":
        tri_ref = refs[pos]
        pos += 1
    o_ref = refs[pos]
    pos += 1
    k_sc, v_sc, m_sc, l_sc, acc_sc = refs[pos:pos + 5]
    pos += 5
    if mode == "fox":
        c_sc, carry_sc = refs[pos:pos + 2]
    gi = pl.program_id(1)
    r_all = n_tok * heads
    c = n_kv * HD
    group = heads // n_kv
    tk = n_pg * PAGE
    ridx = lax.broadcasted_iota(jnp.int32, (r_all, 1), 0)
    tok = ridx // heads
    head = ridx % heads

    @pl.when(gi == 0)
    def _():
        _dec_init(m_sc, l_sc, acc_sc)
        if mode == "fox":
            carry_sc[...] = jnp.zeros(carry_sc.shape, F32)

    for i in range(n_pg):
        k_sc[:, i * PAGE:(i + 1) * PAGE] = pages[i][0, 0:c, :].astype(BF16)
        v_sc[:, i * PAGE:(i + 1) * PAGE] = pages[i][0, c:2 * c, :].astype(BF16)
    q = q_ref[0]
    qb = q.astype(BF16)

    if mode == "fox":
        tri = tri_ref[...]
        hi, mid, lo = _split3(jnp.concatenate([lf_pages[i][0] for i in range(n_pg)], axis=0))
        within = _dot(hi, tri) + _dot(mid, tri) + _dot(lo, tri)
        carry = carry_sc[...]
        for i in range(n_pg):
            page = within[i * heads:(i + 1) * heads]
            c_sc[:, i * PAGE:(i + 1) * PAGE] = page + carry
            carry = carry + page[:, PAGE - 1:PAGE]
        carry_sc[...] = carry
        bias = _token_cumsum(x_ref[0], n_tok, heads)[0] - jnp.concatenate([c_sc[...]] * n_tok, axis=0)
        _dec_update(qb, k_sc[...], v_sc[...], bias, None, m_sc, l_sc, acc_sc)
    else:
        flags = x_ref[0].astype(BF16)
        width = flags.shape[1]
        key_blk = (gi * tk + lax.broadcasted_iota(jnp.int32, (width, tk), 1)) // SEL_BLOCK
        expand = jnp.where(lax.broadcasted_iota(jnp.int32, (width, tk), 0) == key_blk, 1.0, 0.0).astype(BF16)
        _dec_update(qb, k_sc[...], v_sc[...], None, _dot(flags, expand) > 0.5, m_sc, l_sc, acc_sc)

    @pl.when(gi == pl.num_programs(1) - 1)
    def _():
        new = new_ref[0]
        causal = [tok >= u for u in range(n_tok)]
        if mode == "fox":
            fq, per_tok = _token_cumsum(x_ref[0], n_tok, heads)
            total = jnp.concatenate([carry_sc[...]] * n_tok, axis=0)
            biases = [fq - jnp.concatenate([per_tok[u]] * n_tok, axis=0) - total for u in range(n_tok)]
            _dec_new_tokens(q, new[:, 0:c], new[:, c:2 * c], biases, causal, m_sc, l_sc, acc_sc)
        else:
            n_past_blk = (pl.num_programs(1) * tk) // SEL_BLOCK
            flags = x_ref[0]
            lane = lax.broadcasted_iota(jnp.int32, flags.shape, 1)
            new_on = jnp.sum(jnp.where(lane == n_past_blk, flags, 0.0), axis=1, keepdims=True) > 0.5
            _dec_new_tokens(q, new[:, 0:c], new[:, c:2 * c], None, [m & new_on for m in causal], m_sc, l_sc, acc_sc)
        o_ref[0] = _diag_heads(acc_sc[...] / l_sc[...], head // group, n_kv)


def dec_paged(qbd, pool_t, page_table, new, x, n_tok, heads, n_kv, mode, n_pg, lf_pool_t=None):
    b, r_all, c = qbd.shape
    n_pages = page_table.shape[1]
    page_spec = lambda p, rows: pl.BlockSpec((1, rows, PAGE), lambda i, g, pt: (pt[i, g * n_pg + p], 0, 0))
    in_specs = [page_spec(p, 2 * c) for p in range(n_pg)]
    args = [pool_t] * n_pg
    if mode == "fox":
        in_specs += [page_spec(p, heads) for p in range(n_pg)]
        args += [lf_pool_t] * n_pg
    in_specs += [pl.BlockSpec((1, r_all, c), lambda i, g, pt: (i, 0, 0)),
                 pl.BlockSpec((1, n_tok, 2 * c), lambda i, g, pt: (i, 0, 0)),
                 pl.BlockSpec((1, r_all, x.shape[2]), lambda i, g, pt: (i, 0, 0))]
    args += [qbd, new, x]
    scratch = [pltpu.VMEM((c, n_pg * PAGE), BF16), pltpu.VMEM((c, n_pg * PAGE), BF16),
               pltpu.VMEM((r_all, 1), F32), pltpu.VMEM((r_all, 1), F32), pltpu.VMEM((r_all, c), F32)]
    if mode == "fox":
        tri = jnp.asarray(np.triu(np.ones((PAGE, PAGE), np.float32)), dtype=BF16)
        in_specs.append(pl.BlockSpec((PAGE, PAGE), lambda i, g, pt: (0, 0)))
        args.append(tri)
        scratch += [pltpu.VMEM((heads, n_pg * PAGE), F32), pltpu.VMEM((heads, 1), F32)]
    return pl.pallas_call(
        functools.partial(_dec_paged_kernel, n_pg=n_pg, n_tok=n_tok, heads=heads, n_kv=n_kv, mode=mode),
        grid_spec=pltpu.PrefetchScalarGridSpec(
            num_scalar_prefetch=1, grid=(b, n_pages // n_pg), in_specs=in_specs,
            out_specs=pl.BlockSpec((1, r_all, HD), lambda i, g, pt: (i, 0, 0)), scratch_shapes=scratch),
        out_shape=jax.ShapeDtypeStruct((b, r_all, HD), F32),
        compiler_params=_params(("parallel", "arbitrary")), name="dec_paged_" + mode,
    )(page_table, *args)


def _block_diag_q(q, n_tok, heads, n_kv):
    b = q.shape[0] // n_tok
    own = jnp.asarray(np.arange(heads)[:, None] // (heads // n_kv) == np.arange(n_kv)[None, :], dtype=F32)
    qb = q.reshape(b, n_tok, heads, 1, HD) * own[None, None, :, :, None]
    return qb.reshape(b, n_tok * heads, n_kv * HD)


def _rows_last(cache):
    n, rows = cache.shape[:2]
    return cache.transpose(0, 2, 3, 4, 1).reshape(n, -1, rows)


def _mixer_a(hp, hs, g, w_in, w_out, phi, caches, page_table, dims, ropes):
    bsz, seq, dec_b, dec_t, past_len = dims
    (cos_p, sin_p), (cos_s, sin_s) = ropes
    cache_cmp, cache_sel, cache_win = caches
    heads, n_kv = 16, 2
    segs = [(0, 1024, 0, 1024, True), (1024, 1280, 0, 0, False), (1280, 1536, 0, 128, False),
            (1536, 1792, 0, 128, False), (1792, 1840, 0, 0, False)]
    w1p, b1p, w2p = phi
    q_cmp, q_rot, cmp_raw, sel_kv, win_kv, gates = norm_proj(hp, g, w_in, cos_p, sin_p, segs, 512)
    n = bsz * seq
    kvc = compress(cmp_raw.reshape(bsz, seq, -1), None, w1p, b1p, w2p)
    o_cmp, flags = cmp_attend_prompt(q_cmp.reshape(bsz, seq, -1), kvc, 512, n_kv)
    q3 = q_rot.reshape(bsz, seq, -1)
    o_sel = flash_attention(q3, sel_kv.reshape(bsz, seq, -1), 256, 1024, 1, "sel", (flags,))
    o_win = band_attention(q3, win_kv.reshape(bsz, seq, -1), A_WINDOW - 1, 1, 256)
    hp = merge_a(o_cmp.reshape(n, -1), o_sel.reshape(n, -1), o_win.reshape(n, -1), gates, hp, w_out)
    kv5 = lambda z, b, t: z.reshape(b, t, 2, n_kv, HD)
    w = min(A_WINDOW, seq)
    outs_p = (kv5(cmp_raw, bsz, seq), kv5(sel_kv, bsz, seq), kv5(win_kv, bsz, seq)[:, -w:])
    q_cmp, q_rot, cmp_raw, sel_kv, win_kv, gates = norm_proj(hs, g, w_in, cos_s, sin_s, segs, 512)
    ns = dec_b * dec_t
    kvc = compress(_rows_last(cache_cmp), page_table, w1p, b1p, w2p)
    o_cmp, flags = cmp_attend_sample(_block_diag_q(q_cmp, dec_t, heads, n_kv), kvc, dec_t, heads, n_kv, past_len)
    qbd = _block_diag_q(q_rot, dec_t, heads, n_kv)
    flag_rows = jnp.repeat(flags.reshape(dec_b, dec_t, n_kv, 1, -1), heads // n_kv, axis=3).reshape(dec_b, dec_t * heads, -1)
    o_sel = dec_paged(qbd, _rows_last(cache_sel), page_table, sel_kv.reshape(dec_b, dec_t, -1), flag_rows,
                      dec_t, heads, n_kv, "sel", 32)
    o_win = dec_window(qbd, _rows_last(cache_win), win_kv.reshape(dec_b, dec_t, -1), dec_t, heads, n_kv, A_WINDOW - 1, 1)
    hs = merge_a(o_cmp.reshape(ns, -1), o_sel.reshape(ns, -1), o_win.reshape(ns, -1), gates, hs, w_out)
    outs_s = (kv5(cmp_raw, dec_b, dec_t), kv5(sel_kv, dec_b, dec_t), kv5(win_kv, dec_b, dec_t))
    return hp, hs, outs_p, outs_s


def _mixer_b(hp, hs, g, w_in, sinks, w_out, cache_win, dims, ropes):
    bsz, seq, dec_b, dec_t, _ = dims
    (cos_p, sin_p), (cos_s, sin_s) = ropes
    heads, n_kv = 16, 2
    segs = [(0, 1024, 0, 1024, False), (1024, 1280, 0, 128, False)]
    q, kv = norm_proj(hp, g, w_in, cos_p, sin_p, segs, 512)
    o = band_attention(q.reshape(bsz, seq, -1), kv.reshape(bsz, seq, -1), B_WINDOW - 1, 1, 256, sink=sinks)
    hp = out_proj(o.reshape(bsz * seq, -1), hp, w_out)
    w = min(B_WINDOW, seq)
    out_p = kv.reshape(bsz, seq, 2, n_kv, HD)[:, -w:]
    q, kv = norm_proj(hs, g, w_in, cos_s, sin_s, segs, 512)
    sink_rows = jnp.tile(sinks.astype(F32), dec_t).reshape(dec_t * heads, 1)
    o = dec_window(_block_diag_q(q, dec_t, heads, n_kv), _rows_last(cache_win), kv.reshape(dec_b, dec_t, -1),
                   dec_t, heads, n_kv, B_WINDOW - 1, 1, sink_rows=sink_rows)
    hs = out_proj(o.reshape(dec_b * dec_t, -1), hs, w_out)
    return hp, hs, out_p, kv.reshape(dec_b, dec_t, 2, n_kv, HD)


def _mixer_c(hp, hs, g, w_in, b_f, w_out, cache_kv, cache_logf, page_table, dims, ropes):
    bsz, seq, dec_b, dec_t, _ = dims
    (cos_p, sin_p), (cos_s, sin_s) = ropes
    heads, n_kv = 16, 4
    segs = [(0, 1024, 0, 0, False), (1024, 1536, 0, 0, False), (1536, 1552, 0, 0, False)]
    q, kv, f = norm_proj(hp, g, w_in, cos_p, sin_p, segs, 512)
    lf_t, cum_t = fox_gate(f.reshape(bsz, seq, heads).transpose(0, 2, 1), b_f)
    logf_p = lf_t.transpose(0, 2, 1)
    half = heads // 2
    cum_q = cum_t.reshape(bsz, 2, half, seq)
    cum_k = cum_q.transpose(0, 1, 3, 2)
    o = flash_attention(q.reshape(bsz, seq, -1), kv.reshape(bsz, seq, -1), 512, 512, 2, "fox", (cum_q, cum_k))
    hp = out_proj(o.reshape(bsz * seq, -1), hp, w_out)
    out_p = kv.reshape(bsz, seq, 2, n_kv, HD)
    q, kv, f = norm_proj(hs, g, w_in, cos_s, sin_s, segs, 512)
    ns = dec_b * dec_t
    lf_t, _ = fox_gate(f.reshape(1, ns, heads).transpose(0, 2, 1), b_f)
    logf_s = lf_t.transpose(0, 2, 1).reshape(dec_b, dec_t, heads)
    o = dec_paged(_block_diag_q(q, dec_t, heads, n_kv), _rows_last(cache_kv), page_table, kv.reshape(dec_b, dec_t, -1),
                  logf_s.reshape(dec_b, dec_t * heads, 1), dec_t, heads, n_kv, "fox", 32,
                  lf_pool_t=cache_logf.transpose(0, 2, 1))
    hs = out_proj(o.reshape(ns, -1), hs, w_out)
    return hp, hs, (out_p, logf_p), (kv.reshape(dec_b, dec_t, 2, n_kv, HD), logf_s)


def _mixer_d(hp, hs, g, w_in, w_out, caches, dims, ropes):
    bsz, seq, dec_b, dec_t, _ = dims
    (cos_p, sin_p), (cos_s, sin_s) = ropes
    heads = 8
    c = heads * HD
    segs = []
    for grp in range(len(D_GROUPS)):
        base = 3 * grp * c
        segs += [(base, base + c, 0, c, False), (base + c, base + 3 * c, 0, c, False)]
    res = norm_proj(hp, g, w_in, cos_p, sin_p, segs, 256)
    outs, lses, states_p = [], [], []
    for grp, (window, dil) in enumerate(D_GROUPS):
        q, kv = res[2 * grp], res[2 * grp + 1]
        if dil == 1:
            o, lse = band_attention(q.reshape(bsz, seq, c), kv.reshape(bsz, seq, 2 * c), window, 1, 256, with_lse=True)
        else:
            o, lse = dilated_attention(q.reshape(bsz, seq, c), kv.reshape(bsz, seq, 2 * c), window, dil)
        outs.append(o.reshape(bsz * seq, c))
        lses.append(lse.reshape(bsz * seq, c))
        states_p.append(kv.reshape(bsz, seq, 2, heads, HD)[:, -min(window, seq):])
    hp = merge_d(outs, lses, hp, w_out)
    res = norm_proj(hs, g, w_in, cos_s, sin_s, segs, 256)
    outs, lses, states_s = [], [], []
    for grp, (window, dil) in enumerate(D_GROUPS):
        q, kv = res[2 * grp], res[2 * grp + 1]
        o, lse = dec_window(_block_diag_q(q, dec_t, heads, heads), _rows_last(caches[grp]),
                            kv.reshape(dec_b, dec_t, -1), dec_t, heads, heads, window, dil, with_lse=True)
        outs.append(o.reshape(dec_b * dec_t, c))
        lses.append(lse.reshape(dec_b * dec_t, c))
        states_s.append(kv.reshape(dec_b, dec_t, 2, heads, HD))
    hs = merge_d(outs, lses, hs, w_out)
    return hp, hs, states_p, states_s


def kernel(x_prompt, x_sample, cache_a_cmp, cache_a_sel, cache_a_win, cache_b_win, cache_c_kv, cache_c_logf, cache_d1, cache_d2, cache_d3, page_table, norm_mix, norm_ffn, norm_final, a_w_in, a_phi_w1, a_phi_b1, a_phi_w2, a_w_out, b_w_in, b_sinks, b_w_out, c_w_in, c_b_f, c_w_out, d_w_in, d_w_out, ffn_w_gate, ffn_w_up, ffn_w_down):
    bsz, seq, d_model = x_prompt.shape
    dec_b, dec_t, _ = x_sample.shape
    past_len = page_table.shape[1] * PAGE
    dims = (bsz, seq, dec_b, dec_t, past_len)
    ropes = (_rope_tables(jnp.arange(seq)), _rope_tables(past_len + jnp.arange(dec_b * dec_t) % dec_t))
    hp = x_prompt.reshape(bsz * seq, d_model)
    hs = x_sample.reshape(dec_b * dec_t, d_model)
    bf = lambda z: z.astype(BF16)
    phi = _compress_weights(a_phi_w1, a_phi_b1, a_phi_w2)
    depth = norm_mix.shape[0]
    for layer in range(depth):
        kind = layer % 4
        g = norm_mix[layer]
        if kind == 0:
            hp, hs, a_p, a_s = _mixer_a(hp, hs, g, bf(a_w_in), bf(a_w_out), phi,
                                        (cache_a_cmp, cache_a_sel, cache_a_win), page_table, dims, ropes)
        elif kind == 1:
            hp, hs, b_p, b_s = _mixer_b(hp, hs, g, bf(b_w_in), b_sinks, bf(b_w_out), cache_b_win, dims, ropes)
        elif kind == 2:
            hp, hs, c_p, c_s = _mixer_c(hp, hs, g, bf(c_w_in), c_b_f, bf(c_w_out), cache_c_kv, cache_c_logf,
                                        page_table, dims, ropes)
        else:
            hp, hs, d_p, d_s = _mixer_d(hp, hs, g, bf(d_w_in), bf(d_w_out), (cache_d1, cache_d2, cache_d3), dims, ropes)
        wg, wu, wd = bf(ffn_w_gate[layer]), bf(ffn_w_up[layer]), bf(ffn_w_down[layer])
        hp = ffn(hp, norm_ffn[layer], wg, wu, wd)
        hs = ffn(hs, norm_ffn[layer], wg, wu, wd)
    y_p = final_norm(hp, norm_final).reshape(bsz, seq, d_model)
    y_s = final_norm(hs, norm_final).reshape(dec_b, dec_t, d_model)
    return (y_p, y_s, a_p[0], a_s[0], a_p[1], a_s[1], a_p[2], a_s[2], b_p, b_s,
            c_p[0], c_s[0], c_p[1], c_s[1], d_p[0], d_s[0], d_p[1], d_s[1], d_p[2], d_s[2])
```

```python
import functools

import numpy as np
import jax
import jax.numpy as jnp
from jax import lax
from jax.experimental import pallas as pl
from jax.experimental.pallas import tpu as pltpu

F32 = jnp.float32
BF16 = jnp.bfloat16

HD = 64
SCALE = HD ** -0.5
LOG2E = 1.4426950408889634
LN2 = 0.6931471805599453
RMS_EPS = 1e-6
ROPE_THETA = 10000.0
PAGE = 128
CMP_STRIDE = 16
CMP_BLOCK = 32
SEL_BLOCK = 64
N_SEL = 16
A_WINDOW = 512
B_WINDOW = 128
D_GROUPS = ((128, 1), (512, 4), (2048, 16))
FORCE_SCORE = 1e9
NEG = -1e30
DEAD = -3e38
LANE = 128
SUBLANE = 8
ONES_ROWS = 16
VMEM_LIMIT = 52 * 1024 * 1024


def _params(sem, vmem=VMEM_LIMIT):
    return pltpu.CompilerParams(dimension_semantics=sem, vmem_limit_bytes=vmem)


def _nt(a, b):
    return lax.dot_general(a, b, (((1,), (1,)), ((), ())), preferred_element_type=F32)


def _dot(a, b):
    return jnp.dot(a, b, preferred_element_type=F32)


def _split3(x):
    hi = x.astype(BF16)
    r1 = x - hi.astype(F32)
    mid = r1.astype(BF16)
    lo = (r1 - mid.astype(F32)).astype(BF16)
    return hi, mid, lo


def _rms(x, g):
    var = jnp.mean(x * x, axis=-1, keepdims=True)
    return x * lax.rsqrt(var + RMS_EPS) * g


def _rope_chunk(x, cos2, sin2):
    lane = lax.broadcasted_iota(jnp.int32, x.shape, 1)
    first = (lane % HD) < (HD // 2)
    swapped = jnp.where(first, pltpu.roll(x, LANE - HD // 2, axis=1), pltpu.roll(x, HD // 2, axis=1))
    return x * cos2 + swapped * sin2


def _rope_tables(pos):
    half = HD // 2
    inv = ROPE_THETA ** (-jnp.arange(half, dtype=F32) / half)
    ang = pos.astype(F32)[:, None] * inv[None, :]
    c, s = jnp.cos(ang), jnp.sin(ang)
    return jnp.tile(c, (1, 4)), jnp.tile(jnp.concatenate([-s, s], axis=1), (1, 2))


def _norm_proj_kernel(x_ref, g_ref, w_ref, cos_ref, sin_ref, *out_refs, segs, flipped):
    xn = _rms(x_ref[...], g_ref[...]).astype(BF16)
    outs = list(out_refs)
    written = []
    for c0, c1, r0, r1, dup in segs:
        acc = _dot(xn, w_ref[:, c0:c1])
        if dup:
            outs.pop(0)[...] = acc
        o_ref = outs.pop(0)
        written.append(o_ref)
        if r1 <= r0:
            o_ref[...] = acc
            continue
        if r0 > 0:
            o_ref[:, 0:r0] = acc[:, 0:r0]
        for a in range(r0, r1, LANE):
            o_ref[:, a:a + LANE] = _rope_chunk(acc[:, a:a + LANE], cos_ref[...], sin_ref[...])
        if r1 < c1 - c0:
            o_ref[:, r1:] = acc[:, r1:]
    for idx in flipped:
        outs.pop(0)[0] = written[idx][...].T


def norm_proj(x, g, w, cos2, sin2, segs, tm, flipped=(), seq=None):
    n, d = x.shape
    tm = min(tm, n)
    out_shape, out_specs = [], []
    for c0, c1, _, _, dup in segs:
        for _ in range(2 if dup else 1):
            out_shape.append(jax.ShapeDtypeStruct((n, c1 - c0), F32))
            out_specs.append(pl.BlockSpec((tm, c1 - c0), lambda i: (i, 0)))
    for idx in flipped:
        width = segs[idx][1] - segs[idx][0]
        per_seq = seq // tm
        out_shape.append(jax.ShapeDtypeStruct((n // seq, width, seq), F32))
        out_specs.append(pl.BlockSpec((1, width, tm), lambda i: (i // per_seq, 0, i % per_seq)))
    return pl.pallas_call(
        functools.partial(_norm_proj_kernel, segs=tuple(segs), flipped=tuple(flipped)),
        grid=(n // tm,),
        in_specs=[pl.BlockSpec((tm, d), lambda i: (i, 0)),
                  pl.BlockSpec((1, d), lambda i: (0, 0)),
                  pl.BlockSpec(w.shape, lambda i: (0, 0)),
                  pl.BlockSpec((tm, LANE), lambda i: (i % (cos2.shape[0] // tm), 0)),
                  pl.BlockSpec((tm, LANE), lambda i: (i % (sin2.shape[0] // tm), 0))],
        out_specs=out_specs, out_shape=out_shape,
        compiler_params=_params(("parallel",)), name="norm_proj",
    )(x, g.reshape(1, d), w, cos2, sin2)


def _out_proj_kernel(o_ref, h_ref, w_ref, out_ref):
    out_ref[...] = h_ref[...] + _dot(o_ref[...].astype(BF16), w_ref[...])


def out_proj(o, h, w, tm=512):
    n, k = o.shape
    d = w.shape[1]
    tm = min(tm, n)
    return pl.pallas_call(
        _out_proj_kernel, grid=(n // tm,),
        in_specs=[pl.BlockSpec((tm, k), lambda i: (i, 0)), pl.BlockSpec((tm, d), lambda i: (i, 0)),
                  pl.BlockSpec(w.shape, lambda i: (0, 0))],
        out_specs=pl.BlockSpec((tm, d), lambda i: (i, 0)),
        out_shape=jax.ShapeDtypeStruct((n, d), F32),
        compiler_params=_params(("parallel",)), name="out_proj",
    )(o, h, w)


def _merge_a_kernel(oc_ref, os_ref, ow_ref, g_ref, h_ref, w_ref, out_ref, o_sc, *, heads):
    gate = jax.nn.sigmoid(g_ref[...])
    for h in range(heads):
        sl = slice(h * HD, (h + 1) * HD)
        o = (gate[:, 3 * h:3 * h + 1] * oc_ref[:, sl] + gate[:, 3 * h + 1:3 * h + 2] * os_ref[:, sl]
             + gate[:, 3 * h + 2:3 * h + 3] * ow_ref[:, sl])
        o_sc[:, sl] = o
    out_ref[...] = h_ref[...] + _dot(o_sc[...].astype(BF16), w_ref[...])


def merge_a(o_cmp, o_sel, o_win, gates, h, w, tm=512):
    n, k = o_cmp.shape
    d = w.shape[1]
    tm = min(tm, n)
    row = lambda c: pl.BlockSpec((tm, c), lambda i: (i, 0))
    return pl.pallas_call(
        functools.partial(_merge_a_kernel, heads=k // HD), grid=(n // tm,),
        in_specs=[row(k), row(k), row(k), row(gates.shape[1]), row(d), pl.BlockSpec(w.shape, lambda i: (0, 0))],
        out_specs=row(d), out_shape=jax.ShapeDtypeStruct((n, d), F32),
        scratch_shapes=[pltpu.VMEM((tm, k), F32)],
        compiler_params=_params(("parallel",)), name="merge_a",
    )(o_cmp, o_sel, o_win, gates, h, w)


def _merge_d_kernel(o1, o2, o3, l1, l2, l3, h_ref, w_ref, out_ref):
    a, b, c = l1[...], l2[...], l3[...]
    m = jnp.maximum(jnp.maximum(a, b), c)
    ea, eb, ec = jnp.exp(a - m), jnp.exp(b - m), jnp.exp(c - m)
    den = ea + eb + ec
    o = (ea / den) * o1[...] + (eb / den) * o2[...] + (ec / den) * o3[...]
    out_ref[...] = h_ref[...] + _dot(o.astype(BF16), w_ref[...])


def merge_d(outs, lses, h, w, tm=512):
    n, k = outs[0].shape
    d = w.shape[1]
    tm = min(tm, n)
    row = lambda c: pl.BlockSpec((tm, c), lambda i: (i, 0))
    return pl.pallas_call(
        _merge_d_kernel, grid=(n // tm,),
        in_specs=[row(k)] * 6 + [row(d), pl.BlockSpec(w.shape, lambda i: (0, 0))],
        out_specs=row(d), out_shape=jax.ShapeDtypeStruct((n, d), F32),
        compiler_params=_params(("parallel",)), name="merge_d",
    )(*outs, *lses, h, w)


def _ffn_kernel(h_ref, g_ref, wg_ref, wu_ref, wd_ref, out_ref, xn_sc, acc_sc):
    j = pl.program_id(1)

    @pl.when(j == 0)
    def _():
        x = h_ref[...]
        xn_sc[...] = _rms(x, g_ref[...]).astype(BF16)
        acc_sc[...] = x

    xn = xn_sc[...]
    a = _dot(xn, wg_ref[...])
    u = _dot(xn, wu_ref[...])
    mid = (a * jax.nn.sigmoid(a) * u).astype(BF16)
    acc_sc[...] += _dot(mid, wd_ref[...])

    @pl.when(j == pl.num_programs(1) - 1)
    def _():
        out_ref[...] = acc_sc[...]


def ffn(h, g, wg, wu, wd, tm=512, n_f=2):
    n, d = h.shape
    f = wg.shape[1]
    tm = min(tm, n)
    tf = f // n_f
    return pl.pallas_call(
        _ffn_kernel, grid=(n // tm, n_f),
        in_specs=[pl.BlockSpec((tm, d), lambda i, j: (i, 0)), pl.BlockSpec((1, d), lambda i, j: (0, 0)),
                  pl.BlockSpec((d, tf), lambda i, j: (0, j)), pl.BlockSpec((d, tf), lambda i, j: (0, j)),
                  pl.BlockSpec((tf, d), lambda i, j: (j, 0))],
        out_specs=pl.BlockSpec((tm, d), lambda i, j: (i, 0)),
        out_shape=jax.ShapeDtypeStruct((n, d), F32),
        scratch_shapes=[pltpu.VMEM((tm, d), BF16), pltpu.VMEM((tm, d), F32)],
        compiler_params=_params(("parallel", "arbitrary")), name="ffn",
    )(h, g.reshape(1, d), wg, wu, wd)


def _final_norm_kernel(x_ref, g_ref, o_ref):
    o_ref[...] = _rms(x_ref[...], g_ref[...])


def final_norm(x, g, tm=1024):
    n, d = x.shape
    tm = min(tm, n)
    return pl.pallas_call(
        _final_norm_kernel, grid=(n // tm,),
        in_specs=[pl.BlockSpec((tm, d), lambda i: (i, 0)), pl.BlockSpec((1, d), lambda i: (0, 0))],
        out_specs=pl.BlockSpec((tm, d), lambda i: (i, 0)),
        out_shape=jax.ShapeDtypeStruct((n, d), F32),
        compiler_params=_params(("parallel",)), name="final_norm",
    )(x, g.reshape(1, d))


def _fox_gate_kernel(f_ref, b_ref, lf_ref, cum_ref):
    x = f_ref[0] + b_ref[...]
    lf = jnp.minimum(x, 0.0) - jnp.log1p(jnp.exp(-jnp.abs(x)))
    lf_ref[0] = lf
    n = x.shape[1]
    lane = lax.broadcasted_iota(jnp.int32, x.shape, 1)
    c = lf
    s = 1
    while s < n:
        c = c + jnp.where(lane >= s, pltpu.roll(c, s, axis=1), 0.0)
        s *= 2
    cum_ref[0] = c


def fox_gate(f_t, b_f):
    b, hh, n = f_t.shape
    spec = pl.BlockSpec((1, hh, n), lambda i: (i, 0, 0))
    return pl.pallas_call(
        _fox_gate_kernel, grid=(b,),
        in_specs=[spec, pl.BlockSpec((hh, 1), lambda i: (0, 0))],
        out_specs=[spec, spec], out_shape=[jax.ShapeDtypeStruct(f_t.shape, F32)] * 2,
        compiler_params=_params(("parallel",)), name="fox_gate",
    )(f_t, b_f.reshape(hh, 1))


def _attn_init(m_sc, acc_sc):
    m_sc[...] = jnp.full(m_sc.shape, NEG, F32)
    acc_sc[...] = jnp.zeros(acc_sc.shape, F32)


def _value_rows(v_ref):
    vt = v_ref[0].T.astype(BF16)
    ones = jnp.ones((ONES_ROWS, vt.shape[1]), BF16)
    return [jnp.concatenate([vt[kv * HD:(kv + 1) * HD], ones], axis=0) for kv in range(vt.shape[0] // HD)]


def _attn_step(h, k, vt, q, addends, tq, m_sc, acc_sc):
    s = _nt(k, q)
    tk = s.shape[0]
    per_query = None
    if addends is not None and any(t.shape[0] == 1 for terms in addends for t in terms):
        per_query = [[t for t in terms if t.shape[0] == 1] for terms in addends]
        per_query = [sum(ts[1:], ts[0]) if ts else jnp.zeros((1, tq), F32) for ts in per_query]
        per_query = per_query[0] if len(per_query) == 1 else jnp.concatenate(per_query, axis=1)

    def rows(r0, r1):
        blk = s[r0:r1]
        if addends is None:
            return blk
        pieces = []
        for g, terms in enumerate(addends):
            piece = blk[:, g * tq:(g + 1) * tq]
            for term in terms:
                if term.shape[0] != 1:
                    piece = piece + term[r0:r1]
            pieces.append(piece)
        return pieces[0] if len(pieces) == 1 else jnp.concatenate(pieces, axis=1)

    mx = rows(0, SUBLANE)
    for r0 in range(SUBLANE, tk, SUBLANE):
        mx = jnp.maximum(mx, rows(r0, r0 + SUBLANE))
    mx = jnp.max(mx, axis=0, keepdims=True)
    m_prev = m_sc[h]
    m_new = jnp.maximum(m_prev, mx if per_query is None else mx + per_query)
    shift = m_new if per_query is None else m_new - per_query
    strip = 2 * SUBLANE
    p = jnp.concatenate([jnp.exp2(rows(r0, r0 + strip) - shift).astype(BF16) for r0 in range(0, tk, strip)], axis=0)
    acc_sc[h] = jnp.exp2(m_prev - m_new) * acc_sc[h] + _dot(vt, p)
    m_sc[h] = m_new


def _attn_finish(hq, group, tq, m_sc, acc_sc, ot_sc, o_ref, sink_ref=None, lset_sc=None, lse_ref=None):
    for h in range(hq):
        kv, g = divmod(h, group)
        cols = slice(g * tq, (g + 1) * tq)
        num, l, m = acc_sc[kv, 0:HD, cols], acc_sc[kv, HD:HD + 1, cols], m_sc[kv, :, cols]
        if sink_ref is not None:
            sk = sink_ref[:, h:h + 1] * LOG2E
            m_f = jnp.maximum(m, sk)
            w_old = jnp.exp2(m - m_f)
            l = l * w_old + jnp.exp2(sk - m_f)
            num = num * w_old
            m = m_f
        ot_sc[h * HD:(h + 1) * HD, :] = num / l
        if lset_sc is not None:
            lset_sc[h * HD:(h + 1) * HD, :] = jnp.broadcast_to(m * LN2 + jnp.log(l), num.shape)
    o_ref[0] = ot_sc[...].T
    if lset_sc is not None:
        lse_ref[0] = lset_sc[...].T


def _q_group(q_ref, kv, group):
    heads = [(q_ref[0, :, h * HD:(h + 1) * HD] * (SCALE * LOG2E)).astype(BF16)
             for h in range(kv * group, (kv + 1) * group)]
    return heads[0] if group == 1 else jnp.concatenate(heads, axis=0)


def _band_kernel(*refs, t, nb, n_back, dil, hq, group, with_sink, with_lse):
    refs = list(refs)
    q_ref, k_ref, v_ref = refs[:3]
    pos = 3
    sink_ref = None
    if with_sink:
        sink_ref = refs[pos]
        pos += 1
    o_ref = refs[pos]
    pos += 1
    lse_ref = None
    if with_lse:
        lse_ref = refs[pos]
        pos += 1
    m_sc, acc_sc, ot_sc = refs[pos:pos + 3]
    lset_sc = refs[pos + 3] if with_lse else None
    qi, j = pl.program_id(1), pl.program_id(2)
    kt = qi - nb + j

    @pl.when(j == 0)
    def _():
        _attn_init(m_sc, acc_sc)

    @pl.when(kt >= 0)
    def _():
        dist = (qi * t + lax.broadcasted_iota(jnp.int32, (t, t), 1)) - (kt * t + lax.broadcasted_iota(jnp.int32, (t, t), 0))
        band = jnp.where(dist >= 0, dist, n_back + 1)
        if dil > 1:
            band = jnp.where((dist & (dil - 1)) == 0, band, n_back + 1)
        addends = [[jnp.where(band <= n_back, 0.0, NEG)]] * group
        vts = _value_rows(v_ref)
        for kv in range(hq // group):
            k = k_ref[0, :, kv * HD:(kv + 1) * HD].astype(BF16)
            _attn_step(kv, k, vts[kv], _q_group(q_ref, kv, group), addends, t, m_sc, acc_sc)

    @pl.when(j == nb)
    def _():
        _attn_finish(hq, group, t, m_sc, acc_sc, ot_sc, o_ref, sink_ref, lset_sc, lse_ref)


def band_attention(q, kv, n_back, dil, t, sink=None, with_lse=False):
    b, l, cq = q.shape
    ck = kv.shape[2] // 2
    nb = -(-n_back // t)
    hq, hkv = cq // HD, ck // HD
    kmap = lambda blk: (lambda bi, qi, j: (bi, jnp.maximum(qi - nb + j, 0), blk))
    in_specs = [pl.BlockSpec((1, t, cq), lambda bi, qi, j: (bi, qi, 0)),
                pl.BlockSpec((1, t, ck), kmap(0)), pl.BlockSpec((1, t, ck), kmap(1))]
    args = [q, kv, kv]
    if sink is not None:
        in_specs.append(pl.BlockSpec((1, hq), lambda bi, qi, j: (0, 0)))
        args.append(sink.reshape(1, hq))
    o_spec = pl.BlockSpec((1, t, cq), lambda bi, qi, j: (bi, qi, 0))
    o_shape = jax.ShapeDtypeStruct(q.shape, F32)
    wide = hq // hkv * t
    scratch = [pltpu.VMEM((hkv, 1, wide), F32), pltpu.VMEM((hkv, HD + ONES_ROWS, wide), F32), pltpu.VMEM((cq, t), F32)]
    if with_lse:
        scratch.append(pltpu.VMEM((cq, t), F32))
    return pl.pallas_call(
        functools.partial(_band_kernel, t=t, nb=nb, n_back=n_back, dil=dil, hq=hq, group=hq // hkv,
                          with_sink=sink is not None, with_lse=with_lse),
        grid=(b, l // t, nb + 1),
        in_specs=in_specs,
        out_specs=[o_spec, o_spec] if with_lse else o_spec,
        out_shape=[o_shape, o_shape] if with_lse else o_shape,
        scratch_shapes=scratch,
        compiler_params=_params(("parallel", "parallel", "arbitrary")), name="band_attention",
    )(*args)


def _dilated_kernel(q_ref, k_ref, v_ref, o_ref, lse_ref, m_sc, acc_sc, *, dil, sub, n_back):
    qi, j = pl.program_id(2), pl.program_id(3)
    kt = qi - 1 + j

    @pl.when(j == 0)
    def _():
        _attn_init(m_sc, acc_sc)

    @pl.when(kt >= 0)
    def _():
        dist = (qi * sub + lax.broadcasted_iota(jnp.int32, (sub, sub), 1)) - (kt * sub + lax.broadcasted_iota(jnp.int32, (sub, sub), 0))
        addends = [[jnp.where(jnp.where(dist >= 0, dist, n_back + 1) <= n_back, 0.0, NEG)]]
        ones = jnp.ones((ONES_ROWS, sub), BF16)
        for r in range(dil):
            rows = pl.ds(r, sub, stride=dil)
            q = (q_ref[0, rows, :] * (SCALE * LOG2E)).astype(BF16)
            k = k_ref[0, rows, :].astype(BF16)
            vt = v_ref[0, rows, :].T.astype(BF16)
            for h in range(LANE // HD):
                cols = slice(h * HD, (h + 1) * HD)
                _attn_step(r * (LANE // HD) + h, k[:, cols], jnp.concatenate([vt[cols], ones], axis=0), q[:, cols],
                           addends, sub, m_sc, acc_sc)

    @pl.when(j == 1)
    def _():
        for r in range(dil):
            outs, lses = [], []
            for h in range(LANE // HD):
                slot = r * (LANE // HD) + h
                num, l = acc_sc[slot, 0:HD], acc_sc[slot, HD:HD + 1]
                outs.append(num / l)
                lses.append(jnp.broadcast_to(m_sc[slot] * LN2 + jnp.log(l), num.shape))
            rows = pl.ds(r, sub, stride=dil)
            o_ref[0, rows, :] = jnp.concatenate(outs, axis=0).T
            lse_ref[0, rows, :] = jnp.concatenate(lses, axis=0).T


def dilated_attention(q, kv, window, dil):
    b, l, cq = q.shape
    sub = window // dil
    n_pair = cq // LANE
    kmap = lambda off: (lambda bi, c, qi, j: (bi, jnp.maximum(qi - 1 + j, 0), off + c))
    spec = pl.BlockSpec((1, window, LANE), lambda bi, c, qi, j: (bi, qi, c))
    slots = dil * (LANE // HD)
    return pl.pallas_call(
        functools.partial(_dilated_kernel, dil=dil, sub=sub, n_back=sub),
        grid=(b, n_pair, l // window, 2),
        in_specs=[spec, pl.BlockSpec((1, window, LANE), kmap(0)), pl.BlockSpec((1, window, LANE), kmap(n_pair))],
        out_specs=[spec, spec], out_shape=[jax.ShapeDtypeStruct(q.shape, F32)] * 2,
        scratch_shapes=[pltpu.VMEM((slots, 1, sub), F32), pltpu.VMEM((slots, HD + ONES_ROWS, sub), F32)],
        compiler_params=_params(("parallel", "parallel", "parallel", "arbitrary")), name="dilated_attention",
    )(q, kv, kv)


def _flash_kernel(qt_ref, kt_ref, *refs, t, hq, group, mode):
    if mode == "fox":
        q_ref, k_ref, v_ref, cq_ref, ck_ref, o_ref, m_sc, acc_sc, ot_sc = refs
    else:
        q_ref, k_ref, v_ref, sel_ref, o_ref, m_sc, acc_sc, ot_sc = refs
    step = pl.program_id(2)
    qi, kj = qt_ref[step], kt_ref[step]

    @pl.when(kj == 0)
    def _():
        _attn_init(m_sc, acc_sc)

    def body(diagonal):
        vts = _value_rows(v_ref)
        causal = lax.broadcasted_iota(jnp.int32, (t, t), 1) >= lax.broadcasted_iota(jnp.int32, (t, t), 0)
        if mode == "sel":
            selb = sel_ref[0].astype(BF16)
            width = selb.shape[1]
            n_blk = width // (hq // group)
            key_blk = (kj * t + lax.broadcasted_iota(jnp.int32, (t, width), 0)) // SEL_BLOCK
            col = lax.broadcasted_iota(jnp.int32, (t, width), 1)
        for kv in range(hq // group):
            k = k_ref[0, :, kv * HD:(kv + 1) * HD].astype(BF16)
            if mode == "fox":
                tail = [jnp.where(causal, 0.0, NEG)] if diagonal else []
                addends = [[cq_ref[0, 0, h:h + 1, :] * LOG2E, ck_ref[0, 0, :, h:h + 1] * (-LOG2E)] + tail
                           for h in range(kv * group, (kv + 1) * group)]
            else:
                expand = jnp.where(col == kv * n_blk + key_blk, 1.0, 0.0).astype(BF16)
                flag = _nt(expand, selb)
                if diagonal:
                    flag = jnp.where(causal, flag, 0.0)
                addends = [[jnp.where(flag > 0.5, 0.0, NEG)]] * group
            _attn_step(kv, k, vts[kv], _q_group(q_ref, kv, group), addends, t, m_sc, acc_sc)

    @pl.when(kj < qi)
    def _():
        body(False)

    @pl.when(kj == qi)
    def _():
        body(True)
        _attn_finish(hq, group, t, m_sc, acc_sc, ot_sc, o_ref)


def flash_attention(q, kv, t, cq, n_col, mode, extra):
    b, l, _ = q.shape
    hq = cq // HD
    nq = l // t
    pairs = [(qi, kj) for qi in range(nq) for kj in range(qi + 1)]
    qt = jnp.asarray([p[0] for p in pairs], jnp.int32)
    kt = jnp.asarray([p[1] for p in pairs], jnp.int32)
    kmap = lambda off: (lambda bi, c, s, qt, kt: (bi, kt[s], off + c))
    in_specs = [pl.BlockSpec((1, t, cq), lambda bi, c, s, qt, kt: (bi, qt[s], c)),
                pl.BlockSpec((1, t, LANE), kmap(0)), pl.BlockSpec((1, t, LANE), kmap(n_col))]
    if mode == "fox":
        in_specs += [pl.BlockSpec((1, 1, hq, t), lambda bi, c, s, qt, kt: (bi, c, 0, qt[s])),
                     pl.BlockSpec((1, 1, t, hq), lambda bi, c, s, qt, kt: (bi, c, kt[s], 0))]
    else:
        in_specs += [pl.BlockSpec((1, t, extra[0].shape[2]), lambda bi, c, s, qt, kt: (bi, qt[s], 0))]
    return pl.pallas_call(
        functools.partial(_flash_kernel, t=t, hq=hq, group=hq // 2, mode=mode),
        grid_spec=pltpu.PrefetchScalarGridSpec(
            num_scalar_prefetch=2, grid=(b, n_col, len(pairs)), in_specs=in_specs,
            out_specs=pl.BlockSpec((1, t, cq), lambda bi, c, s, qt, kt: (bi, qt[s], c)),
            scratch_shapes=[pltpu.VMEM((2, 1, hq // 2 * t), F32), pltpu.VMEM((2, HD + ONES_ROWS, hq // 2 * t), F32),
                            pltpu.VMEM((cq, t), F32)]),
        out_shape=jax.ShapeDtypeStruct(q.shape, F32),
        compiler_params=_params(("parallel", "parallel", "arbitrary")), name="flash_" + mode,
    )(qt, kt, q, kv, kv, *extra)


def _compress_kernel(*refs, n_src, chunks, paged):
    refs = refs[1:] if paged else refs
    n_in = n_src if paged else 2
    src = refs[:n_in]
    w1_ref, b1_ref, w2_ref, out_ref, x_sc = refs[n_in:n_in + 5]
    total = n_src * chunks
    if paged:
        tok_sc = refs[n_in + 5]
    fold = max(1, 2 * SUBLANE // chunks)

    def token_rows(i0, j, s):
        pick = pl.ds(s, chunks, stride=CMP_STRIDE)
        if paged:
            return jnp.concatenate([tok_sc[f, j, pick, :] for f in range(fold)], axis=0)
        return src[j][0, pick, :]

    for i0 in range(0, n_src, fold):
        r0, r1 = i0 * chunks, (i0 + fold) * chunks
        for j in range(2):
            if paged:
                for f in range(fold):
                    tok_sc[f, j] = src[i0 + f][0, j * LANE:(j + 1) * LANE, :].T
            for s in range(0, CMP_STRIDE, 2):
                a, b = token_rows(i0, j, s), token_rows(i0, j, s + 1)
                low = lax.broadcasted_iota(jnp.int32, a.shape, 1) < HD
                cols = slice(s * HD, (s + 2) * HD)
                x_sc[j, r0:r1, cols] = jnp.where(low, a, pltpu.roll(b, HD, axis=1)).astype(BF16)
                x_sc[j, total + r0:total + r1, cols] = jnp.where(low, pltpu.roll(a, HD, axis=1), b).astype(BF16)
    for j in range(2):
        parts = _dot(x_sc[j], w1_ref[j])
        half = parts.shape[1] // 2
        pre = b1_ref[j] + parts[:, :half] + pltpu.roll(parts[:, half:], 2 * total - 1, axis=0)
        act = pre * jax.nn.sigmoid(pre)
        res = _dot(act.astype(BF16), w2_ref[j])
        for kv in range(2):
            out_ref[0, :, j * LANE + kv * HD:j * LANE + (kv + 1) * HD] = res[kv * total:(kv + 1) * total]


def compress(x, page_table, w1p, b1p, w2p):
    const = lambda nd: (lambda *_: (0,) * nd)
    w_specs = [pl.BlockSpec(w1p.shape, const(3)), pl.BlockSpec(b1p.shape, const(3)), pl.BlockSpec(w2p.shape, const(3))]
    if page_table is None:
        b, l, width = x.shape
        n_src, chunks = 1, l // CMP_STRIDE
        scratch = [pltpu.VMEM((2, 2 * chunks, CMP_STRIDE * HD), BF16)]
        halves = [pl.BlockSpec((1, l, LANE), lambda i: (i, 0, 0)), pl.BlockSpec((1, l, LANE), lambda i: (i, 0, 1))]
        grid_spec = pl.GridSpec(
            grid=(b,), in_specs=halves + w_specs,
            out_specs=pl.BlockSpec((1, chunks, width), lambda i: (i, 0, 0)), scratch_shapes=scratch)
        args = (x, x, w1p, b1p, w2p)
    else:
        b, n_src = page_table.shape
        _, width, rows = x.shape
        chunks = rows // CMP_STRIDE
        fold = max(1, 2 * SUBLANE // chunks)
        scratch = [pltpu.VMEM((2, 2 * n_src * chunks, CMP_STRIDE * HD), BF16),
                   pltpu.VMEM((fold, 2, rows, LANE), F32)]
        page_spec = lambda p: pl.BlockSpec((1, width, rows), lambda i, pt: (pt[i, p], 0, 0))
        grid_spec = pltpu.PrefetchScalarGridSpec(
            num_scalar_prefetch=1, grid=(b,), in_specs=[page_spec(p) for p in range(n_src)] + w_specs,
            out_specs=pl.BlockSpec((1, n_src * chunks, width), lambda i, pt: (i, 0, 0)), scratch_shapes=scratch)
        args = (page_table,) + (x,) * n_src + (w1p, b1p, w2p)
    return pl.pallas_call(
        functools.partial(_compress_kernel, n_src=n_src, chunks=chunks, paged=page_table is not None),
        grid_spec=grid_spec,
        out_shape=jax.ShapeDtypeStruct((b, n_src * chunks, width), F32),
        compiler_params=_params(("arbitrary",)), name="compress",
    )(*args)


def _compress_weights(phi_w1, phi_b1, phi_w2):
    hidden = phi_w1.shape[-1]
    parts = CMP_BLOCK // CMP_STRIDE
    w1p = phi_w1.reshape(2, parts, CMP_STRIDE * HD, hidden).transpose(0, 2, 1, 3).reshape(2, CMP_STRIDE * HD, parts * hidden)
    return w1p.astype(BF16), phi_b1[:, None, :], phi_w2.astype(BF16)


def _top_blocks(imp, n_take):
    lane = lax.broadcasted_iota(jnp.int32, imp.shape, 1).astype(F32)
    sel = jnp.zeros(imp.shape, F32)
    cur = imp
    for _ in range(n_take):
        mx = jnp.max(cur, axis=1, keepdims=True)
        first = jnp.min(jnp.where(cur == mx, lane, 1e9), axis=1, keepdims=True)
        pick = lane == first
        sel = jnp.where(pick, 1.0, sel)
        cur = jnp.where(pick, DEAD, cur)
    return sel


def _cmp_prompt_kernel(q_ref, kvc_ref, cover_ref, o_ref, sel_ref, *, t, n_cmp, n_kv, group, n_blk):
    qi = pl.program_id(1)
    rows = kvc_ref.shape[1]
    qpos = qi * t + lax.broadcasted_iota(jnp.int32, (t, 1), 0)
    ci = lax.broadcasted_iota(jnp.int32, (1, rows), 1)
    valid = (ci * CMP_STRIDE + CMP_BLOCK - 1 <= qpos) & (ci < n_cmp)
    blk = lax.broadcasted_iota(jnp.int32, (1, n_blk), 1)
    cur = qpos // SEL_BLOCK
    forced = (blk == 0) | (blk == cur) | (blk == cur - 1)
    cover = cover_ref[...]
    for kv in range(n_kv):
        kc = kvc_ref[0, :, kv * HD:(kv + 1) * HD].astype(BF16)
        vc = kvc_ref[0, :, (n_kv + kv) * HD:(n_kv + kv + 1) * HD].astype(BF16)
        imp = jnp.zeros((t, n_blk), F32)
        for g in range(group):
            h = kv * group + g
            s = jnp.where(valid, _nt(q_ref[0, :, h * HD:(h + 1) * HD].astype(BF16), kc) * SCALE, NEG)
            m = jnp.max(s, axis=1, keepdims=True)
            e = jnp.where(valid, jnp.exp(s - m), 0.0)
            den = jnp.sum(e, axis=1, keepdims=True)
            p = (e / jnp.where(den > 0, den, 1.0)).astype(BF16)
            o_ref[0, :, h * HD:(h + 1) * HD] = _dot(p, vc)
            imp = imp + _dot(p, cover)
        imp = jnp.where(blk * SEL_BLOCK <= qpos, jnp.where(forced, FORCE_SCORE, imp), NEG)
        sel_ref[0, :, kv * n_blk:(kv + 1) * n_blk] = _top_blocks(imp, min(N_SEL, n_blk))


def _cover_matrix(rows, n_cmp, n_blk, width):
    c_start = np.arange(rows)[:, None] * CMP_STRIDE
    s_start = np.arange(width)[None, :] * SEL_BLOCK
    cov = (c_start < s_start + SEL_BLOCK) & (c_start + CMP_BLOCK > s_start)
    cov &= (np.arange(rows)[:, None] < n_cmp) & (np.arange(width)[None, :] < n_blk)
    return jnp.asarray(cov, dtype=BF16)


def cmp_attend_prompt(q, kvc, t, n_kv):
    b, l, cq = q.shape
    rows = kvc.shape[1]
    n_blk = l // SEL_BLOCK
    cover = _cover_matrix(rows, rows - 1, n_blk, n_blk)
    return pl.pallas_call(
        functools.partial(_cmp_prompt_kernel, t=t, n_cmp=rows - 1, n_kv=n_kv, group=cq // HD // n_kv, n_blk=n_blk),
        grid=(b, l // t),
        in_specs=[pl.BlockSpec((1, t, cq), lambda bi, qi: (bi, qi, 0)),
                  pl.BlockSpec((1, rows, kvc.shape[2]), lambda bi, qi: (bi, 0, 0)),
                  pl.BlockSpec(cover.shape, lambda bi, qi: (0, 0))],
        out_specs=[pl.BlockSpec((1, t, cq), lambda bi, qi: (bi, qi, 0)),
                   pl.BlockSpec((1, t, n_kv * n_blk), lambda bi, qi: (bi, qi, 0))],
        out_shape=[jax.ShapeDtypeStruct(q.shape, F32), jax.ShapeDtypeStruct((b, l, n_kv * n_blk), F32)],
        compiler_params=_params(("parallel", "parallel")), name="cmp_attend_prompt",
    )(q, kvc, cover)


def _diag_heads(x, rows_kv, n_kv):
    out = jnp.zeros((x.shape[0], HD), F32)
    for kv in range(n_kv):
        out = out + jnp.where(rows_kv == kv, x[:, kv * HD:(kv + 1) * HD], 0.0)
    return out


def _cmp_sample_kernel(q_ref, kvc_ref, cover_ref, o_ref, sel_ref, *, n_tok, heads, n_kv, n_cmp, past_len, n_blk):
    rows = kvc_ref.shape[1]
    r_all = n_tok * heads
    c = n_kv * HD
    width = cover_ref.shape[1]
    group = heads // n_kv
    ridx = lax.broadcasted_iota(jnp.int32, (r_all, 1), 0)
    qpos = past_len + ridx // heads
    ci = lax.broadcasted_iota(jnp.int32, (1, rows), 1)
    valid = (ci * CMP_STRIDE + CMP_BLOCK - 1 <= qpos) & (ci < n_cmp)
    g_rows = r_all // group
    n_seq = q_ref.shape[0]
    imps = []
    for i in range(n_seq):
        kc = kvc_ref[i, :, 0:c].astype(BF16)
        vc = kvc_ref[i, :, c:2 * c].astype(BF16)
        s = jnp.where(valid, _nt(q_ref[i].astype(BF16), kc) * SCALE, NEG)
        m = jnp.max(s, axis=1, keepdims=True)
        e = jnp.where(valid, jnp.exp(s - m), 0.0)
        den = jnp.sum(e, axis=1, keepdims=True)
        p = (e / jnp.where(den > 0, den, 1.0)).astype(BF16)
        o_ref[i] = _diag_heads(_dot(p, vc), (ridx % heads) // group, n_kv)
        imps.append(jnp.sum(_dot(p, cover_ref[...]).reshape(g_rows, group, width), axis=1))
    imp = jnp.concatenate(imps, axis=0)
    gpos = past_len + (lax.broadcasted_iota(jnp.int32, (n_seq * g_rows, 1), 0) % g_rows) // n_kv
    blk = lax.broadcasted_iota(jnp.int32, (1, width), 1)
    cur = gpos // SEL_BLOCK
    forced = (blk == 0) | (blk == cur) | (blk == cur - 1)
    imp = jnp.where(blk * SEL_BLOCK <= gpos, jnp.where(forced, FORCE_SCORE, imp), NEG)
    imp = jnp.where(blk < n_blk, imp, DEAD)
    sel = _top_blocks(imp, min(N_SEL, n_blk))
    for i in range(n_seq):
        sel_ref[i] = sel[i * g_rows:(i + 1) * g_rows]


def cmp_attend_sample(qbd, kvc, n_tok, heads, n_kv, past_len):
    b, r_all, c = qbd.shape
    rows = kvc.shape[1]
    n_blk = past_len // SEL_BLOCK + -(-n_tok // SEL_BLOCK)
    width = -(-n_blk // LANE) * LANE
    cover = _cover_matrix(rows, rows - 1, n_blk, width)
    g_rows = r_all // (heads // n_kv)
    n_seq = next(n for n in (8, 4, 2, 1) if b % n == 0)
    return pl.pallas_call(
        functools.partial(_cmp_sample_kernel, n_tok=n_tok, heads=heads, n_kv=n_kv, n_cmp=rows - 1,
                          past_len=past_len, n_blk=n_blk),
        grid=(b // n_seq,),
        in_specs=[pl.BlockSpec((n_seq, r_all, c), lambda i: (i, 0, 0)),
                  pl.BlockSpec((n_seq, rows, 2 * c), lambda i: (i, 0, 0)),
                  pl.BlockSpec(cover.shape, lambda i: (0, 0))],
        out_specs=[pl.BlockSpec((n_seq, r_all, HD), lambda i: (i, 0, 0)),
                   pl.BlockSpec((n_seq, g_rows, width), lambda i: (i, 0, 0))],
        out_shape=[jax.ShapeDtypeStruct((b, r_all, HD), F32), jax.ShapeDtypeStruct((b, g_rows, width), F32)],
        compiler_params=_params(("parallel",)), name="cmp_attend_sample",
    )(qbd, kvc, cover)


def _dec_init(m_sc, l_sc, acc_sc):
    m_sc[...] = jnp.full(m_sc.shape, NEG, F32)
    l_sc[...] = jnp.zeros(l_sc.shape, F32)
    acc_sc[...] = jnp.zeros(acc_sc.shape, F32)


def _dec_update(qb, kt, vt, bias, mask, m_sc, l_sc, acc_sc):
    s = _dot(qb, kt) * SCALE
    if bias is not None:
        s = s + bias
    if mask is not None:
        s = jnp.where(mask, s, NEG)
    m_prev = m_sc[...]
    m_new = jnp.maximum(m_prev, jnp.max(s, axis=1, keepdims=True))
    alpha = jnp.exp(m_prev - m_new)
    p = jnp.exp(s - m_new)
    if mask is not None:
        p = jnp.where(mask, p, 0.0)
    l_sc[...] = alpha * l_sc[...] + jnp.sum(p, axis=1, keepdims=True)
    acc_sc[...] = alpha * acc_sc[...] + _nt(p.astype(BF16), vt)
    m_sc[...] = m_new


def _dec_new_tokens(q, knew, vnew, biases, masks, m_sc, l_sc, acc_sc):
    n_new = knew.shape[0]
    scores = []
    for u in range(n_new):
        s = jnp.sum(q * knew[u:u + 1, :], axis=1, keepdims=True) * SCALE
        if biases is not None:
            s = s + biases[u]
        scores.append(jnp.where(masks[u], s, NEG))
    m_prev = m_sc[...]
    m_new = m_prev
    for s in scores:
        m_new = jnp.maximum(m_new, s)
    alpha = jnp.exp(m_prev - m_new)
    l = alpha * l_sc[...]
    acc = alpha * acc_sc[...]
    for u in range(n_new):
        p = jnp.where(masks[u], jnp.exp(scores[u] - m_new), 0.0)
        l = l + p
        acc = acc + p * vnew[u:u + 1, :]
    l_sc[...] = l
    acc_sc[...] = acc
    m_sc[...] = m_new


def _token_cumsum(x, n_tok, heads):
    parts, run = [], None
    for u in range(n_tok):
        seg = x[u * heads:(u + 1) * heads]
        run = seg if run is None else run + seg
        parts.append(run)
    return jnp.concatenate(parts, axis=0), parts


def _dec_window_kernel(*refs, n_tok, heads, n_kv, lb, n_back, dil, chunk, with_sink, with_lse):
    refs = list(refs)
    q_ref, cache_ref, new_ref = refs[:3]
    pos = 3
    sink_ref = None
    if with_sink:
        sink_ref = refs[pos]
        pos += 1
    o_ref = refs[pos]
    pos += 1
    lse_ref = None
    if with_lse:
        lse_ref = refs[pos]
        pos += 1
    m_sc, l_sc, acc_sc = refs[pos:pos + 3]
    r_all = n_tok * heads
    c = n_kv * HD
    group = heads // n_kv
    ridx = lax.broadcasted_iota(jnp.int32, (r_all, 1), 0)
    tok = ridx // heads
    _dec_init(m_sc, l_sc, acc_sc)
    q = q_ref[0]
    qb = q.astype(BF16)

    def ok(dist):
        good = (dist >= 0) & (dist <= n_back)
        if dil > 1:
            good = good & (dist % dil == 0)
        return good

    for c0 in range(0, lb, chunk):
        kt = cache_ref[0, 0:c, c0:c0 + chunk].astype(BF16)
        vt = cache_ref[0, c:2 * c, c0:c0 + chunk].astype(BF16)
        dist = (lb + tok) - (c0 + lax.broadcasted_iota(jnp.int32, (1, chunk), 1))
        _dec_update(qb, kt, vt, None, ok(dist), m_sc, l_sc, acc_sc)
    new = new_ref[0]
    _dec_new_tokens(q, new[:, 0:c], new[:, c:2 * c], None, [ok(tok - u) for u in range(n_tok)], m_sc, l_sc, acc_sc)
    m, l, acc = m_sc[...], l_sc[...], acc_sc[...]
    if with_sink:
        sk = sink_ref[...]
        m_f = jnp.maximum(m, sk)
        scale_old = jnp.exp(m - m_f)
        l = l * scale_old + jnp.exp(sk - m_f)
        acc = acc * scale_old
        m = m_f
    o_ref[0] = _diag_heads(acc / l, (ridx % heads) // group, n_kv)
    if with_lse:
        lse_ref[0] = jnp.broadcast_to(m + jnp.log(l), (r_all, HD))


def dec_window(qbd, cache_t, new, n_tok, heads, n_kv, n_back, dil, sink_rows=None, with_lse=False):
    b, r_all, c = qbd.shape
    lb = cache_t.shape[2]
    in_specs = [pl.BlockSpec((1, r_all, c), lambda i: (i, 0, 0)),
                pl.BlockSpec((1, 2 * c, lb), lambda i: (i, 0, 0)),
                pl.BlockSpec((1, n_tok, 2 * c), lambda i: (i, 0, 0))]
    args = [qbd, cache_t, new]
    if sink_rows is not None:
        in_specs.append(pl.BlockSpec((r_all, 1), lambda i: (0, 0)))
        args.append(sink_rows)
    o_spec = pl.BlockSpec((1, r_all, HD), lambda i: (i, 0, 0))
    o_shape = jax.ShapeDtypeStruct((b, r_all, HD), F32)
    return pl.pallas_call(
        functools.partial(_dec_window_kernel, n_tok=n_tok, heads=heads, n_kv=n_kv, lb=lb, n_back=n_back, dil=dil,
                          chunk=min(lb, 512), with_sink=sink_rows is not None, with_lse=with_lse),
        grid=(b,), in_specs=in_specs,
        out_specs=[o_spec, o_spec] if with_lse else o_spec,
        out_shape=[o_shape, o_shape] if with_lse else o_shape,
        scratch_shapes=[pltpu.VMEM((r_all, 1), F32), pltpu.VMEM((r_all, 1), F32), pltpu.VMEM((r_all, c), F32)],
        compiler_params=_params(("parallel",)), name="dec_window",
    )(*args)


def _dec_paged_kernel(*refs, n_pg, n_tok, heads, n_kv, mode):
    pt_ref, pool_ref = refs[:2]
    pos = 2
    lf_ref = None
    if mode == "fox":
        lf_ref = refs[pos]
        pos += 1
    q_ref, new_ref, x_ref = refs[pos:pos + 3]
    pos += 3
    if mode == "fox":
        tri_ref = refs[pos]
        pos += 1
    o_ref = refs[pos]
    pos += 1
    buf, sem, k_sc, v_sc, m_sc, l_sc, acc_sc = refs[pos:pos + 7]
    pos += 7
    if mode == "fox":
        lf_buf, lf_sem, c_sc, carry_sc = refs[pos:pos + 4]
    bi, gi = pl.program_id(0), pl.program_id(1)
    n_grp = pl.num_programs(1)
    step = bi * n_grp + gi
    slot = lax.rem(step, 2)

    def page_copies(seq_i, grp_i, into):
        copies = []
        for p in range(n_pg):
            page = pt_ref[seq_i, grp_i * n_pg + p]
            copies.append(pltpu.make_async_copy(pool_ref.at[page], buf.at[into, p], sem.at[into]))
            if mode == "fox":
                copies.append(pltpu.make_async_copy(lf_ref.at[page], lf_buf.at[into, p], lf_sem.at[into]))
        return copies

    @pl.when(step == 0)
    def _():
        for cp in page_copies(bi, gi, slot):
            cp.start()

    @pl.when(step + 1 < pl.num_programs(0) * n_grp)
    def _():
        nxt = step + 1
        for cp in page_copies(nxt // n_grp, lax.rem(nxt, n_grp), 1 - slot):
            cp.start()

    for cp in page_copies(bi, gi, slot):
        cp.wait()
    pages = [buf.at[slot, p] for p in range(n_pg)]
    lf_pages = [lf_buf.at[slot, p] for p in range(n_pg)] if mode == "fox" else None
    r_all = n_tok * heads
    c = n_kv * HD
    group = heads // n_kv
    tk = n_pg * PAGE
    ridx = lax.broadcasted_iota(jnp.int32, (r_all, 1), 0)
    tok = ridx // heads
    head = ridx % heads

    @pl.when(gi == 0)
    def _():
        _dec_init(m_sc, l_sc, acc_sc)
        if mode == "fox":
            carry_sc[...] = jnp.zeros(carry_sc.shape, F32)

    for i in range(n_pg):
        k_sc[:, i * PAGE:(i + 1) * PAGE] = pages[i][0:c, :].astype(BF16)
        v_sc[:, i * PAGE:(i + 1) * PAGE] = pages[i][c:2 * c, :].astype(BF16)
    q = q_ref[0]
    qb = q.astype(BF16)

    if mode == "fox":
        tri = tri_ref[...]
        hi, mid, lo = _split3(jnp.concatenate([lf_pages[i][...] for i in range(n_pg)], axis=0))
        within = _dot(hi, tri) + _dot(mid, tri) + _dot(lo, tri)
        carry = carry_sc[...]
        for i in range(n_pg):
            page = within[i * heads:(i + 1) * heads]
            c_sc[:, i * PAGE:(i + 1) * PAGE] = page + carry
            carry = carry + page[:, PAGE - 1:PAGE]
        carry_sc[...] = carry
        bias = _token_cumsum(x_ref[0], n_tok, heads)[0] - jnp.concatenate([c_sc[...]] * n_tok, axis=0)
        _dec_update(qb, k_sc[...], v_sc[...], bias, None, m_sc, l_sc, acc_sc)
    else:
        flags = x_ref[0].astype(BF16)
        width = flags.shape[1]
        key_blk = (gi * tk + lax.broadcasted_iota(jnp.int32, (width, tk), 1)) // SEL_BLOCK
        expand = jnp.where(lax.broadcasted_iota(jnp.int32, (width, tk), 0) == key_blk, 1.0, 0.0).astype(BF16)
        _dec_update(qb, k_sc[...], v_sc[...], None, _dot(flags, expand) > 0.5, m_sc, l_sc, acc_sc)

    @pl.when(gi == pl.num_programs(1) - 1)
    def _():
        new = new_ref[0]
        causal = [tok >= u for u in range(n_tok)]
        if mode == "fox":
            fq, per_tok = _token_cumsum(x_ref[0], n_tok, heads)
            total = jnp.concatenate([carry_sc[...]] * n_tok, axis=0)
            biases = [fq - jnp.concatenate([per_tok[u]] * n_tok, axis=0) - total for u in range(n_tok)]
            _dec_new_tokens(q, new[:, 0:c], new[:, c:2 * c], biases, causal, m_sc, l_sc, acc_sc)
        else:
            n_past_blk = (pl.num_programs(1) * tk) // SEL_BLOCK
            flags = x_ref[0]
            lane = lax.broadcasted_iota(jnp.int32, flags.shape, 1)
            new_on = jnp.sum(jnp.where(lane == n_past_blk, flags, 0.0), axis=1, keepdims=True) > 0.5
            _dec_new_tokens(q, new[:, 0:c], new[:, c:2 * c], None, [m & new_on for m in causal], m_sc, l_sc, acc_sc)
        o_ref[0] = _diag_heads(acc_sc[...] / l_sc[...], head // group, n_kv)


def dec_paged(qbd, pool_t, page_table, new, x, n_tok, heads, n_kv, mode, n_pg, lf_pool_t=None):
    b, r_all, c = qbd.shape
    n_pages = page_table.shape[1]
    in_specs = [pl.BlockSpec(memory_space=pl.ANY)]
    args = [pool_t]
    if mode == "fox":
        in_specs.append(pl.BlockSpec(memory_space=pl.ANY))
        args.append(lf_pool_t)
    in_specs += [pl.BlockSpec((1, r_all, c), lambda i, g, pt: (i, 0, 0)),
                 pl.BlockSpec((1, n_tok, 2 * c), lambda i, g, pt: (i, 0, 0)),
                 pl.BlockSpec((1, r_all, x.shape[2]), lambda i, g, pt: (i, 0, 0))]
    args += [qbd, new, x]
    scratch = [pltpu.VMEM((2, n_pg, 2 * c, PAGE), F32), pltpu.SemaphoreType.DMA((2,)),
               pltpu.VMEM((c, n_pg * PAGE), BF16), pltpu.VMEM((c, n_pg * PAGE), BF16),
               pltpu.VMEM((r_all, 1), F32), pltpu.VMEM((r_all, 1), F32), pltpu.VMEM((r_all, c), F32)]
    if mode == "fox":
        tri = jnp.asarray(np.triu(np.ones((PAGE, PAGE), np.float32)), dtype=BF16)
        in_specs.append(pl.BlockSpec((PAGE, PAGE), lambda i, g, pt: (0, 0)))
        args.append(tri)
        scratch += [pltpu.VMEM((2, n_pg, heads, PAGE), F32), pltpu.SemaphoreType.DMA((2,)),
                    pltpu.VMEM((heads, n_pg * PAGE), F32), pltpu.VMEM((heads, 1), F32)]
    return pl.pallas_call(
        functools.partial(_dec_paged_kernel, n_pg=n_pg, n_tok=n_tok, heads=heads, n_kv=n_kv, mode=mode),
        grid_spec=pltpu.PrefetchScalarGridSpec(
            num_scalar_prefetch=1, grid=(b, n_pages // n_pg), in_specs=in_specs,
            out_specs=pl.BlockSpec((1, r_all, HD), lambda i, g, pt: (i, 0, 0)), scratch_shapes=scratch),
        out_shape=jax.ShapeDtypeStruct((b, r_all, HD), F32),
        compiler_params=_params(("arbitrary", "arbitrary")), name="dec_paged_" + mode,
    )(page_table, *args)


def _block_diag_q(q, n_tok, heads, n_kv):
    b = q.shape[0] // n_tok
    own = jnp.asarray(np.arange(heads)[:, None] // (heads // n_kv) == np.arange(n_kv)[None, :], dtype=F32)
    qb = q.reshape(b, n_tok, heads, 1, HD) * own[None, None, :, :, None]
    return qb.reshape(b, n_tok * heads, n_kv * HD)


def _rows_first(z_t, n_kv):
    b, _, l = z_t.shape
    return z_t.reshape(b, 2, n_kv, HD, l).transpose(0, 4, 1, 2, 3)


def _rows_last(cache):
    n, rows = cache.shape[:2]
    return cache.transpose(0, 2, 3, 4, 1).reshape(n, -1, rows)


def _mixer_a(hp, hs, g, w_in, w_out, phi, caches, page_table, dims, ropes):
    bsz, seq, dec_b, dec_t, past_len = dims
    (cos_p, sin_p), (cos_s, sin_s) = ropes
    cache_cmp, cache_sel, cache_win = caches
    heads, n_kv = 16, 2
    segs = [(0, 1024, 0, 1024, True), (1024, 1280, 0, 0, False), (1280, 1536, 0, 128, False),
            (1536, 1792, 0, 128, False), (1792, 1840, 0, 0, False)]
    w1p, b1p, w2p = phi
    q_cmp, q_rot, cmp_raw, sel_kv, win_kv, gates, cmp_t, sel_t, win_t = norm_proj(
        hp, g, w_in, cos_p, sin_p, segs, 512, flipped=(1, 2, 3), seq=seq)
    n = bsz * seq
    kvc = compress(cmp_raw.reshape(bsz, seq, -1), None, w1p, b1p, w2p)
    o_cmp, flags = cmp_attend_prompt(q_cmp.reshape(bsz, seq, -1), kvc, 512, n_kv)
    q3 = q_rot.reshape(bsz, seq, -1)
    o_sel = flash_attention(q3, sel_kv.reshape(bsz, seq, -1), 256, 1024, 1, "sel", (flags,))
    o_win = band_attention(q3, win_kv.reshape(bsz, seq, -1), A_WINDOW - 1, 1, 256)
    hp = merge_a(o_cmp.reshape(n, -1), o_sel.reshape(n, -1), o_win.reshape(n, -1), gates, hp, w_out)
    kv5 = lambda z, b, t: z.reshape(b, t, 2, n_kv, HD)
    w = min(A_WINDOW, seq)
    outs_p = (_rows_first(cmp_t, n_kv), _rows_first(sel_t, n_kv), _rows_first(win_t[:, :, seq - w:], n_kv))
    q_cmp, q_rot, cmp_raw, sel_kv, win_kv, gates = norm_proj(hs, g, w_in, cos_s, sin_s, segs, 512)
    ns = dec_b * dec_t
    kvc = compress(_rows_last(cache_cmp), page_table, w1p, b1p, w2p)
    o_cmp, flags = cmp_attend_sample(_block_diag_q(q_cmp, dec_t, heads, n_kv), kvc, dec_t, heads, n_kv, past_len)
    qbd = _block_diag_q(q_rot, dec_t, heads, n_kv)
    flag_rows = jnp.repeat(flags.reshape(dec_b, dec_t, n_kv, 1, -1), heads // n_kv, axis=3).reshape(dec_b, dec_t * heads, -1)
    o_sel = dec_paged(qbd, _rows_last(cache_sel), page_table, sel_kv.reshape(dec_b, dec_t, -1), flag_rows,
                      dec_t, heads, n_kv, "sel", 32)
    o_win = dec_window(qbd, _rows_last(cache_win), win_kv.reshape(dec_b, dec_t, -1), dec_t, heads, n_kv, A_WINDOW - 1, 1)
    hs = merge_a(o_cmp.reshape(ns, -1), o_sel.reshape(ns, -1), o_win.reshape(ns, -1), gates, hs, w_out)
    outs_s = (kv5(cmp_raw, dec_b, dec_t), kv5(sel_kv, dec_b, dec_t), kv5(win_kv, dec_b, dec_t))
    return hp, hs, outs_p, outs_s


def _mixer_b(hp, hs, g, w_in, sinks, w_out, cache_win, dims, ropes):
    bsz, seq, dec_b, dec_t, _ = dims
    (cos_p, sin_p), (cos_s, sin_s) = ropes
    heads, n_kv = 16, 2
    segs = [(0, 1024, 0, 1024, False), (1024, 1280, 0, 128, False)]
    q, kv, kv_t = norm_proj(hp, g, w_in, cos_p, sin_p, segs, 512, flipped=(1,), seq=seq)
    o = band_attention(q.reshape(bsz, seq, -1), kv.reshape(bsz, seq, -1), B_WINDOW - 1, 1, 256, sink=sinks)
    hp = out_proj(o.reshape(bsz * seq, -1), hp, w_out)
    w = min(B_WINDOW, seq)
    out_p = _rows_first(kv_t[:, :, seq - w:], n_kv)
    q, kv = norm_proj(hs, g, w_in, cos_s, sin_s, segs, 512)
    sink_rows = jnp.tile(sinks.astype(F32), dec_t).reshape(dec_t * heads, 1)
    o = dec_window(_block_diag_q(q, dec_t, heads, n_kv), _rows_last(cache_win), kv.reshape(dec_b, dec_t, -1),
                   dec_t, heads, n_kv, B_WINDOW - 1, 1, sink_rows=sink_rows)
    hs = out_proj(o.reshape(dec_b * dec_t, -1), hs, w_out)
    return hp, hs, out_p, kv.reshape(dec_b, dec_t, 2, n_kv, HD)


def _mixer_c(hp, hs, g, w_in, b_f, w_out, cache_kv, cache_logf, page_table, dims, ropes):
    bsz, seq, dec_b, dec_t, _ = dims
    (cos_p, sin_p), (cos_s, sin_s) = ropes
    heads, n_kv = 16, 4
    segs = [(0, 1024, 0, 0, False), (1024, 1536, 0, 0, False), (1536, 1552, 0, 0, False)]
    q, kv, f, kv_t = norm_proj(hp, g, w_in, cos_p, sin_p, segs, 512, flipped=(1,), seq=seq)
    lf_t, cum_t = fox_gate(f.reshape(bsz, seq, heads).transpose(0, 2, 1), b_f)
    logf_p = lf_t.transpose(0, 2, 1)
    half = heads // 2
    cum_q = cum_t.reshape(bsz, 2, half, seq)
    cum_k = cum_q.transpose(0, 1, 3, 2)
    o = flash_attention(q.reshape(bsz, seq, -1), kv.reshape(bsz, seq, -1), 512, 512, 2, "fox", (cum_q, cum_k))
    hp = out_proj(o.reshape(bsz * seq, -1), hp, w_out)
    out_p = _rows_first(kv_t, n_kv)
    q, kv, f = norm_proj(hs, g, w_in, cos_s, sin_s, segs, 512)
    ns = dec_b * dec_t
    lf_t, _ = fox_gate(f.reshape(1, ns, heads).transpose(0, 2, 1), b_f)
    logf_s = lf_t.transpose(0, 2, 1).reshape(dec_b, dec_t, heads)
    o = dec_paged(_block_diag_q(q, dec_t, heads, n_kv), _rows_last(cache_kv), page_table, kv.reshape(dec_b, dec_t, -1),
                  logf_s.reshape(dec_b, dec_t * heads, 1), dec_t, heads, n_kv, "fox", 32,
                  lf_pool_t=cache_logf.transpose(0, 2, 1))
    hs = out_proj(o.reshape(ns, -1), hs, w_out)
    return hp, hs, (out_p, logf_p), (kv.reshape(dec_b, dec_t, 2, n_kv, HD), logf_s)


def _mixer_d(hp, hs, g, w_in, w_out, caches, dims, ropes):
    bsz, seq, dec_b, dec_t, _ = dims
    (cos_p, sin_p), (cos_s, sin_s) = ropes
    heads = 8
    c = heads * HD
    segs = []
    for grp in range(len(D_GROUPS)):
        base = 3 * grp * c
        segs += [(base, base + c, 0, c, False), (base + c, base + 3 * c, 0, c, False)]
    res = norm_proj(hp, g, w_in, cos_p, sin_p, segs, 256, flipped=(1, 3, 5), seq=seq)
    outs, lses, states_p = [], [], []
    for grp, (window, dil) in enumerate(D_GROUPS):
        q, kv = res[2 * grp], res[2 * grp + 1]
        if dil == 1:
            o, lse = band_attention(q.reshape(bsz, seq, c), kv.reshape(bsz, seq, 2 * c), window, 1, 256, with_lse=True)
        else:
            o, lse = dilated_attention(q.reshape(bsz, seq, c), kv.reshape(bsz, seq, 2 * c), window, dil)
        outs.append(o.reshape(bsz * seq, c))
        lses.append(lse.reshape(bsz * seq, c))
        states_p.append(_rows_first(res[6 + grp][:, :, seq - min(window, seq):], heads))
    hp = merge_d(outs, lses, hp, w_out)
    res = norm_proj(hs, g, w_in, cos_s, sin_s, segs, 256)
    outs, lses, states_s = [], [], []
    for grp, (window, dil) in enumerate(D_GROUPS):
        q, kv = res[2 * grp], res[2 * grp + 1]
        o, lse = dec_window(_block_diag_q(q, dec_t, heads, heads), _rows_last(caches[grp]),
                            kv.reshape(dec_b, dec_t, -1), dec_t, heads, heads, window, dil, with_lse=True)
        outs.append(o.reshape(dec_b * dec_t, c))
        lses.append(lse.reshape(dec_b * dec_t, c))
        states_s.append(kv.reshape(dec_b, dec_t, 2, heads, HD))
    hs = merge_d(outs, lses, hs, w_out)
    return hp, hs, states_p, states_s


def kernel(x_prompt, x_sample, cache_a_cmp, cache_a_sel, cache_a_win, cache_b_win, cache_c_kv, cache_c_logf, cache_d1, cache_d2, cache_d3, page_table, norm_mix, norm_ffn, norm_final, a_w_in, a_phi_w1, a_phi_b1, a_phi_w2, a_w_out, b_w_in, b_sinks, b_w_out, c_w_in, c_b_f, c_w_out, d_w_in, d_w_out, ffn_w_gate, ffn_w_up, ffn_w_down):
    bsz, seq, d_model = x_prompt.shape
    dec_b, dec_t, _ = x_sample.shape
    past_len = page_table.shape[1] * PAGE
    dims = (bsz, seq, dec_b, dec_t, past_len)
    ropes = (_rope_tables(jnp.arange(seq)), _rope_tables(past_len + jnp.arange(dec_b * dec_t) % dec_t))
    hp = x_prompt.reshape(bsz * seq, d_model)
    hs = x_sample.reshape(dec_b * dec_t, d_model)
    bf = lambda z: z.astype(BF16)
    phi = _compress_weights(a_phi_w1, a_phi_b1, a_phi_w2)
    depth = norm_mix.shape[0]
    for layer in range(depth):
        kind = layer % 4
        g = norm_mix[layer]
        if kind == 0:
            hp, hs, a_p, a_s = _mixer_a(hp, hs, g, bf(a_w_in), bf(a_w_out), phi,
                                        (cache_a_cmp, cache_a_sel, cache_a_win), page_table, dims, ropes)
        elif kind == 1:
            hp, hs, b_p, b_s = _mixer_b(hp, hs, g, bf(b_w_in), b_sinks, bf(b_w_out), cache_b_win, dims, ropes)
        elif kind == 2:
            hp, hs, c_p, c_s = _mixer_c(hp, hs, g, bf(c_w_in), c_b_f, bf(c_w_out), cache_c_kv, cache_c_logf,
                                        page_table, dims, ropes)
        else:
            hp, hs, d_p, d_s = _mixer_d(hp, hs, g, bf(d_w_in), bf(d_w_out), (cache_d1, cache_d2, cache_d3), dims, ropes)
        wg, wu, wd = bf(ffn_w_gate[layer]), bf(ffn_w_up[layer]), bf(ffn_w_down[layer])
        hp = ffn(hp, norm_ffn[layer], wg, wu, wd)
        hs = ffn(hs, norm_ffn[layer], wg, wu, wd)
    y_p = final_norm(hp, norm_final).reshape(bsz, seq, d_model)
    y_s = final_norm(hs, norm_final).reshape(dec_b, dec_t, d_model)
    return (y_p, y_s, a_p[0], a_s[0], a_p[1], a_s[1], a_p[2], a_s[2], b_p, b_s,
            c_p[0], c_s[0], c_p[1], c_s[1], d_p[0], d_s[0], d_p[1], d_s[1], d_p[2], d_s[2])
```

```python
import functools

import numpy as np
import jax
import jax.numpy as jnp
from jax import lax
from jax.experimental import pallas as pl
from jax.experimental.pallas import tpu as pltpu

F32 = jnp.float32
BF16 = jnp.bfloat16

HD = 64
SCALE = HD ** -0.5
LOG2E = 1.4426950408889634
LN2 = 0.6931471805599453
RMS_EPS = 1e-6
ROPE_THETA = 10000.0
PAGE = 128
CMP_STRIDE = 16
CMP_BLOCK = 32
SEL_BLOCK = 64
N_SEL = 16
A_WINDOW = 512
B_WINDOW = 128
D_GROUPS = ((128, 1), (512, 4), (2048, 16))
FORCE_SCORE = 1e9
NEG = -1e30
DEAD = -3e38
LANE = 128
SUBLANE = 8
ONES_ROWS = 16
VMEM_LIMIT = 52 * 1024 * 1024


def _params(sem, vmem=VMEM_LIMIT):
    return pltpu.CompilerParams(dimension_semantics=sem, vmem_limit_bytes=vmem)


def _nt(a, b):
    return lax.dot_general(a, b, (((1,), (1,)), ((), ())), preferred_element_type=F32)


def _dot(a, b):
    return jnp.dot(a, b, preferred_element_type=F32)


def _split3(x):
    hi = x.astype(BF16)
    r1 = x - hi.astype(F32)
    mid = r1.astype(BF16)
    lo = (r1 - mid.astype(F32)).astype(BF16)
    return hi, mid, lo


def _rms(x, g):
    var = jnp.mean(x * x, axis=-1, keepdims=True)
    return x * lax.rsqrt(var + RMS_EPS) * g


def _rope_chunk(x, cos2, sin2):
    lane = lax.broadcasted_iota(jnp.int32, x.shape, 1)
    first = (lane % HD) < (HD // 2)
    swapped = jnp.where(first, pltpu.roll(x, LANE - HD // 2, axis=1), pltpu.roll(x, HD // 2, axis=1))
    return x * cos2 + swapped * sin2


def _rope_tables(pos):
    half = HD // 2
    inv = ROPE_THETA ** (-jnp.arange(half, dtype=F32) / half)
    ang = pos.astype(F32)[:, None] * inv[None, :]
    c, s = jnp.cos(ang), jnp.sin(ang)
    return jnp.tile(c, (1, 4)), jnp.tile(jnp.concatenate([-s, s], axis=1), (1, 2))


def _norm_proj_kernel(x_ref, g_ref, w_ref, cos_ref, sin_ref, *out_refs, segs, flipped):
    xn = _rms(x_ref[...], g_ref[...]).astype(BF16)
    outs = list(out_refs)
    written = []
    for c0, c1, r0, r1, dup in segs:
        acc = _dot(xn, w_ref[:, c0:c1])
        if dup:
            outs.pop(0)[...] = acc
        o_ref = outs.pop(0)
        written.append(o_ref)
        if r1 <= r0:
            o_ref[...] = acc
            continue
        if r0 > 0:
            o_ref[:, 0:r0] = acc[:, 0:r0]
        for a in range(r0, r1, LANE):
            o_ref[:, a:a + LANE] = _rope_chunk(acc[:, a:a + LANE], cos_ref[...], sin_ref[...])
        if r1 < c1 - c0:
            o_ref[:, r1:] = acc[:, r1:]
    for idx in flipped:
        outs.pop(0)[0] = written[idx][...].T


def norm_proj(x, g, w, cos2, sin2, segs, tm, flipped=(), seq=None):
    n, d = x.shape
    tm = min(tm, n)
    out_shape, out_specs = [], []
    for c0, c1, _, _, dup in segs:
        for _ in range(2 if dup else 1):
            out_shape.append(jax.ShapeDtypeStruct((n, c1 - c0), F32))
            out_specs.append(pl.BlockSpec((tm, c1 - c0), lambda i: (i, 0)))
    for idx in flipped:
        width = segs[idx][1] - segs[idx][0]
        per_seq = seq // tm
        out_shape.append(jax.ShapeDtypeStruct((n // seq, width, seq), F32))
        out_specs.append(pl.BlockSpec((1, width, tm), lambda i: (i // per_seq, 0, i % per_seq)))
    return pl.pallas_call(
        functools.partial(_norm_proj_kernel, segs=tuple(segs), flipped=tuple(flipped)),
        grid=(n // tm,),
        in_specs=[pl.BlockSpec((tm, d), lambda i: (i, 0)),
                  pl.BlockSpec((1, d), lambda i: (0, 0)),
                  pl.BlockSpec(w.shape, lambda i: (0, 0)),
                  pl.BlockSpec((tm, LANE), lambda i: (i % (cos2.shape[0] // tm), 0)),
                  pl.BlockSpec((tm, LANE), lambda i: (i % (sin2.shape[0] // tm), 0))],
        out_specs=out_specs, out_shape=out_shape,
        compiler_params=_params(("parallel",)), name="norm_proj",
    )(x, g.reshape(1, d), w, cos2, sin2)


def _out_proj_kernel(o_ref, h_ref, w_ref, out_ref):
    out_ref[...] = h_ref[...] + _dot(o_ref[...].astype(BF16), w_ref[...])


def out_proj(o, h, w, tm=512):
    n, k = o.shape
    d = w.shape[1]
    tm = min(tm, n)
    return pl.pallas_call(
        _out_proj_kernel, grid=(n // tm,),
        in_specs=[pl.BlockSpec((tm, k), lambda i: (i, 0)), pl.BlockSpec((tm, d), lambda i: (i, 0)),
                  pl.BlockSpec(w.shape, lambda i: (0, 0))],
        out_specs=pl.BlockSpec((tm, d), lambda i: (i, 0)),
        out_shape=jax.ShapeDtypeStruct((n, d), F32),
        compiler_params=_params(("parallel",)), name="out_proj",
    )(o, h, w)


def _merge_a_kernel(oc_ref, os_ref, ow_ref, g_ref, h_ref, w_ref, out_ref, o_sc, *, heads):
    gate = jax.nn.sigmoid(g_ref[...])
    for h in range(heads):
        sl = slice(h * HD, (h + 1) * HD)
        o = (gate[:, 3 * h:3 * h + 1] * oc_ref[:, sl] + gate[:, 3 * h + 1:3 * h + 2] * os_ref[:, sl]
             + gate[:, 3 * h + 2:3 * h + 3] * ow_ref[:, sl])
        o_sc[:, sl] = o
    out_ref[...] = h_ref[...] + _dot(o_sc[...].astype(BF16), w_ref[...])


def merge_a(o_cmp, o_sel, o_win, gates, h, w, tm=512):
    n, k = o_cmp.shape
    d = w.shape[1]
    tm = min(tm, n)
    row = lambda c: pl.BlockSpec((tm, c), lambda i: (i, 0))
    return pl.pallas_call(
        functools.partial(_merge_a_kernel, heads=k // HD), grid=(n // tm,),
        in_specs=[row(k), row(k), row(k), row(gates.shape[1]), row(d), pl.BlockSpec(w.shape, lambda i: (0, 0))],
        out_specs=row(d), out_shape=jax.ShapeDtypeStruct((n, d), F32),
        scratch_shapes=[pltpu.VMEM((tm, k), F32)],
        compiler_params=_params(("parallel",)), name="merge_a",
    )(o_cmp, o_sel, o_win, gates, h, w)


def _merge_d_kernel(o1, o2, o3, l1, l2, l3, h_ref, w_ref, out_ref):
    a, b, c = l1[...], l2[...], l3[...]
    m = jnp.maximum(jnp.maximum(a, b), c)
    ea, eb, ec = jnp.exp(a - m), jnp.exp(b - m), jnp.exp(c - m)
    den = ea + eb + ec
    o = (ea / den) * o1[...] + (eb / den) * o2[...] + (ec / den) * o3[...]
    out_ref[...] = h_ref[...] + _dot(o.astype(BF16), w_ref[...])


def merge_d(outs, lses, h, w, tm=512):
    n, k = outs[0].shape
    d = w.shape[1]
    tm = min(tm, n)
    row = lambda c: pl.BlockSpec((tm, c), lambda i: (i, 0))
    return pl.pallas_call(
        _merge_d_kernel, grid=(n // tm,),
        in_specs=[row(k)] * 6 + [row(d), pl.BlockSpec(w.shape, lambda i: (0, 0))],
        out_specs=row(d), out_shape=jax.ShapeDtypeStruct((n, d), F32),
        compiler_params=_params(("parallel",)), name="merge_d",
    )(*outs, *lses, h, w)


def _ffn_kernel(h_ref, g_ref, wg_ref, wu_ref, wd_ref, out_ref, xn_sc, acc_sc):
    j = pl.program_id(1)

    @pl.when(j == 0)
    def _():
        x = h_ref[...]
        xn_sc[...] = _rms(x, g_ref[...]).astype(BF16)
        acc_sc[...] = x

    xn = xn_sc[...]
    a = _dot(xn, wg_ref[...])
    u = _dot(xn, wu_ref[...])
    mid = (a * jax.nn.sigmoid(a) * u).astype(BF16)
    acc_sc[...] += _dot(mid, wd_ref[...])

    @pl.when(j == pl.num_programs(1) - 1)
    def _():
        out_ref[...] = acc_sc[...]


def ffn(h, g, wg, wu, wd, tm=512, n_f=2):
    n, d = h.shape
    f = wg.shape[1]
    tm = min(tm, n)
    tf = f // n_f
    return pl.pallas_call(
        _ffn_kernel, grid=(n // tm, n_f),
        in_specs=[pl.BlockSpec((tm, d), lambda i, j: (i, 0)), pl.BlockSpec((1, d), lambda i, j: (0, 0)),
                  pl.BlockSpec((d, tf), lambda i, j: (0, j)), pl.BlockSpec((d, tf), lambda i, j: (0, j)),
                  pl.BlockSpec((tf, d), lambda i, j: (j, 0))],
        out_specs=pl.BlockSpec((tm, d), lambda i, j: (i, 0)),
        out_shape=jax.ShapeDtypeStruct((n, d), F32),
        scratch_shapes=[pltpu.VMEM((tm, d), BF16), pltpu.VMEM((tm, d), F32)],
        compiler_params=_params(("parallel", "arbitrary")), name="ffn",
    )(h, g.reshape(1, d), wg, wu, wd)


def _final_norm_kernel(x_ref, g_ref, o_ref):
    o_ref[...] = _rms(x_ref[...], g_ref[...])


def final_norm(x, g, tm=1024):
    n, d = x.shape
    tm = min(tm, n)
    return pl.pallas_call(
        _final_norm_kernel, grid=(n // tm,),
        in_specs=[pl.BlockSpec((tm, d), lambda i: (i, 0)), pl.BlockSpec((1, d), lambda i: (0, 0))],
        out_specs=pl.BlockSpec((tm, d), lambda i: (i, 0)),
        out_shape=jax.ShapeDtypeStruct((n, d), F32),
        compiler_params=_params(("parallel",)), name="final_norm",
    )(x, g.reshape(1, d))


def _fox_gate_kernel(f_ref, b_ref, lf_ref, cum_ref):
    x = f_ref[0] + b_ref[...]
    lf = jnp.minimum(x, 0.0) - jnp.log1p(jnp.exp(-jnp.abs(x)))
    lf_ref[0] = lf
    n = x.shape[1]
    lane = lax.broadcasted_iota(jnp.int32, x.shape, 1)
    c = lf
    s = 1
    while s < n:
        c = c + jnp.where(lane >= s, pltpu.roll(c, s, axis=1), 0.0)
        s *= 2
    cum_ref[0] = c


def fox_gate(f_t, b_f):
    b, hh, n = f_t.shape
    spec = pl.BlockSpec((1, hh, n), lambda i: (i, 0, 0))
    return pl.pallas_call(
        _fox_gate_kernel, grid=(b,),
        in_specs=[spec, pl.BlockSpec((hh, 1), lambda i: (0, 0))],
        out_specs=[spec, spec], out_shape=[jax.ShapeDtypeStruct(f_t.shape, F32)] * 2,
        compiler_params=_params(("parallel",)), name="fox_gate",
    )(f_t, b_f.reshape(hh, 1))


def _attn_init(m_sc, acc_sc):
    m_sc[...] = jnp.full(m_sc.shape, NEG, F32)
    acc_sc[...] = jnp.zeros(acc_sc.shape, F32)


def _value_rows(v_ref):
    vt = v_ref[0].T.astype(BF16)
    ones = jnp.ones((ONES_ROWS, vt.shape[1]), BF16)
    return [jnp.concatenate([vt[kv * HD:(kv + 1) * HD], ones], axis=0) for kv in range(vt.shape[0] // HD)]


def _attn_steps(slots, work, tq, m_sc, acc_sc):
    state = [(m_sc[h], acc_sc[h]) for h in slots]
    new = [_attn_step(m, acc, *args, tq) for (m, acc), args in zip(state, work)]
    for h, (m, acc) in zip(slots, new):
        m_sc[h] = m
        acc_sc[h] = acc


def _attn_step(m_prev, acc_prev, k, vt, q, addends, tq):
    s = _nt(k, q)
    tk = s.shape[0]
    per_query = None
    if addends is not None and any(t.shape[0] == 1 for terms in addends for t in terms):
        per_query = [[t for t in terms if t.shape[0] == 1] for terms in addends]
        per_query = [sum(ts[1:], ts[0]) if ts else jnp.zeros((1, tq), F32) for ts in per_query]
        per_query = per_query[0] if len(per_query) == 1 else jnp.concatenate(per_query, axis=1)

    def rows(r0, r1):
        blk = s[r0:r1]
        if addends is None:
            return blk
        pieces = []
        for g, terms in enumerate(addends):
            piece = blk[:, g * tq:(g + 1) * tq]
            for term in terms:
                if term.shape[0] != 1:
                    piece = piece + term[r0:r1]
            pieces.append(piece)
        return pieces[0] if len(pieces) == 1 else jnp.concatenate(pieces, axis=1)

    mx = rows(0, SUBLANE)
    for r0 in range(SUBLANE, tk, SUBLANE):
        mx = jnp.maximum(mx, rows(r0, r0 + SUBLANE))
    mx = jnp.max(mx, axis=0, keepdims=True)
    m_new = jnp.maximum(m_prev, mx if per_query is None else mx + per_query)
    shift = m_new if per_query is None else m_new - per_query
    strip = 2 * SUBLANE
    p = jnp.concatenate([jnp.exp2(rows(r0, r0 + strip) - shift).astype(BF16) for r0 in range(0, tk, strip)], axis=0)
    return m_new, jnp.exp2(m_prev - m_new) * acc_prev + _dot(vt, p)


def _attn_finish(hq, group, tq, m_sc, acc_sc, ot_sc, o_ref, sink_ref=None, lset_sc=None, lse_ref=None):
    for h in range(hq):
        kv, g = divmod(h, group)
        cols = slice(g * tq, (g + 1) * tq)
        num, l, m = acc_sc[kv, 0:HD, cols], acc_sc[kv, HD:HD + 1, cols], m_sc[kv, :, cols]
        if sink_ref is not None:
            sk = sink_ref[:, h:h + 1] * LOG2E
            m_f = jnp.maximum(m, sk)
            w_old = jnp.exp2(m - m_f)
            l = l * w_old + jnp.exp2(sk - m_f)
            num = num * w_old
            m = m_f
        ot_sc[h * HD:(h + 1) * HD, :] = num / l
        if lset_sc is not None:
            lset_sc[h * HD:(h + 1) * HD, :] = jnp.broadcast_to(m * LN2 + jnp.log(l), num.shape)
    o_ref[0] = ot_sc[...].T
    if lset_sc is not None:
        lse_ref[0] = lset_sc[...].T


def _q_group(q_ref, kv, group):
    heads = [(q_ref[0, :, h * HD:(h + 1) * HD] * (SCALE * LOG2E)).astype(BF16)
             for h in range(kv * group, (kv + 1) * group)]
    return heads[0] if group == 1 else jnp.concatenate(heads, axis=0)


def _band_kernel(*refs, t, nb, n_back, dil, hq, group, with_sink, with_lse):
    refs = list(refs)
    q_ref, k_ref, v_ref = refs[:3]
    pos = 3
    sink_ref = None
    if with_sink:
        sink_ref = refs[pos]
        pos += 1
    o_ref = refs[pos]
    pos += 1
    lse_ref = None
    if with_lse:
        lse_ref = refs[pos]
        pos += 1
    m_sc, acc_sc, ot_sc = refs[pos:pos + 3]
    lset_sc = refs[pos + 3] if with_lse else None
    qi, j = pl.program_id(1), pl.program_id(2)
    kt = qi - nb + j

    @pl.when(j == 0)
    def _():
        _attn_init(m_sc, acc_sc)

    @pl.when(kt >= 0)
    def _():
        dist = (qi * t + lax.broadcasted_iota(jnp.int32, (t, t), 1)) - (kt * t + lax.broadcasted_iota(jnp.int32, (t, t), 0))
        band = jnp.where(dist >= 0, dist, n_back + 1)
        if dil > 1:
            band = jnp.where((dist & (dil - 1)) == 0, band, n_back + 1)
        addends = [[jnp.where(band <= n_back, 0.0, NEG)]] * group
        vts = _value_rows(v_ref)
        work = [(k_ref[0, :, kv * HD:(kv + 1) * HD].astype(BF16), vts[kv], _q_group(q_ref, kv, group), addends)
                for kv in range(hq // group)]
        _attn_steps(range(hq // group), work, t, m_sc, acc_sc)

    @pl.when(j == nb)
    def _():
        _attn_finish(hq, group, t, m_sc, acc_sc, ot_sc, o_ref, sink_ref, lset_sc, lse_ref)


def band_attention(q, kv, n_back, dil, t, sink=None, with_lse=False):
    b, l, cq = q.shape
    ck = kv.shape[2] // 2
    nb = -(-n_back // t)
    hq, hkv = cq // HD, ck // HD
    kmap = lambda blk: (lambda bi, qi, j: (bi, jnp.maximum(qi - nb + j, 0), blk))
    in_specs = [pl.BlockSpec((1, t, cq), lambda bi, qi, j: (bi, qi, 0)),
                pl.BlockSpec((1, t, ck), kmap(0)), pl.BlockSpec((1, t, ck), kmap(1))]
    args = [q, kv, kv]
    if sink is not None:
        in_specs.append(pl.BlockSpec((1, hq), lambda bi, qi, j: (0, 0)))
        args.append(sink.reshape(1, hq))
    o_spec = pl.BlockSpec((1, t, cq), lambda bi, qi, j: (bi, qi, 0))
    o_shape = jax.ShapeDtypeStruct(q.shape, F32)
    wide = hq // hkv * t
    scratch = [pltpu.VMEM((hkv, 1, wide), F32), pltpu.VMEM((hkv, HD + ONES_ROWS, wide), F32), pltpu.VMEM((cq, t), F32)]
    if with_lse:
        scratch.append(pltpu.VMEM((cq, t), F32))
    return pl.pallas_call(
        functools.partial(_band_kernel, t=t, nb=nb, n_back=n_back, dil=dil, hq=hq, group=hq // hkv,
                          with_sink=sink is not None, with_lse=with_lse),
        grid=(b, l // t, nb + 1),
        in_specs=in_specs,
        out_specs=[o_spec, o_spec] if with_lse else o_spec,
        out_shape=[o_shape, o_shape] if with_lse else o_shape,
        scratch_shapes=scratch,
        compiler_params=_params(("parallel", "parallel", "arbitrary")), name="band_attention",
    )(*args)


def _dilated_kernel(q_ref, k_ref, v_ref, o_ref, lse_ref, m_sc, acc_sc, *, dil, sub, n_back):
    qi, j = pl.program_id(2), pl.program_id(3)
    kt = qi - 1 + j

    @pl.when(j == 0)
    def _():
        _attn_init(m_sc, acc_sc)

    @pl.when(kt >= 0)
    def _():
        dist = (qi * sub + lax.broadcasted_iota(jnp.int32, (sub, sub), 1)) - (kt * sub + lax.broadcasted_iota(jnp.int32, (sub, sub), 0))
        addends = [[jnp.where(jnp.where(dist >= 0, dist, n_back + 1) <= n_back, 0.0, NEG)]]
        ones = jnp.ones((ONES_ROWS, sub), BF16)
        work = []
        for r in range(dil):
            rows = pl.ds(r, sub, stride=dil)
            q = (q_ref[0, rows, :] * (SCALE * LOG2E)).astype(BF16)
            k = k_ref[0, rows, :].astype(BF16)
            vt = v_ref[0, rows, :].T.astype(BF16)
            for h in range(LANE // HD):
                cols = slice(h * HD, (h + 1) * HD)
                work.append((k[:, cols], jnp.concatenate([vt[cols], ones], axis=0), q[:, cols], addends))
        _attn_steps(range(dil * (LANE // HD)), work, sub, m_sc, acc_sc)

    @pl.when(j == 1)
    def _():
        for r in range(dil):
            outs, lses = [], []
            for h in range(LANE // HD):
                slot = r * (LANE // HD) + h
                num, l = acc_sc[slot, 0:HD], acc_sc[slot, HD:HD + 1]
                outs.append(num / l)
                lses.append(jnp.broadcast_to(m_sc[slot] * LN2 + jnp.log(l), num.shape))
            rows = pl.ds(r, sub, stride=dil)
            o_ref[0, rows, :] = jnp.concatenate(outs, axis=0).T
            lse_ref[0, rows, :] = jnp.concatenate(lses, axis=0).T


def dilated_attention(q, kv, window, dil):
    b, l, cq = q.shape
    sub = window // dil
    n_pair = cq // LANE
    kmap = lambda off: (lambda bi, c, qi, j: (bi, jnp.maximum(qi - 1 + j, 0), off + c))
    spec = pl.BlockSpec((1, window, LANE), lambda bi, c, qi, j: (bi, qi, c))
    slots = dil * (LANE // HD)
    return pl.pallas_call(
        functools.partial(_dilated_kernel, dil=dil, sub=sub, n_back=sub),
        grid=(b, n_pair, l // window, 2),
        in_specs=[spec, pl.BlockSpec((1, window, LANE), kmap(0)), pl.BlockSpec((1, window, LANE), kmap(n_pair))],
        out_specs=[spec, spec], out_shape=[jax.ShapeDtypeStruct(q.shape, F32)] * 2,
        scratch_shapes=[pltpu.VMEM((slots, 1, sub), F32), pltpu.VMEM((slots, HD + ONES_ROWS, sub), F32)],
        compiler_params=_params(("parallel", "parallel", "parallel", "arbitrary")), name="dilated_attention",
    )(q, kv, kv)


def _flash_kernel(qt_ref, kt_ref, *refs, t, hq, group, mode):
    if mode == "fox":
        q_ref, k_ref, v_ref, cq_ref, ck_ref, o_ref, m_sc, acc_sc, ot_sc = refs
    else:
        q_ref, k_ref, v_ref, sel_ref, o_ref, m_sc, acc_sc, ot_sc = refs
    step = pl.program_id(2)
    qi, kj = qt_ref[step], kt_ref[step]

    @pl.when(kj == 0)
    def _():
        _attn_init(m_sc, acc_sc)

    def body(diagonal):
        vts = _value_rows(v_ref)
        causal = lax.broadcasted_iota(jnp.int32, (t, t), 1) >= lax.broadcasted_iota(jnp.int32, (t, t), 0)
        if mode == "sel":
            selb = sel_ref[0].astype(BF16)
            width = selb.shape[1]
            n_blk = width // (hq // group)
            key_blk = (kj * t + lax.broadcasted_iota(jnp.int32, (t, width), 0)) // SEL_BLOCK
            col = lax.broadcasted_iota(jnp.int32, (t, width), 1)
        work = []
        for kv in range(hq // group):
            k = k_ref[0, :, kv * HD:(kv + 1) * HD].astype(BF16)
            if mode == "fox":
                tail = [jnp.where(causal, 0.0, NEG)] if diagonal else []
                addends = [[cq_ref[0, 0, h:h + 1, :] * LOG2E, ck_ref[0, 0, :, h:h + 1] * (-LOG2E)] + tail
                           for h in range(kv * group, (kv + 1) * group)]
            else:
                expand = jnp.where(col == kv * n_blk + key_blk, 1.0, 0.0).astype(BF16)
                flag = _nt(expand, selb)
                if diagonal:
                    flag = jnp.where(causal, flag, 0.0)
                addends = [[jnp.where(flag > 0.5, 0.0, NEG)]] * group
            work.append((k, vts[kv], _q_group(q_ref, kv, group), addends))
        _attn_steps(range(hq // group), work, t, m_sc, acc_sc)

    @pl.when(kj < qi)
    def _():
        body(False)

    @pl.when(kj == qi)
    def _():
        body(True)
        _attn_finish(hq, group, t, m_sc, acc_sc, ot_sc, o_ref)


def flash_attention(q, kv, t, cq, n_col, mode, extra):
    b, l, _ = q.shape
    hq = cq // HD
    nq = l // t
    pairs = [(qi, kj) for qi in range(nq) for kj in range(qi + 1)]
    qt = jnp.asarray([p[0] for p in pairs], jnp.int32)
    kt = jnp.asarray([p[1] for p in pairs], jnp.int32)
    kmap = lambda off: (lambda bi, c, s, qt, kt: (bi, kt[s], off + c))
    in_specs = [pl.BlockSpec((1, t, cq), lambda bi, c, s, qt, kt: (bi, qt[s], c)),
                pl.BlockSpec((1, t, LANE), kmap(0)), pl.BlockSpec((1, t, LANE), kmap(n_col))]
    if mode == "fox":
        in_specs += [pl.BlockSpec((1, 1, hq, t), lambda bi, c, s, qt, kt: (bi, c, 0, qt[s])),
                     pl.BlockSpec((1, 1, t, hq), lambda bi, c, s, qt, kt: (bi, c, kt[s], 0))]
    else:
        in_specs += [pl.BlockSpec((1, t, extra[0].shape[2]), lambda bi, c, s, qt, kt: (bi, qt[s], 0))]
    return pl.pallas_call(
        functools.partial(_flash_kernel, t=t, hq=hq, group=hq // 2, mode=mode),
        grid_spec=pltpu.PrefetchScalarGridSpec(
            num_scalar_prefetch=2, grid=(b, n_col, len(pairs)), in_specs=in_specs,
            out_specs=pl.BlockSpec((1, t, cq), lambda bi, c, s, qt, kt: (bi, qt[s], c)),
            scratch_shapes=[pltpu.VMEM((2, 1, hq // 2 * t), F32), pltpu.VMEM((2, HD + ONES_ROWS, hq // 2 * t), F32),
                            pltpu.VMEM((cq, t), F32)]),
        out_shape=jax.ShapeDtypeStruct(q.shape, F32),
        compiler_params=_params(("parallel", "parallel", "arbitrary")), name="flash_" + mode,
    )(qt, kt, q, kv, kv, *extra)


def _compress_regroup(x_sc, j, rows_at, n_rows, r0, total):
    for s in range(0, CMP_STRIDE, 2):
        a, b = rows_at(s), rows_at(s + 1)
        low = lax.broadcasted_iota(jnp.int32, a.shape, 1) < HD
        cols = slice(s * HD, (s + 2) * HD)
        x_sc[j, pl.ds(r0, n_rows), cols] = jnp.where(low, a, pltpu.roll(b, HD, axis=1)).astype(BF16)
        x_sc[j, pl.ds(total + r0, n_rows), cols] = jnp.where(low, pltpu.roll(a, HD, axis=1), b).astype(BF16)


def _compress_mlp(x_sc, w1_ref, b1_ref, w2_ref, out_ref, total):
    for j in range(2):
        parts = _dot(x_sc[j], w1_ref[j])
        half = parts.shape[1] // 2
        pre = b1_ref[j] + parts[:, :half] + pltpu.roll(parts[:, half:], 2 * total - 1, axis=0)
        act = pre * jax.nn.sigmoid(pre)
        res = _dot(act.astype(BF16), w2_ref[j])
        for kv in range(2):
            out_ref[0, :, j * LANE + kv * HD:j * LANE + (kv + 1) * HD] = res[kv * total:(kv + 1) * total]


def _compress_rows_kernel(k_ref, v_ref, w1_ref, b1_ref, w2_ref, out_ref, x_sc, *, chunks):
    for j, src in enumerate((k_ref, v_ref)):
        _compress_regroup(x_sc, j, lambda s, src=src: src[0, pl.ds(s, chunks, stride=CMP_STRIDE), :], chunks, 0, chunks)
    _compress_mlp(x_sc, w1_ref, b1_ref, w2_ref, out_ref, chunks)


def _compress_paged_kernel(pt_ref, pool_ref, perm_ref, w1_ref, b1_ref, w2_ref, out_ref, x_sc, buf, sem, *, n_src, chunks):
    bi = pl.program_id(0)
    slot = lax.rem(bi, 2)
    total = n_src * chunks
    fold = 2 * SUBLANE // chunks

    def page_copies(seq_i, into):
        return [pltpu.make_async_copy(pool_ref.at[pt_ref[seq_i, p]], buf.at[into, p], sem.at[into]) for p in range(n_src)]

    @pl.when(bi == 0)
    def _():
        for cp in page_copies(bi, slot):
            cp.start()

    @pl.when(bi + 1 < pl.num_programs(0))
    def _():
        for cp in page_copies(bi + 1, 1 - slot):
            cp.start()

    for cp in page_copies(bi, slot):
        cp.wait()

    def group(i2, carry):
        perm = perm_ref[...]
        for j in range(2):
            sorted_tokens = [_nt(perm, buf[slot, i2 * fold + f, j * LANE:(j + 1) * LANE, :].astype(BF16))
                             for f in range(fold)]
            rows_at = lambda s, toks=sorted_tokens: jnp.concatenate(
                [tok[s * chunks:(s + 1) * chunks] for tok in toks], axis=0)
            _compress_regroup(x_sc, j, rows_at, fold * chunks, pl.multiple_of(i2 * (fold * chunks), fold * chunks), total)
        return carry

    lax.fori_loop(0, n_src // fold, group, 0, unroll=8)
    _compress_mlp(x_sc, w1_ref, b1_ref, w2_ref, out_ref, total)


def compress(x, page_table, w1p, b1p, w2p):
    const = lambda nd: (lambda *_: (0,) * nd)
    w_specs = [pl.BlockSpec(w1p.shape, const(3)), pl.BlockSpec(b1p.shape, const(3)), pl.BlockSpec(w2p.shape, const(3))]
    if page_table is None:
        b, l, width = x.shape
        n_src, chunks = 1, l // CMP_STRIDE
        scratch = [pltpu.VMEM((2, 2 * chunks, CMP_STRIDE * HD), BF16)]
        halves = [pl.BlockSpec((1, l, LANE), lambda i: (i, 0, 0)), pl.BlockSpec((1, l, LANE), lambda i: (i, 0, 1))]
        grid_spec = pl.GridSpec(
            grid=(b,), in_specs=halves + w_specs,
            out_specs=pl.BlockSpec((1, chunks, width), lambda i: (i, 0, 0)), scratch_shapes=scratch)
        args = (x, x, w1p, b1p, w2p)
        body = functools.partial(_compress_rows_kernel, chunks=chunks)
    else:
        b, n_src = page_table.shape
        _, width, rows = x.shape
        chunks = rows // CMP_STRIDE
        fold = 2 * SUBLANE // chunks
        scratch = [pltpu.VMEM((2, 2 * n_src * chunks, CMP_STRIDE * HD), BF16),
                   pltpu.VMEM((2, n_src, width, rows), F32), pltpu.SemaphoreType.DMA((2,))]
        tok = np.arange(rows)
        perm = jnp.asarray(tok[None, :] == ((tok % chunks) * CMP_STRIDE + tok // chunks)[:, None], dtype=BF16)
        grid_spec = pltpu.PrefetchScalarGridSpec(
            num_scalar_prefetch=1, grid=(b,),
            in_specs=[pl.BlockSpec(memory_space=pl.ANY), pl.BlockSpec((rows, rows), lambda i, pt: (0, 0))] + w_specs,
            out_specs=pl.BlockSpec((1, n_src * chunks, width), lambda i, pt: (i, 0, 0)), scratch_shapes=scratch)
        args = (page_table, x, perm, w1p, b1p, w2p)
        body = functools.partial(_compress_paged_kernel, n_src=n_src, chunks=chunks)
    return pl.pallas_call(
        body,
        grid_spec=grid_spec,
        out_shape=jax.ShapeDtypeStruct((b, n_src * chunks, width), F32),
        compiler_params=_params(("arbitrary",)), name="compress",
    )(*args)


def _compress_weights(phi_w1, phi_b1, phi_w2):
    hidden = phi_w1.shape[-1]
    parts = CMP_BLOCK // CMP_STRIDE
    w1p = phi_w1.reshape(2, parts, CMP_STRIDE * HD, hidden).transpose(0, 2, 1, 3).reshape(2, CMP_STRIDE * HD, parts * hidden)
    return w1p.astype(BF16), phi_b1[:, None, :], phi_w2.astype(BF16)


def _top_blocks(imp, n_take):
    lane = lax.broadcasted_iota(jnp.int32, imp.shape, 1).astype(F32)
    sel = jnp.zeros(imp.shape, F32)
    cur = imp
    for _ in range(n_take):
        mx = jnp.max(cur, axis=1, keepdims=True)
        first = jnp.min(jnp.where(cur == mx, lane, 1e9), axis=1, keepdims=True)
        pick = lane == first
        sel = jnp.where(pick, 1.0, sel)
        cur = jnp.where(pick, DEAD, cur)
    return sel


def _cmp_prompt_kernel(q_ref, kvc_ref, cover_ref, o_ref, sel_ref, *, t, n_cmp, n_kv, group, n_blk):
    qi = pl.program_id(1)
    rows = kvc_ref.shape[1]
    qpos = qi * t + lax.broadcasted_iota(jnp.int32, (t, 1), 0)
    ci = lax.broadcasted_iota(jnp.int32, (1, rows), 1)
    valid = (ci * CMP_STRIDE + CMP_BLOCK - 1 <= qpos) & (ci < n_cmp)
    blk = lax.broadcasted_iota(jnp.int32, (1, n_blk), 1)
    cur = qpos // SEL_BLOCK
    forced = (blk == 0) | (blk == cur) | (blk == cur - 1)
    cover = cover_ref[...]
    for kv in range(n_kv):
        kc = kvc_ref[0, :, kv * HD:(kv + 1) * HD].astype(BF16)
        vc = kvc_ref[0, :, (n_kv + kv) * HD:(n_kv + kv + 1) * HD].astype(BF16)
        imp = jnp.zeros((t, n_blk), F32)
        for g in range(group):
            h = kv * group + g
            s = jnp.where(valid, _nt(q_ref[0, :, h * HD:(h + 1) * HD].astype(BF16), kc) * SCALE, NEG)
            m = jnp.max(s, axis=1, keepdims=True)
            e = jnp.where(valid, jnp.exp(s - m), 0.0)
            den = jnp.sum(e, axis=1, keepdims=True)
            p = (e / jnp.where(den > 0, den, 1.0)).astype(BF16)
            o_ref[0, :, h * HD:(h + 1) * HD] = _dot(p, vc)
            imp = imp + _dot(p, cover)
        imp = jnp.where(blk * SEL_BLOCK <= qpos, jnp.where(forced, FORCE_SCORE, imp), NEG)
        sel_ref[0, :, kv * n_blk:(kv + 1) * n_blk] = _top_blocks(imp, min(N_SEL, n_blk))


def _cover_matrix(rows, n_cmp, n_blk, width):
    c_start = np.arange(rows)[:, None] * CMP_STRIDE
    s_start = np.arange(width)[None, :] * SEL_BLOCK
    cov = (c_start < s_start + SEL_BLOCK) & (c_start + CMP_BLOCK > s_start)
    cov &= (np.arange(rows)[:, None] < n_cmp) & (np.arange(width)[None, :] < n_blk)
    return jnp.asarray(cov, dtype=BF16)


def cmp_attend_prompt(q, kvc, t, n_kv):
    b, l, cq = q.shape
    rows = kvc.shape[1]
    n_blk = l // SEL_BLOCK
    cover = _cover_matrix(rows, rows - 1, n_blk, n_blk)
    return pl.pallas_call(
        functools.partial(_cmp_prompt_kernel, t=t, n_cmp=rows - 1, n_kv=n_kv, group=cq // HD // n_kv, n_blk=n_blk),
        grid=(b, l // t),
        in_specs=[pl.BlockSpec((1, t, cq), lambda bi, qi: (bi, qi, 0)),
                  pl.BlockSpec((1, rows, kvc.shape[2]), lambda bi, qi: (bi, 0, 0)),
                  pl.BlockSpec(cover.shape, lambda bi, qi: (0, 0))],
        out_specs=[pl.BlockSpec((1, t, cq), lambda bi, qi: (bi, qi, 0)),
                   pl.BlockSpec((1, t, n_kv * n_blk), lambda bi, qi: (bi, qi, 0))],
        out_shape=[jax.ShapeDtypeStruct(q.shape, F32), jax.ShapeDtypeStruct((b, l, n_kv * n_blk), F32)],
        compiler_params=_params(("parallel", "parallel")), name="cmp_attend_prompt",
    )(q, kvc, cover)


def _diag_heads(x, rows_kv, n_kv):
    out = jnp.zeros((x.shape[0], HD), F32)
    for kv in range(n_kv):
        out = out + jnp.where(rows_kv == kv, x[:, kv * HD:(kv + 1) * HD], 0.0)
    return out


def _cmp_sample_kernel(q_ref, kvc_ref, cover_ref, o_ref, sel_ref, *, n_tok, heads, n_kv, n_cmp, past_len, n_blk):
    rows = kvc_ref.shape[1]
    r_all = n_tok * heads
    c = n_kv * HD
    width = cover_ref.shape[1]
    group = heads // n_kv
    ridx = lax.broadcasted_iota(jnp.int32, (r_all, 1), 0)
    qpos = past_len + ridx // heads
    ci = lax.broadcasted_iota(jnp.int32, (1, rows), 1)
    valid = (ci * CMP_STRIDE + CMP_BLOCK - 1 <= qpos) & (ci < n_cmp)
    g_rows = r_all // group
    n_seq = q_ref.shape[0]
    imps = []
    for i in range(n_seq):
        kc = kvc_ref[i, :, 0:c].astype(BF16)
        vc = kvc_ref[i, :, c:2 * c].astype(BF16)
        s = jnp.where(valid, _nt(q_ref[i].astype(BF16), kc) * SCALE, NEG)
        m = jnp.max(s, axis=1, keepdims=True)
        e = jnp.where(valid, jnp.exp(s - m), 0.0)
        den = jnp.sum(e, axis=1, keepdims=True)
        p = (e / jnp.where(den > 0, den, 1.0)).astype(BF16)
        o_ref[i] = _diag_heads(_dot(p, vc), (ridx % heads) // group, n_kv)
        imps.append(jnp.sum(_dot(p, cover_ref[...]).reshape(g_rows, group, width), axis=1))
    imp = jnp.concatenate(imps, axis=0)
    gpos = past_len + (lax.broadcasted_iota(jnp.int32, (n_seq * g_rows, 1), 0) % g_rows) // n_kv
    blk = lax.broadcasted_iota(jnp.int32, (1, width), 1)
    cur = gpos // SEL_BLOCK
    forced = (blk == 0) | (blk == cur) | (blk == cur - 1)
    imp = jnp.where(blk * SEL_BLOCK <= gpos, jnp.where(forced, FORCE_SCORE, imp), NEG)
    imp = jnp.where(blk < n_blk, imp, DEAD)
    sel = _top_blocks(imp, min(N_SEL, n_blk))
    for i in range(n_seq):
        sel_ref[i] = sel[i * g_rows:(i + 1) * g_rows]


def cmp_attend_sample(qbd, kvc, n_tok, heads, n_kv, past_len):
    b, r_all, c = qbd.shape
    rows = kvc.shape[1]
    n_blk = past_len // SEL_BLOCK + -(-n_tok // SEL_BLOCK)
    width = -(-n_blk // LANE) * LANE
    cover = _cover_matrix(rows, rows - 1, n_blk, width)
    g_rows = r_all // (heads // n_kv)
    n_seq = next(n for n in (8, 4, 2, 1) if b % n == 0)
    return pl.pallas_call(
        functools.partial(_cmp_sample_kernel, n_tok=n_tok, heads=heads, n_kv=n_kv, n_cmp=rows - 1,
                          past_len=past_len, n_blk=n_blk),
        grid=(b // n_seq,),
        in_specs=[pl.BlockSpec((n_seq, r_all, c), lambda i: (i, 0, 0)),
                  pl.BlockSpec((n_seq, rows, 2 * c), lambda i: (i, 0, 0)),
                  pl.BlockSpec(cover.shape, lambda i: (0, 0))],
        out_specs=[pl.BlockSpec((n_seq, r_all, HD), lambda i: (i, 0, 0)),
                   pl.BlockSpec((n_seq, g_rows, width), lambda i: (i, 0, 0))],
        out_shape=[jax.ShapeDtypeStruct((b, r_all, HD), F32), jax.ShapeDtypeStruct((b, g_rows, width), F32)],
        compiler_params=_params(("parallel",)), name="cmp_attend_sample",
    )(qbd, kvc, cover)


def _dec_init(m_sc, l_sc, acc_sc):
    m_sc[...] = jnp.full(m_sc.shape, NEG, F32)
    l_sc[...] = jnp.zeros(l_sc.shape, F32)
    acc_sc[...] = jnp.zeros(acc_sc.shape, F32)


def _dec_update(qb, kt, vt, bias, mask, m_sc, l_sc, acc_sc):
    s = _dot(qb, kt) * SCALE
    if bias is not None:
        s = s + bias
    if mask is not None:
        s = jnp.where(mask, s, NEG)
    m_prev = m_sc[...]
    m_new = jnp.maximum(m_prev, jnp.max(s, axis=1, keepdims=True))
    alpha = jnp.exp(m_prev - m_new)
    p = jnp.exp(s - m_new)
    if mask is not None:
        p = jnp.where(mask, p, 0.0)
    l_sc[...] = alpha * l_sc[...] + jnp.sum(p, axis=1, keepdims=True)
    acc_sc[...] = alpha * acc_sc[...] + _nt(p.astype(BF16), vt)
    m_sc[...] = m_new


def _dec_new_tokens(q, knew, vnew, biases, masks, m_sc, l_sc, acc_sc):
    n_new = knew.shape[0]
    scores = []
    for u in range(n_new):
        s = jnp.sum(q * knew[u:u + 1, :], axis=1, keepdims=True) * SCALE
        if biases is not None:
            s = s + biases[u]
        scores.append(jnp.where(masks[u], s, NEG))
    m_prev = m_sc[...]
    m_new = m_prev
    for s in scores:
        m_new = jnp.maximum(m_new, s)
    alpha = jnp.exp(m_prev - m_new)
    l = alpha * l_sc[...]
    acc = alpha * acc_sc[...]
    for u in range(n_new):
        p = jnp.where(masks[u], jnp.exp(scores[u] - m_new), 0.0)
        l = l + p
        acc = acc + p * vnew[u:u + 1, :]
    l_sc[...] = l
    acc_sc[...] = acc
    m_sc[...] = m_new


def _token_cumsum(x, n_tok, heads):
    parts, run = [], None
    for u in range(n_tok):
        seg = x[u * heads:(u + 1) * heads]
        run = seg if run is None else run + seg
        parts.append(run)
    return jnp.concatenate(parts, axis=0), parts


def _dec_window_kernel(*refs, n_tok, heads, n_kv, lb, n_back, dil, chunk, with_sink, with_lse):
    refs = list(refs)
    q_ref, cache_ref, new_ref = refs[:3]
    pos = 3
    sink_ref = None
    if with_sink:
        sink_ref = refs[pos]
        pos += 1
    o_ref = refs[pos]
    pos += 1
    lse_ref = None
    if with_lse:
        lse_ref = refs[pos]
        pos += 1
    m_sc, l_sc, acc_sc = refs[pos:pos + 3]
    r_all = n_tok * heads
    c = n_kv * HD
    group = heads // n_kv
    ridx = lax.broadcasted_iota(jnp.int32, (r_all, 1), 0)
    tok = ridx // heads
    _dec_init(m_sc, l_sc, acc_sc)
    q = q_ref[0]
    qb = q.astype(BF16)

    def ok(dist):
        good = (dist >= 0) & (dist <= n_back)
        if dil > 1:
            good = good & (dist % dil == 0)
        return good

    for c0 in range(0, lb, chunk):
        kt = cache_ref[0, 0:c, c0:c0 + chunk].astype(BF16)
        vt = cache_ref[0, c:2 * c, c0:c0 + chunk].astype(BF16)
        dist = (lb + tok) - (c0 + lax.broadcasted_iota(jnp.int32, (1, chunk), 1))
        _dec_update(qb, kt, vt, None, ok(dist), m_sc, l_sc, acc_sc)
    new = new_ref[0]
    _dec_new_tokens(q, new[:, 0:c], new[:, c:2 * c], None, [ok(tok - u) for u in range(n_tok)], m_sc, l_sc, acc_sc)
    m, l, acc = m_sc[...], l_sc[...], acc_sc[...]
    if with_sink:
        sk = sink_ref[...]
        m_f = jnp.maximum(m, sk)
        scale_old = jnp.exp(m - m_f)
        l = l * scale_old + jnp.exp(sk - m_f)
        acc = acc * scale_old
        m = m_f
    o_ref[0] = _diag_heads(acc / l, (ridx % heads) // group, n_kv)
    if with_lse:
        lse_ref[0] = jnp.broadcast_to(m + jnp.log(l), (r_all, HD))


def dec_window(qbd, cache_t, new, n_tok, heads, n_kv, n_back, dil, sink_rows=None, with_lse=False):
    b, r_all, c = qbd.shape
    lb = cache_t.shape[2]
    in_specs = [pl.BlockSpec((1, r_all, c), lambda i: (i, 0, 0)),
                pl.BlockSpec((1, 2 * c, lb), lambda i: (i, 0, 0)),
                pl.BlockSpec((1, n_tok, 2 * c), lambda i: (i, 0, 0))]
    args = [qbd, cache_t, new]
    if sink_rows is not None:
        in_specs.append(pl.BlockSpec((r_all, 1), lambda i: (0, 0)))
        args.append(sink_rows)
    o_spec = pl.BlockSpec((1, r_all, HD), lambda i: (i, 0, 0))
    o_shape = jax.ShapeDtypeStruct((b, r_all, HD), F32)
    return pl.pallas_call(
        functools.partial(_dec_window_kernel, n_tok=n_tok, heads=heads, n_kv=n_kv, lb=lb, n_back=n_back, dil=dil,
                          chunk=min(lb, 512), with_sink=sink_rows is not None, with_lse=with_lse),
        grid=(b,), in_specs=in_specs,
        out_specs=[o_spec, o_spec] if with_lse else o_spec,
        out_shape=[o_shape, o_shape] if with_lse else o_shape,
        scratch_shapes=[pltpu.VMEM((r_all, 1), F32), pltpu.VMEM((r_all, 1), F32), pltpu.VMEM((r_all, c), F32)],
        compiler_params=_params(("parallel",)), name="dec_window",
    )(*args)


def _dec_paged_kernel(*refs, n_pg, n_tok, heads, n_kv, mode):
    pt_ref, pool_ref = refs[:2]
    pos = 2
    lf_ref = None
    if mode == "fox":
        lf_ref = refs[pos]
        pos += 1
    q_ref, new_ref, x_ref = refs[pos:pos + 3]
    pos += 3
    if mode == "fox":
        tri_ref = refs[pos]
        pos += 1
    o_ref = refs[pos]
    pos += 1
    buf, sem, k_sc, v_sc, m_sc, l_sc, acc_sc = refs[pos:pos + 7]
    pos += 7
    if mode == "fox":
        lf_buf, lf_sem, c_sc, carry_sc = refs[pos:pos + 4]
    bi, gi = pl.program_id(0), pl.program_id(1)
    n_grp = pl.num_programs(1)
    step = bi * n_grp + gi
    slot = lax.rem(step, 2)

    def page_copies(seq_i, grp_i, into):
        copies = []
        for p in range(n_pg):
            page = pt_ref[seq_i, grp_i * n_pg + p]
            copies.append(pltpu.make_async_copy(pool_ref.at[page], buf.at[into, p], sem.at[into]))
            if mode == "fox":
                copies.append(pltpu.make_async_copy(lf_ref.at[page], lf_buf.at[into, p], lf_sem.at[into]))
        return copies

    @pl.when(step == 0)
    def _():
        for cp in page_copies(bi, gi, slot):
            cp.start()

    @pl.when(step + 1 < pl.num_programs(0) * n_grp)
    def _():
        nxt = step + 1
        for cp in page_copies(nxt // n_grp, lax.rem(nxt, n_grp), 1 - slot):
            cp.start()

    for cp in page_copies(bi, gi, slot):
        cp.wait()
    pages = [buf.at[slot, p] for p in range(n_pg)]
    lf_pages = [lf_buf.at[slot, p] for p in range(n_pg)] if mode == "fox" else None
    r_all = n_tok * heads
    c = n_kv * HD
    group = heads // n_kv
    tk = n_pg * PAGE
    ridx = lax.broadcasted_iota(jnp.int32, (r_all, 1), 0)
    tok = ridx // heads
    head = ridx % heads

    @pl.when(gi == 0)
    def _():
        _dec_init(m_sc, l_sc, acc_sc)
        if mode == "fox":
            carry_sc[...] = jnp.zeros(carry_sc.shape, F32)

    for i in range(n_pg):
        k_sc[:, i * PAGE:(i + 1) * PAGE] = pages[i][0:c, :].astype(BF16)
        v_sc[:, i * PAGE:(i + 1) * PAGE] = pages[i][c:2 * c, :].astype(BF16)
    q = q_ref[0]
    qb = q.astype(BF16)

    if mode == "fox":
        tri = tri_ref[...]
        hi, mid, lo = _split3(jnp.concatenate([lf_pages[i][...] for i in range(n_pg)], axis=0))
        within = _dot(hi, tri) + _dot(mid, tri) + _dot(lo, tri)
        carry = carry_sc[...]
        for i in range(n_pg):
            page = within[i * heads:(i + 1) * heads]
            c_sc[:, i * PAGE:(i + 1) * PAGE] = page + carry
            carry = carry + page[:, PAGE - 1:PAGE]
        carry_sc[...] = carry
        bias = _token_cumsum(x_ref[0], n_tok, heads)[0] - jnp.concatenate([c_sc[...]] * n_tok, axis=0)
        _dec_update(qb, k_sc[...], v_sc[...], bias, None, m_sc, l_sc, acc_sc)
    else:
        flags = x_ref[0].astype(BF16)
        width = flags.shape[1]
        key_blk = (gi * tk + lax.broadcasted_iota(jnp.int32, (width, tk), 1)) // SEL_BLOCK
        expand = jnp.where(lax.broadcasted_iota(jnp.int32, (width, tk), 0) == key_blk, 1.0, 0.0).astype(BF16)
        _dec_update(qb, k_sc[...], v_sc[...], None, _dot(flags, expand) > 0.5, m_sc, l_sc, acc_sc)

    @pl.when(gi == pl.num_programs(1) - 1)
    def _():
        new = new_ref[0]
        causal = [tok >= u for u in range(n_tok)]
        if mode == "fox":
            fq, per_tok = _token_cumsum(x_ref[0], n_tok, heads)
            total = jnp.concatenate([carry_sc[...]] * n_tok, axis=0)
            biases = [fq - jnp.concatenate([per_tok[u]] * n_tok, axis=0) - total for u in range(n_tok)]
            _dec_new_tokens(q, new[:, 0:c], new[:, c:2 * c], biases, causal, m_sc, l_sc, acc_sc)
        else:
            n_past_blk = (pl.num_programs(1) * tk) // SEL_BLOCK
            flags = x_ref[0]
            lane = lax.broadcasted_iota(jnp.int32, flags.shape, 1)
            new_on = jnp.sum(jnp.where(lane == n_past_blk, flags, 0.0), axis=1, keepdims=True) > 0.5
            _dec_new_tokens(q, new[:, 0:c], new[:, c:2 * c], None, [m & new_on for m in causal], m_sc, l_sc, acc_sc)
        o_ref[0] = _diag_heads(acc_sc[...] / l_sc[...], head // group, n_kv)


def dec_paged(qbd, pool_t, page_table, new, x, n_tok, heads, n_kv, mode, n_pg, lf_pool_t=None):
    b, r_all, c = qbd.shape
    n_pages = page_table.shape[1]
    in_specs = [pl.BlockSpec(memory_space=pl.ANY)]
    args = [pool_t]
    if mode == "fox":
        in_specs.append(pl.BlockSpec(memory_space=pl.ANY))
        args.append(lf_pool_t)
    in_specs += [pl.BlockSpec((1, r_all, c), lambda i, g, pt: (i, 0, 0)),
                 pl.BlockSpec((1, n_tok, 2 * c), lambda i, g, pt: (i, 0, 0)),
                 pl.BlockSpec((1, r_all, x.shape[2]), lambda i, g, pt: (i, 0, 0))]
    args += [qbd, new, x]
    scratch = [pltpu.VMEM((2, n_pg, 2 * c, PAGE), F32), pltpu.SemaphoreType.DMA((2,)),
               pltpu.VMEM((c, n_pg * PAGE), BF16), pltpu.VMEM((c, n_pg * PAGE), BF16),
               pltpu.VMEM((r_all, 1), F32), pltpu.VMEM((r_all, 1), F32), pltpu.VMEM((r_all, c), F32)]
    if mode == "fox":
        tri = jnp.asarray(np.triu(np.ones((PAGE, PAGE), np.float32)), dtype=BF16)
        in_specs.append(pl.BlockSpec((PAGE, PAGE), lambda i, g, pt: (0, 0)))
        args.append(tri)
        scratch += [pltpu.VMEM((2, n_pg, heads, PAGE), F32), pltpu.SemaphoreType.DMA((2,)),
                    pltpu.VMEM((heads, n_pg * PAGE), F32), pltpu.VMEM((heads, 1), F32)]
    return pl.pallas_call(
        functools.partial(_dec_paged_kernel, n_pg=n_pg, n_tok=n_tok, heads=heads, n_kv=n_kv, mode=mode),
        grid_spec=pltpu.PrefetchScalarGridSpec(
            num_scalar_prefetch=1, grid=(b, n_pages // n_pg), in_specs=in_specs,
            out_specs=pl.BlockSpec((1, r_all, HD), lambda i, g, pt: (i, 0, 0)), scratch_shapes=scratch),
        out_shape=jax.ShapeDtypeStruct((b, r_all, HD), F32),
        compiler_params=_params(("arbitrary", "arbitrary")), name="dec_paged_" + mode,
    )(page_table, *args)


def _block_diag_q(q, n_tok, heads, n_kv):
    b = q.shape[0] // n_tok
    own = jnp.asarray(np.arange(heads)[:, None] // (heads // n_kv) == np.arange(n_kv)[None, :], dtype=F32)
    qb = q.reshape(b, n_tok, heads, 1, HD) * own[None, None, :, :, None]
    return qb.reshape(b, n_tok * heads, n_kv * HD)


def _rows_first(z_t, n_kv):
    b, _, l = z_t.shape
    return z_t.reshape(b, 2, n_kv, HD, l).transpose(0, 4, 1, 2, 3)


def _rows_last(cache):
    n, rows = cache.shape[:2]
    return cache.transpose(0, 2, 3, 4, 1).reshape(n, -1, rows)


def _mixer_a(hp, hs, g, w_in, w_out, phi, caches, page_table, dims, ropes):
    bsz, seq, dec_b, dec_t, past_len = dims
    (cos_p, sin_p), (cos_s, sin_s) = ropes
    cache_cmp, cache_sel, cache_win = caches
    heads, n_kv = 16, 2
    segs = [(0, 1024, 0, 1024, True), (1024, 1280, 0, 0, False), (1280, 1536, 0, 128, False),
            (1536, 1792, 0, 128, False), (1792, 1840, 0, 0, False)]
    w1p, b1p, w2p = phi
    q_cmp, q_rot, cmp_raw, sel_kv, win_kv, gates, cmp_t, sel_t, win_t = norm_proj(
        hp, g, w_in, cos_p, sin_p, segs, 512, flipped=(1, 2, 3), seq=seq)
    n = bsz * seq
    kvc = compress(cmp_raw.reshape(bsz, seq, -1), None, w1p, b1p, w2p)
    o_cmp, flags = cmp_attend_prompt(q_cmp.reshape(bsz, seq, -1), kvc, 512, n_kv)
    q3 = q_rot.reshape(bsz, seq, -1)
    o_sel = flash_attention(q3, sel_kv.reshape(bsz, seq, -1), 256, 1024, 1, "sel", (flags,))
    o_win = band_attention(q3, win_kv.reshape(bsz, seq, -1), A_WINDOW - 1, 1, 256)
    hp = merge_a(o_cmp.reshape(n, -1), o_sel.reshape(n, -1), o_win.reshape(n, -1), gates, hp, w_out)
    kv5 = lambda z, b, t: z.reshape(b, t, 2, n_kv, HD)
    w = min(A_WINDOW, seq)
    outs_p = (_rows_first(cmp_t, n_kv), _rows_first(sel_t, n_kv), _rows_first(win_t[:, :, seq - w:], n_kv))
    q_cmp, q_rot, cmp_raw, sel_kv, win_kv, gates = norm_proj(hs, g, w_in, cos_s, sin_s, segs, 512)
    ns = dec_b * dec_t
    kvc = compress(_rows_last(cache_cmp), page_table, w1p, b1p, w2p)
    o_cmp, flags = cmp_attend_sample(_block_diag_q(q_cmp, dec_t, heads, n_kv), kvc, dec_t, heads, n_kv, past_len)
    qbd = _block_diag_q(q_rot, dec_t, heads, n_kv)
    flag_rows = jnp.repeat(flags.reshape(dec_b, dec_t, n_kv, 1, -1), heads // n_kv, axis=3).reshape(dec_b, dec_t * heads, -1)
    o_sel = dec_paged(qbd, _rows_last(cache_sel), page_table, sel_kv.reshape(dec_b, dec_t, -1), flag_rows,
                      dec_t, heads, n_kv, "sel", 32)
    o_win = dec_window(qbd, _rows_last(cache_win), win_kv.reshape(dec_b, dec_t, -1), dec_t, heads, n_kv, A_WINDOW - 1, 1)
    hs = merge_a(o_cmp.reshape(ns, -1), o_sel.reshape(ns, -1), o_win.reshape(ns, -1), gates, hs, w_out)
    outs_s = (kv5(cmp_raw, dec_b, dec_t), kv5(sel_kv, dec_b, dec_t), kv5(win_kv, dec_b, dec_t))
    return hp, hs, outs_p, outs_s


def _mixer_b(hp, hs, g, w_in, sinks, w_out, cache_win, dims, ropes):
    bsz, seq, dec_b, dec_t, _ = dims
    (cos_p, sin_p), (cos_s, sin_s) = ropes
    heads, n_kv = 16, 2
    segs = [(0, 1024, 0, 1024, False), (1024, 1280, 0, 128, False)]
    q, kv, kv_t = norm_proj(hp, g, w_in, cos_p, sin_p, segs, 512, flipped=(1,), seq=seq)
    o = band_attention(q.reshape(bsz, seq, -1), kv.reshape(bsz, seq, -1), B_WINDOW - 1, 1, 256, sink=sinks)
    hp = out_proj(o.reshape(bsz * seq, -1), hp, w_out)
    w = min(B_WINDOW, seq)
    out_p = _rows_first(kv_t[:, :, seq - w:], n_kv)
    q, kv = norm_proj(hs, g, w_in, cos_s, sin_s, segs, 512)
    sink_rows = jnp.tile(sinks.astype(F32), dec_t).reshape(dec_t * heads, 1)
    o = dec_window(_block_diag_q(q, dec_t, heads, n_kv), _rows_last(cache_win), kv.reshape(dec_b, dec_t, -1),
                   dec_t, heads, n_kv, B_WINDOW - 1, 1, sink_rows=sink_rows)
    hs = out_proj(o.reshape(dec_b * dec_t, -1), hs, w_out)
    return hp, hs, out_p, kv.reshape(dec_b, dec_t, 2, n_kv, HD)


def _mixer_c(hp, hs, g, w_in, b_f, w_out, cache_kv, cache_logf, page_table, dims, ropes):
    bsz, seq, dec_b, dec_t, _ = dims
    (cos_p, sin_p), (cos_s, sin_s) = ropes
    heads, n_kv = 16, 4
    segs = [(0, 1024, 0, 0, False), (1024, 1536, 0, 0, False), (1536, 1552, 0, 0, False)]
    q, kv, f, kv_t = norm_proj(hp, g, w_in, cos_p, sin_p, segs, 512, flipped=(1,), seq=seq)
    lf_t, cum_t = fox_gate(f.reshape(bsz, seq, heads).transpose(0, 2, 1), b_f)
    logf_p = lf_t.transpose(0, 2, 1)
    half = heads // 2
    cum_q = cum_t.reshape(bsz, 2, half, seq)
    cum_k = cum_q.transpose(0, 1, 3, 2)
    o = flash_attention(q.reshape(bsz, seq, -1), kv.reshape(bsz, seq, -1), 512, 512, 2, "fox", (cum_q, cum_k))
    hp = out_proj(o.reshape(bsz * seq, -1), hp, w_out)
    out_p = _rows_first(kv_t, n_kv)
    q, kv, f = norm_proj(hs, g, w_in, cos_s, sin_s, segs, 512)
    ns = dec_b * dec_t
    lf_t, _ = fox_gate(f.reshape(1, ns, heads).transpose(0, 2, 1), b_f)
    logf_s = lf_t.transpose(0, 2, 1).reshape(dec_b, dec_t, heads)
    o = dec_paged(_block_diag_q(q, dec_t, heads, n_kv), _rows_last(cache_kv), page_table, kv.reshape(dec_b, dec_t, -1),
                  logf_s.reshape(dec_b, dec_t * heads, 1), dec_t, heads, n_kv, "fox", 32,
                  lf_pool_t=cache_logf.transpose(0, 2, 1))
    hs = out_proj(o.reshape(ns, -1), hs, w_out)
    return hp, hs, (out_p, logf_p), (kv.reshape(dec_b, dec_t, 2, n_kv, HD), logf_s)


def _mixer_d(hp, hs, g, w_in, w_out, caches, dims, ropes):
    bsz, seq, dec_b, dec_t, _ = dims
    (cos_p, sin_p), (cos_s, sin_s) = ropes
    heads = 8
    c = heads * HD
    segs = []
    for grp in range(len(D_GROUPS)):
        base = 3 * grp * c
        segs += [(base, base + c, 0, c, False), (base + c, base + 3 * c, 0, c, False)]
    res = norm_proj(hp, g, w_in, cos_p, sin_p, segs, 256, flipped=(1, 3, 5), seq=seq)
    outs, lses, states_p = [], [], []
    for grp, (window, dil) in enumerate(D_GROUPS):
        q, kv = res[2 * grp], res[2 * grp + 1]
        if dil == 1:
            o, lse = band_attention(q.reshape(bsz, seq, c), kv.reshape(bsz, seq, 2 * c), window, 1, 256, with_lse=True)
        else:
            o, lse = dilated_attention(q.reshape(bsz, seq, c), kv.reshape(bsz, seq, 2 * c), window, dil)
        outs.append(o.reshape(bsz * seq, c))
        lses.append(lse.reshape(bsz * seq, c))
        states_p.append(_rows_first(res[6 + grp][:, :, seq - min(window, seq):], heads))
    hp = merge_d(outs, lses, hp, w_out)
    res = norm_proj(hs, g, w_in, cos_s, sin_s, segs, 256)
    outs, lses, states_s = [], [], []
    for grp, (window, dil) in enumerate(D_GROUPS):
        q, kv = res[2 * grp], res[2 * grp + 1]
        o, lse = dec_window(_block_diag_q(q, dec_t, heads, heads), _rows_last(caches[grp]),
                            kv.reshape(dec_b, dec_t, -1), dec_t, heads, heads, window, dil, with_lse=True)
        outs.append(o.reshape(dec_b * dec_t, c))
        lses.append(lse.reshape(dec_b * dec_t, c))
        states_s.append(kv.reshape(dec_b, dec_t, 2, heads, HD))
    hs = merge_d(outs, lses, hs, w_out)
    return hp, hs, states_p, states_s


def kernel(x_prompt, x_sample, cache_a_cmp, cache_a_sel, cache_a_win, cache_b_win, cache_c_kv, cache_c_logf, cache_d1, cache_d2, cache_d3, page_table, norm_mix, norm_ffn, norm_final, a_w_in, a_phi_w1, a_phi_b1, a_phi_w2, a_w_out, b_w_in, b_sinks, b_w_out, c_w_in, c_b_f, c_w_out, d_w_in, d_w_out, ffn_w_gate, ffn_w_up, ffn_w_down):
    bsz, seq, d_model = x_prompt.shape
    dec_b, dec_t, _ = x_sample.shape
    past_len = page_table.shape[1] * PAGE
    dims = (bsz, seq, dec_b, dec_t, past_len)
    ropes = (_rope_tables(jnp.arange(seq)), _rope_tables(past_len + jnp.arange(dec_b * dec_t) % dec_t))
    hp = x_prompt.reshape(bsz * seq, d_model)
    hs = x_sample.reshape(dec_b * dec_t, d_model)
    bf = lambda z: z.astype(BF16)
    phi = _compress_weights(a_phi_w1, a_phi_b1, a_phi_w2)
    depth = norm_mix.shape[0]
    for layer in range(depth):
        kind = layer % 4
        g = norm_mix[layer]
        if kind == 0:
            hp, hs, a_p, a_s = _mixer_a(hp, hs, g, bf(a_w_in), bf(a_w_out), phi,
                                        (cache_a_cmp, cache_a_sel, cache_a_win), page_table, dims, ropes)
        elif kind == 1:
            hp, hs, b_p, b_s = _mixer_b(hp, hs, g, bf(b_w_in), b_sinks, bf(b_w_out), cache_b_win, dims, ropes)
        elif kind == 2:
            hp, hs, c_p, c_s = _mixer_c(hp, hs, g, bf(c_w_in), c_b_f, bf(c_w_out), cache_c_kv, cache_c_logf,
                                        page_table, dims, ropes)
        else:
            hp, hs, d_p, d_s = _mixer_d(hp, hs, g, bf(d_w_in), bf(d_w_out), (cache_d1, cache_d2, cache_d3), dims, ropes)
        wg, wu, wd = bf(ffn_w_gate[layer]), bf(ffn_w_up[layer]), bf(ffn_w_down[layer])
        hp = ffn(hp, norm_ffn[layer], wg, wu, wd)
        hs = ffn(hs, norm_ffn[layer], wg, wu, wd)
    y_p = final_norm(hp, norm_final).reshape(bsz, seq, d_model)
    y_s = final_norm(hs, norm_final).reshape(dec_b, dec_t, d_model)
    return (y_p, y_s, a_p[0], a_s[0], a_p[1], a_s[1], a_p[2], a_s[2], b_p, b_s,
            c_p[0], c_s[0], c_p[1], c_s[1], d_p[0], d_s[0], d_p[1], d_s[1], d_p[2], d_s[2])
```

```python
import functools

import numpy as np
import jax
import jax.numpy as jnp
from jax import lax
from jax.experimental import pallas as pl
from jax.experimental.pallas import tpu as pltpu

F32 = jnp.float32
BF16 = jnp.bfloat16

HD = 64
SCALE = HD ** -0.5
LOG2E = 1.4426950408889634
LN2 = 0.6931471805599453
RMS_EPS = 1e-6
ROPE_THETA = 10000.0
PAGE = 128
CMP_STRIDE = 16
CMP_BLOCK = 32
SEL_BLOCK = 64
N_SEL = 16
A_WINDOW = 512
B_WINDOW = 128
D_GROUPS = ((128, 1), (512, 4), (2048, 16))
FORCE_SCORE = 1e9
NEG = -1e30
DEAD = -3e38
LANE = 128
SUBLANE = 8
ONES_ROWS = 16
VMEM_LIMIT = 52 * 1024 * 1024


def _params(sem, vmem=VMEM_LIMIT):
    return pltpu.CompilerParams(dimension_semantics=sem, vmem_limit_bytes=vmem)


def _nt(a, b):
    return lax.dot_general(a, b, (((1,), (1,)), ((), ())), preferred_element_type=F32)


def _dot(a, b):
    return jnp.dot(a, b, preferred_element_type=F32)


def _split3(x):
    hi = x.astype(BF16)
    r1 = x - hi.astype(F32)
    mid = r1.astype(BF16)
    lo = (r1 - mid.astype(F32)).astype(BF16)
    return hi, mid, lo


def _rms(x, g):
    var = jnp.mean(x * x, axis=-1, keepdims=True)
    return x * lax.rsqrt(var + RMS_EPS) * g


def _rope_chunk(x, cos2, sin2):
    lane = lax.broadcasted_iota(jnp.int32, x.shape, 1)
    first = (lane % HD) < (HD // 2)
    swapped = jnp.where(first, pltpu.roll(x, LANE - HD // 2, axis=1), pltpu.roll(x, HD // 2, axis=1))
    return x * cos2 + swapped * sin2


def _rope_tables(pos):
    half = HD // 2
    inv = ROPE_THETA ** (-jnp.arange(half, dtype=F32) / half)
    ang = pos.astype(F32)[:, None] * inv[None, :]
    c, s = jnp.cos(ang), jnp.sin(ang)
    return jnp.tile(c, (1, 4)), jnp.tile(jnp.concatenate([-s, s], axis=1), (1, 2))


def _norm_proj_kernel(x_ref, g_ref, w_ref, cos_ref, sin_ref, *out_refs, segs, flipped):
    xn = _rms(x_ref[...], g_ref[...]).astype(BF16)
    outs = list(out_refs)
    written = []
    for c0, c1, r0, r1, dup in segs:
        acc = _dot(xn, w_ref[:, c0:c1])
        if dup:
            outs.pop(0)[...] = acc
        o_ref = outs.pop(0)
        written.append(o_ref)
        if r1 <= r0:
            o_ref[...] = acc
            continue
        if r0 > 0:
            o_ref[:, 0:r0] = acc[:, 0:r0]
        for a in range(r0, r1, LANE):
            o_ref[:, a:a + LANE] = _rope_chunk(acc[:, a:a + LANE], cos_ref[...], sin_ref[...])
        if r1 < c1 - c0:
            o_ref[:, r1:] = acc[:, r1:]
    for idx in flipped:
        outs.pop(0)[0] = written[idx][...].T


def norm_proj(x, g, w, cos2, sin2, segs, tm, flipped=(), seq=None):
    n, d = x.shape
    tm = min(tm, n)
    out_shape, out_specs = [], []
    for c0, c1, _, _, dup in segs:
        for _ in range(2 if dup else 1):
            out_shape.append(jax.ShapeDtypeStruct((n, c1 - c0), F32))
            out_specs.append(pl.BlockSpec((tm, c1 - c0), lambda i: (i, 0)))
    for idx in flipped:
        width = segs[idx][1] - segs[idx][0]
        per_seq = seq // tm
        out_shape.append(jax.ShapeDtypeStruct((n // seq, width, seq), F32))
        out_specs.append(pl.BlockSpec((1, width, tm), lambda i: (i // per_seq, 0, i % per_seq)))
    return pl.pallas_call(
        functools.partial(_norm_proj_kernel, segs=tuple(segs), flipped=tuple(flipped)),
        grid=(n // tm,),
        in_specs=[pl.BlockSpec((tm, d), lambda i: (i, 0)),
                  pl.BlockSpec((1, d), lambda i: (0, 0)),
                  pl.BlockSpec(w.shape, lambda i: (0, 0)),
                  pl.BlockSpec((tm, LANE), lambda i: (i % (cos2.shape[0] // tm), 0)),
                  pl.BlockSpec((tm, LANE), lambda i: (i % (sin2.shape[0] // tm), 0))],
        out_specs=out_specs, out_shape=out_shape,
        compiler_params=_params(("parallel",)), name="norm_proj",
    )(x, g.reshape(1, d), w, cos2, sin2)


def _out_proj_kernel(o_ref, h_ref, w_ref, out_ref):
    out_ref[...] = h_ref[...] + _dot(o_ref[...].astype(BF16), w_ref[...])


def out_proj(o, h, w, tm=512):
    n, k = o.shape
    d = w.shape[1]
    tm = min(tm, n)
    return pl.pallas_call(
        _out_proj_kernel, grid=(n // tm,),
        in_specs=[pl.BlockSpec((tm, k), lambda i: (i, 0)), pl.BlockSpec((tm, d), lambda i: (i, 0)),
                  pl.BlockSpec(w.shape, lambda i: (0, 0))],
        out_specs=pl.BlockSpec((tm, d), lambda i: (i, 0)),
        out_shape=jax.ShapeDtypeStruct((n, d), F32),
        compiler_params=_params(("parallel",)), name="out_proj",
    )(o, h, w)


def _merge_a_kernel(oc_ref, os_ref, ow_ref, g_ref, h_ref, w_ref, out_ref, o_sc, *, heads):
    gate = jax.nn.sigmoid(g_ref[...])
    for h in range(heads):
        sl = slice(h * HD, (h + 1) * HD)
        o = (gate[:, 3 * h:3 * h + 1] * oc_ref[:, sl] + gate[:, 3 * h + 1:3 * h + 2] * os_ref[:, sl]
             + gate[:, 3 * h + 2:3 * h + 3] * ow_ref[:, sl])
        o_sc[:, sl] = o
    out_ref[...] = h_ref[...] + _dot(o_sc[...].astype(BF16), w_ref[...])


def merge_a(o_cmp, o_sel, o_win, gates, h, w, tm=512):
    n, k = o_cmp.shape
    d = w.shape[1]
    tm = min(tm, n)
    row = lambda c: pl.BlockSpec((tm, c), lambda i: (i, 0))
    return pl.pallas_call(
        functools.partial(_merge_a_kernel, heads=k // HD), grid=(n // tm,),
        in_specs=[row(k), row(k), row(k), row(gates.shape[1]), row(d), pl.BlockSpec(w.shape, lambda i: (0, 0))],
        out_specs=row(d), out_shape=jax.ShapeDtypeStruct((n, d), F32),
        scratch_shapes=[pltpu.VMEM((tm, k), F32)],
        compiler_params=_params(("parallel",)), name="merge_a",
    )(o_cmp, o_sel, o_win, gates, h, w)


def _merge_d_kernel(o1, o2, o3, l1, l2, l3, h_ref, w_ref, out_ref):
    a, b, c = l1[...], l2[...], l3[...]
    m = jnp.maximum(jnp.maximum(a, b), c)
    ea, eb, ec = jnp.exp(a - m), jnp.exp(b - m), jnp.exp(c - m)
    den = ea + eb + ec
    o = (ea / den) * o1[...] + (eb / den) * o2[...] + (ec / den) * o3[...]
    out_ref[...] = h_ref[...] + _dot(o.astype(BF16), w_ref[...])


def merge_d(outs, lses, h, w, tm=512):
    n, k = outs[0].shape
    d = w.shape[1]
    tm = min(tm, n)
    row = lambda c: pl.BlockSpec((tm, c), lambda i: (i, 0))
    return pl.pallas_call(
        _merge_d_kernel, grid=(n // tm,),
        in_specs=[row(k)] * 6 + [row(d), pl.BlockSpec(w.shape, lambda i: (0, 0))],
        out_specs=row(d), out_shape=jax.ShapeDtypeStruct((n, d), F32),
        compiler_params=_params(("parallel",)), name="merge_d",
    )(*outs, *lses, h, w)


def _ffn_kernel(h_ref, g_ref, wg_ref, wu_ref, wd_ref, out_ref, xn_sc, acc_sc):
    j = pl.program_id(1)

    @pl.when(j == 0)
    def _():
        x = h_ref[...]
        xn_sc[...] = _rms(x, g_ref[...]).astype(BF16)
        acc_sc[...] = x

    xn = xn_sc[...]
    a = _dot(xn, wg_ref[...])
    u = _dot(xn, wu_ref[...])
    mid = (a * jax.nn.sigmoid(a) * u).astype(BF16)
    acc_sc[...] += _dot(mid, wd_ref[...])

    @pl.when(j == pl.num_programs(1) - 1)
    def _():
        out_ref[...] = acc_sc[...]


def ffn(h, g, wg, wu, wd, tm=512, n_f=2):
    n, d = h.shape
    f = wg.shape[1]
    tm = min(tm, n)
    tf = f // n_f
    return pl.pallas_call(
        _ffn_kernel, grid=(n // tm, n_f),
        in_specs=[pl.BlockSpec((tm, d), lambda i, j: (i, 0)), pl.BlockSpec((1, d), lambda i, j: (0, 0)),
                  pl.BlockSpec((d, tf), lambda i, j: (0, j)), pl.BlockSpec((d, tf), lambda i, j: (0, j)),
                  pl.BlockSpec((tf, d), lambda i, j: (j, 0))],
        out_specs=pl.BlockSpec((tm, d), lambda i, j: (i, 0)),
        out_shape=jax.ShapeDtypeStruct((n, d), F32),
        scratch_shapes=[pltpu.VMEM((tm, d), BF16), pltpu.VMEM((tm, d), F32)],
        compiler_params=_params(("parallel", "arbitrary")), name="ffn",
    )(h, g.reshape(1, d), wg, wu, wd)


def _final_norm_kernel(x_ref, g_ref, o_ref):
    o_ref[...] = _rms(x_ref[...], g_ref[...])


def final_norm(x, g, tm=1024):
    n, d = x.shape
    tm = min(tm, n)
    return pl.pallas_call(
        _final_norm_kernel, grid=(n // tm,),
        in_specs=[pl.BlockSpec((tm, d), lambda i: (i, 0)), pl.BlockSpec((1, d), lambda i: (0, 0))],
        out_specs=pl.BlockSpec((tm, d), lambda i: (i, 0)),
        out_shape=jax.ShapeDtypeStruct((n, d), F32),
        compiler_params=_params(("parallel",)), name="final_norm",
    )(x, g.reshape(1, d))


def _fox_gate_kernel(f_ref, b_ref, lf_ref, cum_ref):
    x = f_ref[0] + b_ref[...]
    lf = jnp.minimum(x, 0.0) - jnp.log1p(jnp.exp(-jnp.abs(x)))
    lf_ref[0] = lf
    n = x.shape[1]
    lane = lax.broadcasted_iota(jnp.int32, x.shape, 1)
    c = lf
    s = 1
    while s < n:
        c = c + jnp.where(lane >= s, pltpu.roll(c, s, axis=1), 0.0)
        s *= 2
    cum_ref[0] = c


def fox_gate(f_t, b_f):
    b, hh, n = f_t.shape
    spec = pl.BlockSpec((1, hh, n), lambda i: (i, 0, 0))
    return pl.pallas_call(
        _fox_gate_kernel, grid=(b,),
        in_specs=[spec, pl.BlockSpec((hh, 1), lambda i: (0, 0))],
        out_specs=[spec, spec], out_shape=[jax.ShapeDtypeStruct(f_t.shape, F32)] * 2,
        compiler_params=_params(("parallel",)), name="fox_gate",
    )(f_t, b_f.reshape(hh, 1))


def _attn_init(m_sc, acc_sc):
    m_sc[...] = jnp.full(m_sc.shape, NEG, F32)
    acc_sc[...] = jnp.zeros(acc_sc.shape, F32)


def _value_rows(v_ref):
    vt = v_ref[0].T.astype(BF16)
    ones = jnp.ones((ONES_ROWS, vt.shape[1]), BF16)
    return [jnp.concatenate([vt[kv * HD:(kv + 1) * HD], ones], axis=0) for kv in range(vt.shape[0] // HD)]


def _attn_steps(slots, work, tq, m_sc, acc_sc):
    state = [(m_sc[h], acc_sc[h]) for h in slots]
    steps = [_attn_step(m, acc, *args, tq) for (m, acc), args in zip(state, work)]
    new = {}
    while len(new) < len(steps):
        for i, step in enumerate(steps):
            if i not in new:
                try:
                    next(step)
                except StopIteration as done:
                    new[i] = done.value
    for i, h in enumerate(slots):
        m_sc[h], acc_sc[h] = new[i]


def _attn_step(m_prev, acc_prev, k, vt, q, addends, tq):
    s = _nt(k, q)
    yield
    tk = s.shape[0]
    per_query = None
    if addends is not None and any(t.shape[0] == 1 for terms in addends for t in terms):
        per_query = [[t for t in terms if t.shape[0] == 1] for terms in addends]
        per_query = [sum(ts[1:], ts[0]) if ts else jnp.zeros((1, tq), F32) for ts in per_query]
        per_query = per_query[0] if len(per_query) == 1 else jnp.concatenate(per_query, axis=1)

    def rows(r0, r1):
        blk = s[r0:r1]
        if addends is None:
            return blk
        pieces = []
        for g, terms in enumerate(addends):
            piece = blk[:, g * tq:(g + 1) * tq]
            for term in terms:
                if term.shape[0] != 1:
                    piece = piece + term[r0:r1]
            pieces.append(piece)
        return pieces[0] if len(pieces) == 1 else jnp.concatenate(pieces, axis=1)

    mx = rows(0, SUBLANE)
    for r0 in range(SUBLANE, tk, SUBLANE):
        mx = jnp.maximum(mx, rows(r0, r0 + SUBLANE))
    mx = jnp.max(mx, axis=0, keepdims=True)
    m_new = jnp.maximum(m_prev, mx if per_query is None else mx + per_query)
    shift = m_new if per_query is None else m_new - per_query
    yield
    strip = 2 * SUBLANE
    p = jnp.concatenate([jnp.exp2(rows(r0, r0 + strip) - shift).astype(BF16) for r0 in range(0, tk, strip)], axis=0)
    yield
    return m_new, jnp.exp2(m_prev - m_new) * acc_prev + _dot(vt, p)


def _attn_finish(hq, group, tq, m_sc, acc_sc, ot_sc, o_ref, sink_ref=None, lset_sc=None, lse_ref=None):
    for h in range(hq):
        kv, g = divmod(h, group)
        cols = slice(g * tq, (g + 1) * tq)
        num, l, m = acc_sc[kv, 0:HD, cols], acc_sc[kv, HD:HD + 1, cols], m_sc[kv, :, cols]
        if sink_ref is not None:
            sk = sink_ref[:, h:h + 1] * LOG2E
            m_f = jnp.maximum(m, sk)
            w_old = jnp.exp2(m - m_f)
            l = l * w_old + jnp.exp2(sk - m_f)
            num = num * w_old
            m = m_f
        ot_sc[h * HD:(h + 1) * HD, :] = num / l
        if lset_sc is not None:
            lset_sc[h * HD:(h + 1) * HD, :] = jnp.broadcast_to(m * LN2 + jnp.log(l), num.shape)
    o_ref[0] = ot_sc[...].T
    if lset_sc is not None:
        lse_ref[0] = lset_sc[...].T


def _q_group(q_ref, kv, group):
    heads = [(q_ref[0, :, h * HD:(h + 1) * HD] * (SCALE * LOG2E)).astype(BF16)
             for h in range(kv * group, (kv + 1) * group)]
    return heads[0] if group == 1 else jnp.concatenate(heads, axis=0)


def _band_kernel(*refs, t, nb, n_back, dil, hq, group, with_sink, with_lse):
    refs = list(refs)
    q_ref, k_ref, v_ref = refs[:3]
    pos = 3
    sink_ref = None
    if with_sink:
        sink_ref = refs[pos]
        pos += 1
    o_ref = refs[pos]
    pos += 1
    lse_ref = None
    if with_lse:
        lse_ref = refs[pos]
        pos += 1
    m_sc, acc_sc, ot_sc = refs[pos:pos + 3]
    lset_sc = refs[pos + 3] if with_lse else None
    qi, j = pl.program_id(1), pl.program_id(2)
    kt = qi - nb + j

    @pl.when(j == 0)
    def _():
        _attn_init(m_sc, acc_sc)

    @pl.when(kt >= 0)
    def _():
        dist = (qi * t + lax.broadcasted_iota(jnp.int32, (t, t), 1)) - (kt * t + lax.broadcasted_iota(jnp.int32, (t, t), 0))
        band = jnp.where(dist >= 0, dist, n_back + 1)
        if dil > 1:
            band = jnp.where((dist & (dil - 1)) == 0, band, n_back + 1)
        addends = [[jnp.where(band <= n_back, 0.0, NEG)]] * group
        vts = _value_rows(v_ref)
        work = [(k_ref[0, :, kv * HD:(kv + 1) * HD].astype(BF16), vts[kv], _q_group(q_ref, kv, group), addends)
                for kv in range(hq // group)]
        _attn_steps(range(hq // group), work, t, m_sc, acc_sc)

    @pl.when(j == nb)
    def _():
        _attn_finish(hq, group, t, m_sc, acc_sc, ot_sc, o_ref, sink_ref, lset_sc, lse_ref)


def band_attention(q, kv, n_back, dil, t, sink=None, with_lse=False):
    b, l, cq = q.shape
    ck = kv.shape[2] // 2
    nb = -(-n_back // t)
    hq, hkv = cq // HD, ck // HD
    kmap = lambda blk: (lambda bi, qi, j: (bi, jnp.maximum(qi - nb + j, 0), blk))
    in_specs = [pl.BlockSpec((1, t, cq), lambda bi, qi, j: (bi, qi, 0)),
                pl.BlockSpec((1, t, ck), kmap(0)), pl.BlockSpec((1, t, ck), kmap(1))]
    args = [q, kv, kv]
    if sink is not None:
        in_specs.append(pl.BlockSpec((1, hq), lambda bi, qi, j: (0, 0)))
        args.append(sink.reshape(1, hq))
    o_spec = pl.BlockSpec((1, t, cq), lambda bi, qi, j: (bi, qi, 0))
    o_shape = jax.ShapeDtypeStruct(q.shape, F32)
    wide = hq // hkv * t
    scratch = [pltpu.VMEM((hkv, 1, wide), F32), pltpu.VMEM((hkv, HD + ONES_ROWS, wide), F32), pltpu.VMEM((cq, t), F32)]
    if with_lse:
        scratch.append(pltpu.VMEM((cq, t), F32))
    return pl.pallas_call(
        functools.partial(_band_kernel, t=t, nb=nb, n_back=n_back, dil=dil, hq=hq, group=hq // hkv,
                          with_sink=sink is not None, with_lse=with_lse),
        grid=(b, l // t, nb + 1),
        in_specs=in_specs,
        out_specs=[o_spec, o_spec] if with_lse else o_spec,
        out_shape=[o_shape, o_shape] if with_lse else o_shape,
        scratch_shapes=scratch,
        compiler_params=_params(("parallel", "parallel", "arbitrary")), name="band_attention",
    )(*args)


def _dilated_kernel(q_ref, k_ref, v_ref, o_ref, lse_ref, m_sc, acc_sc, *, dil, sub, n_back):
    qi, j = pl.program_id(2), pl.program_id(3)
    kt = qi - 1 + j

    @pl.when(j == 0)
    def _():
        _attn_init(m_sc, acc_sc)

    @pl.when(kt >= 0)
    def _():
        dist = (qi * sub + lax.broadcasted_iota(jnp.int32, (sub, sub), 1)) - (kt * sub + lax.broadcasted_iota(jnp.int32, (sub, sub), 0))
        addends = [[jnp.where(jnp.where(dist >= 0, dist, n_back + 1) <= n_back, 0.0, NEG)]]
        ones = jnp.ones((ONES_ROWS, sub), BF16)
        work = []
        for r in range(dil):
            rows = pl.ds(r, sub, stride=dil)
            q = (q_ref[0, rows, :] * (SCALE * LOG2E)).astype(BF16)
            k = k_ref[0, rows, :].astype(BF16)
            vt = v_ref[0, rows, :].T.astype(BF16)
            for h in range(LANE // HD):
                cols = slice(h * HD, (h + 1) * HD)
                work.append((k[:, cols], jnp.concatenate([vt[cols], ones], axis=0), q[:, cols], addends))
        _attn_steps(range(dil * (LANE // HD)), work, sub, m_sc, acc_sc)

    @pl.when(j == 1)
    def _():
        for r in range(dil):
            outs, lses = [], []
            for h in range(LANE // HD):
                slot = r * (LANE // HD) + h
                num, l = acc_sc[slot, 0:HD], acc_sc[slot, HD:HD + 1]
                outs.append(num / l)
                lses.append(jnp.broadcast_to(m_sc[slot] * LN2 + jnp.log(l), num.shape))
            rows = pl.ds(r, sub, stride=dil)
            o_ref[0, rows, :] = jnp.concatenate(outs, axis=0).T
            lse_ref[0, rows, :] = jnp.concatenate(lses, axis=0).T


def dilated_attention(q, kv, window, dil):
    b, l, cq = q.shape
    sub = window // dil
    n_pair = cq // LANE
    kmap = lambda off: (lambda bi, c, qi, j: (bi, jnp.maximum(qi - 1 + j, 0), off + c))
    spec = pl.BlockSpec((1, window, LANE), lambda bi, c, qi, j: (bi, qi, c))
    slots = dil * (LANE // HD)
    return pl.pallas_call(
        functools.partial(_dilated_kernel, dil=dil, sub=sub, n_back=sub),
        grid=(b, n_pair, l // window, 2),
        in_specs=[spec, pl.BlockSpec((1, window, LANE), kmap(0)), pl.BlockSpec((1, window, LANE), kmap(n_pair))],
        out_specs=[spec, spec], out_shape=[jax.ShapeDtypeStruct(q.shape, F32)] * 2,
        scratch_shapes=[pltpu.VMEM((slots, 1, sub), F32), pltpu.VMEM((slots, HD + ONES_ROWS, sub), F32)],
        compiler_params=_params(("parallel", "parallel", "parallel", "arbitrary")), name="dilated_attention",
    )(q, kv, kv)


def _flash_kernel(qt_ref, kt_ref, *refs, t, hq, group, mode):
    if mode == "fox":
        q_ref, k_ref, v_ref, cq_ref, ck_ref, o_ref, m_sc, acc_sc, ot_sc = refs
    else:
        q_ref, k_ref, v_ref, sel_ref, o_ref, m_sc, acc_sc, ot_sc = refs
    step = pl.program_id(2)
    qi, kj = qt_ref[step], kt_ref[step]

    @pl.when(kj == 0)
    def _():
        _attn_init(m_sc, acc_sc)

    def body(diagonal):
        vts = _value_rows(v_ref)
        causal = lax.broadcasted_iota(jnp.int32, (t, t), 1) >= lax.broadcasted_iota(jnp.int32, (t, t), 0)
        if mode == "sel":
            selb = sel_ref[0].astype(BF16)
            width = selb.shape[1]
            n_blk = width // (hq // group)
            key_blk = (kj * t + lax.broadcasted_iota(jnp.int32, (t, width), 0)) // SEL_BLOCK
            col = lax.broadcasted_iota(jnp.int32, (t, width), 1)
        work = []
        for kv in range(hq // group):
            k = k_ref[0, :, kv * HD:(kv + 1) * HD].astype(BF16)
            if mode == "fox":
                tail = [jnp.where(causal, 0.0, NEG)] if diagonal else []
                addends = [[cq_ref[0, 0, h:h + 1, :] * LOG2E, ck_ref[0, 0, :, h:h + 1] * (-LOG2E)] + tail
                           for h in range(kv * group, (kv + 1) * group)]
            else:
                expand = jnp.where(col == kv * n_blk + key_blk, 1.0, 0.0).astype(BF16)
                flag = _nt(expand, selb)
                if diagonal:
                    flag = jnp.where(causal, flag, 0.0)
                addends = [[jnp.where(flag > 0.5, 0.0, NEG)]] * group
            work.append((k, vts[kv], _q_group(q_ref, kv, group), addends))
        _attn_steps(range(hq // group), work, t, m_sc, acc_sc)

    @pl.when(kj < qi)
    def _():
        body(False)

    @pl.when(kj == qi)
    def _():
        body(True)
        _attn_finish(hq, group, t, m_sc, acc_sc, ot_sc, o_ref)


def flash_attention(q, kv, t, cq, n_col, mode, extra, kv_per_col=2):
    b, l, _ = q.shape
    hq = cq // HD
    nq = l // t
    pairs = [(qi, kj) for qi in range(nq) for kj in range(qi + 1)]
    qt = jnp.asarray([p[0] for p in pairs], jnp.int32)
    kt = jnp.asarray([p[1] for p in pairs], jnp.int32)
    kmap = lambda off: (lambda bi, c, s, qt, kt: (bi, kt[s], off + c))
    in_specs = [pl.BlockSpec((1, t, cq), lambda bi, c, s, qt, kt: (bi, qt[s], c)),
                pl.BlockSpec((1, t, kv_per_col * HD), kmap(0)), pl.BlockSpec((1, t, kv_per_col * HD), kmap(n_col))]
    if mode == "fox":
        in_specs += [pl.BlockSpec((1, 1, hq, t), lambda bi, c, s, qt, kt: (bi, c, 0, qt[s])),
                     pl.BlockSpec((1, 1, t, hq), lambda bi, c, s, qt, kt: (bi, c, kt[s], 0))]
    else:
        in_specs += [pl.BlockSpec((1, t, extra[0].shape[2]), lambda bi, c, s, qt, kt: (bi, qt[s], 0))]
    return pl.pallas_call(
        functools.partial(_flash_kernel, t=t, hq=hq, group=hq // kv_per_col, mode=mode),
        grid_spec=pltpu.PrefetchScalarGridSpec(
            num_scalar_prefetch=2, grid=(b, n_col, len(pairs)), in_specs=in_specs,
            out_specs=pl.BlockSpec((1, t, cq), lambda bi, c, s, qt, kt: (bi, qt[s], c)),
            scratch_shapes=[pltpu.VMEM((kv_per_col, 1, hq // kv_per_col * t), F32),
                            pltpu.VMEM((kv_per_col, HD + ONES_ROWS, hq // kv_per_col * t), F32),
                            pltpu.VMEM((cq, t), F32)]),
        out_shape=jax.ShapeDtypeStruct(q.shape, F32),
        compiler_params=_params(("parallel", "parallel", "arbitrary")), name="flash_" + mode,
    )(qt, kt, q, kv, kv, *extra)


def _compress_regroup(x_sc, j, rows_at, n_rows, r0, total):
    for s in range(0, CMP_STRIDE, 2):
        a, b = rows_at(s), rows_at(s + 1)
        low = lax.broadcasted_iota(jnp.int32, a.shape, 1) < HD
        cols = slice(s * HD, (s + 2) * HD)
        x_sc[j, pl.ds(r0, n_rows), cols] = jnp.where(low, a, pltpu.roll(b, HD, axis=1)).astype(BF16)
        x_sc[j, pl.ds(total + r0, n_rows), cols] = jnp.where(low, pltpu.roll(a, HD, axis=1), b).astype(BF16)


def _compress_mlp(x_sc, w1_ref, b1_ref, w2_ref, out_ref, total):
    for j in range(2):
        parts = _dot(x_sc[j], w1_ref[j])
        half = parts.shape[1] // 2
        pre = b1_ref[j] + parts[:, :half] + pltpu.roll(parts[:, half:], 2 * total - 1, axis=0)
        act = pre * jax.nn.sigmoid(pre)
        res = _dot(act.astype(BF16), w2_ref[j])
        for kv in range(2):
            out_ref[0, :, j * LANE + kv * HD:j * LANE + (kv + 1) * HD] = res[kv * total:(kv + 1) * total]


def _compress_rows_kernel(k_ref, v_ref, w1_ref, b1_ref, w2_ref, out_ref, x_sc, *, chunks):
    for j, src in enumerate((k_ref, v_ref)):
        _compress_regroup(x_sc, j, lambda s, src=src: src[0, pl.ds(s, chunks, stride=CMP_STRIDE), :], chunks, 0, chunks)
    _compress_mlp(x_sc, w1_ref, b1_ref, w2_ref, out_ref, chunks)


def _compress_paged_kernel(pt_ref, pool_ref, perm_ref, w1_ref, b1_ref, w2_ref, out_ref, x_sc, buf, sem, *, n_src, chunks):
    bi = pl.program_id(0)
    slot = lax.rem(bi, 2)
    total = n_src * chunks
    fold = 2 * SUBLANE // chunks

    def page_copies(seq_i, into):
        return [pltpu.make_async_copy(pool_ref.at[pt_ref[seq_i, p]], buf.at[into, p], sem.at[into]) for p in range(n_src)]

    @pl.when(bi == 0)
    def _():
        for cp in page_copies(bi, slot):
            cp.start()

    @pl.when(bi + 1 < pl.num_programs(0))
    def _():
        for cp in page_copies(bi + 1, 1 - slot):
            cp.start()

    for cp in page_copies(bi, slot):
        cp.wait()

    def group(i2, carry):
        perm = perm_ref[...]
        for j in range(2):
            sorted_tokens = [_nt(perm, buf[slot, i2 * fold + f, j * LANE:(j + 1) * LANE, :].astype(BF16))
                             for f in range(fold)]
            rows_at = lambda s, toks=sorted_tokens: jnp.concatenate(
                [tok[s * chunks:(s + 1) * chunks] for tok in toks], axis=0)
            _compress_regroup(x_sc, j, rows_at, fold * chunks, pl.multiple_of(i2 * (fold * chunks), fold * chunks), total)
        return carry

    lax.fori_loop(0, n_src // fold, group, 0, unroll=8)
    _compress_mlp(x_sc, w1_ref, b1_ref, w2_ref, out_ref, total)


def compress(x, page_table, w1p, b1p, w2p):
    const = lambda nd: (lambda *_: (0,) * nd)
    w_specs = [pl.BlockSpec(w1p.shape, const(3)), pl.BlockSpec(b1p.shape, const(3)), pl.BlockSpec(w2p.shape, const(3))]
    if page_table is None:
        b, l, width = x.shape
        n_src, chunks = 1, l // CMP_STRIDE
        scratch = [pltpu.VMEM((2, 2 * chunks, CMP_STRIDE * HD), BF16)]
        halves = [pl.BlockSpec((1, l, LANE), lambda i: (i, 0, 0)), pl.BlockSpec((1, l, LANE), lambda i: (i, 0, 1))]
        grid_spec = pl.GridSpec(
            grid=(b,), in_specs=halves + w_specs,
            out_specs=pl.BlockSpec((1, chunks, width), lambda i: (i, 0, 0)), scratch_shapes=scratch)
        args = (x, x, w1p, b1p, w2p)
        body = functools.partial(_compress_rows_kernel, chunks=chunks)
    else:
        b, n_src = page_table.shape
        _, width, rows = x.shape
        chunks = rows // CMP_STRIDE
        fold = 2 * SUBLANE // chunks
        scratch = [pltpu.VMEM((2, 2 * n_src * chunks, CMP_STRIDE * HD), BF16),
                   pltpu.VMEM((2, n_src, width, rows), F32), pltpu.SemaphoreType.DMA((2,))]
        tok = np.arange(rows)
        perm = jnp.asarray(tok[None, :] == ((tok % chunks) * CMP_STRIDE + tok // chunks)[:, None], dtype=BF16)
        grid_spec = pltpu.PrefetchScalarGridSpec(
            num_scalar_prefetch=1, grid=(b,),
            in_specs=[pl.BlockSpec(memory_space=pl.ANY), pl.BlockSpec((rows, rows), lambda i, pt: (0, 0))] + w_specs,
            out_specs=pl.BlockSpec((1, n_src * chunks, width), lambda i, pt: (i, 0, 0)), scratch_shapes=scratch)
        args = (page_table, x, perm, w1p, b1p, w2p)
        body = functools.partial(_compress_paged_kernel, n_src=n_src, chunks=chunks)
    return pl.pallas_call(
        body,
        grid_spec=grid_spec,
        out_shape=jax.ShapeDtypeStruct((b, n_src * chunks, width), F32),
        compiler_params=_params(("arbitrary",)), name="compress",
    )(*args)


def _compress_weights(phi_w1, phi_b1, phi_w2):
    hidden = phi_w1.shape[-1]
    parts = CMP_BLOCK // CMP_STRIDE
    w1p = phi_w1.reshape(2, parts, CMP_STRIDE * HD, hidden).transpose(0, 2, 1, 3).reshape(2, CMP_STRIDE * HD, parts * hidden)
    return w1p.astype(BF16), phi_b1[:, None, :], phi_w2.astype(BF16)


def _top_blocks(imp, n_take):
    lane = lax.broadcasted_iota(jnp.int32, imp.shape, 1).astype(F32)
    sel = jnp.zeros(imp.shape, F32)
    cur = imp
    for _ in range(n_take):
        mx = jnp.max(cur, axis=1, keepdims=True)
        first = jnp.min(jnp.where(cur == mx, lane, 1e9), axis=1, keepdims=True)
        pick = lane == first
        sel = jnp.where(pick, 1.0, sel)
        cur = jnp.where(pick, DEAD, cur)
    return sel


def _cmp_prompt_kernel(q_ref, kvc_ref, cover_ref, o_ref, sel_ref, *, t, n_cmp, n_kv, group, n_blk):
    qi = pl.program_id(1)
    rows = kvc_ref.shape[1]
    qpos = qi * t + lax.broadcasted_iota(jnp.int32, (t, 1), 0)
    ci = lax.broadcasted_iota(jnp.int32, (1, rows), 1)
    valid = (ci * CMP_STRIDE + CMP_BLOCK - 1 <= qpos) & (ci < n_cmp)
    blk = lax.broadcasted_iota(jnp.int32, (1, n_blk), 1)
    cur = qpos // SEL_BLOCK
    forced = (blk == 0) | (blk == cur) | (blk == cur - 1)
    cover = cover_ref[...]
    for kv in range(n_kv):
        kc = kvc_ref[0, :, kv * HD:(kv + 1) * HD].astype(BF16)
        vc = kvc_ref[0, :, (n_kv + kv) * HD:(n_kv + kv + 1) * HD].astype(BF16)
        imp = jnp.zeros((t, n_blk), F32)
        for g in range(group):
            h = kv * group + g
            s = jnp.where(valid, _nt(q_ref[0, :, h * HD:(h + 1) * HD].astype(BF16), kc) * SCALE, NEG)
            m = jnp.max(s, axis=1, keepdims=True)
            e = jnp.where(valid, jnp.exp(s - m), 0.0)
            den = jnp.sum(e, axis=1, keepdims=True)
            p = (e / jnp.where(den > 0, den, 1.0)).astype(BF16)
            o_ref[0, :, h * HD:(h + 1) * HD] = _dot(p, vc)
            imp = imp + _dot(p, cover)
        imp = jnp.where(blk * SEL_BLOCK <= qpos, jnp.where(forced, FORCE_SCORE, imp), NEG)
        sel_ref[0, :, kv * n_blk:(kv + 1) * n_blk] = _top_blocks(imp, min(N_SEL, n_blk))


def _cover_matrix(rows, n_cmp, n_blk, width):
    c_start = np.arange(rows)[:, None] * CMP_STRIDE
    s_start = np.arange(width)[None, :] * SEL_BLOCK
    cov = (c_start < s_start + SEL_BLOCK) & (c_start + CMP_BLOCK > s_start)
    cov &= (np.arange(rows)[:, None] < n_cmp) & (np.arange(width)[None, :] < n_blk)
    return jnp.asarray(cov, dtype=BF16)


def cmp_attend_prompt(q, kvc, t, n_kv):
    b, l, cq = q.shape
    rows = kvc.shape[1]
    n_blk = l // SEL_BLOCK
    cover = _cover_matrix(rows, rows - 1, n_blk, n_blk)
    return pl.pallas_call(
        functools.partial(_cmp_prompt_kernel, t=t, n_cmp=rows - 1, n_kv=n_kv, group=cq // HD // n_kv, n_blk=n_blk),
        grid=(b, l // t),
        in_specs=[pl.BlockSpec((1, t, cq), lambda bi, qi: (bi, qi, 0)),
                  pl.BlockSpec((1, rows, kvc.shape[2]), lambda bi, qi: (bi, 0, 0)),
                  pl.BlockSpec(cover.shape, lambda bi, qi: (0, 0))],
        out_specs=[pl.BlockSpec((1, t, cq), lambda bi, qi: (bi, qi, 0)),
                   pl.BlockSpec((1, t, n_kv * n_blk), lambda bi, qi: (bi, qi, 0))],
        out_shape=[jax.ShapeDtypeStruct(q.shape, F32), jax.ShapeDtypeStruct((b, l, n_kv * n_blk), F32)],
        compiler_params=_params(("parallel", "parallel")), name="cmp_attend_prompt",
    )(q, kvc, cover)


def _diag_heads(x, rows_kv, n_kv):
    out = jnp.zeros((x.shape[0], HD), F32)
    for kv in range(n_kv):
        out = out + jnp.where(rows_kv == kv, x[:, kv * HD:(kv + 1) * HD], 0.0)
    return out


def _cmp_sample_kernel(q_ref, kvc_ref, cover_ref, o_ref, sel_ref, *, n_tok, heads, n_kv, n_cmp, past_len, n_blk):
    rows = kvc_ref.shape[1]
    r_all = n_tok * heads
    c = n_kv * HD
    width = cover_ref.shape[1]
    group = heads // n_kv
    ridx = lax.broadcasted_iota(jnp.int32, (r_all, 1), 0)
    qpos = past_len + ridx // heads
    ci = lax.broadcasted_iota(jnp.int32, (1, rows), 1)
    valid = (ci * CMP_STRIDE + CMP_BLOCK - 1 <= qpos) & (ci < n_cmp)
    g_rows = r_all // group
    n_seq = q_ref.shape[0]
    imps = []
    for i in range(n_seq):
        kc = kvc_ref[i, :, 0:c].astype(BF16)
        vc = kvc_ref[i, :, c:2 * c].astype(BF16)
        s = jnp.where(valid, _nt(q_ref[i].astype(BF16), kc) * SCALE, NEG)
        m = jnp.max(s, axis=1, keepdims=True)
        e = jnp.where(valid, jnp.exp(s - m), 0.0)
        den = jnp.sum(e, axis=1, keepdims=True)
        p = (e / jnp.where(den > 0, den, 1.0)).astype(BF16)
        o_ref[i] = _diag_heads(_dot(p, vc), (ridx % heads) // group, n_kv)
        imps.append(jnp.sum(_dot(p, cover_ref[...]).reshape(g_rows, group, width), axis=1))
    imp = jnp.concatenate(imps, axis=0)
    gpos = past_len + (lax.broadcasted_iota(jnp.int32, (n_seq * g_rows, 1), 0) % g_rows) // n_kv
    blk = lax.broadcasted_iota(jnp.int32, (1, width), 1)
    cur = gpos // SEL_BLOCK
    forced = (blk == 0) | (blk == cur) | (blk == cur - 1)
    imp = jnp.where(blk * SEL_BLOCK <= gpos, jnp.where(forced, FORCE_SCORE, imp), NEG)
    imp = jnp.where(blk < n_blk, imp, DEAD)
    sel = _top_blocks(imp, min(N_SEL, n_blk))
    for i in range(n_seq):
        sel_ref[i] = sel[i * g_rows:(i + 1) * g_rows]


def cmp_attend_sample(qbd, kvc, n_tok, heads, n_kv, past_len):
    b, r_all, c = qbd.shape
    rows = kvc.shape[1]
    n_blk = past_len // SEL_BLOCK + -(-n_tok // SEL_BLOCK)
    width = -(-n_blk // LANE) * LANE
    cover = _cover_matrix(rows, rows - 1, n_blk, width)
    g_rows = r_all // (heads // n_kv)
    n_seq = next(n for n in (8, 4, 2, 1) if b % n == 0)
    return pl.pallas_call(
        functools.partial(_cmp_sample_kernel, n_tok=n_tok, heads=heads, n_kv=n_kv, n_cmp=rows - 1,
                          past_len=past_len, n_blk=n_blk),
        grid=(b // n_seq,),
        in_specs=[pl.BlockSpec((n_seq, r_all, c), lambda i: (i, 0, 0)),
                  pl.BlockSpec((n_seq, rows, 2 * c), lambda i: (i, 0, 0)),
                  pl.BlockSpec(cover.shape, lambda i: (0, 0))],
        out_specs=[pl.BlockSpec((n_seq, r_all, HD), lambda i: (i, 0, 0)),
                   pl.BlockSpec((n_seq, g_rows, width), lambda i: (i, 0, 0))],
        out_shape=[jax.ShapeDtypeStruct((b, r_all, HD), F32), jax.ShapeDtypeStruct((b, g_rows, width), F32)],
        compiler_params=_params(("parallel",)), name="cmp_attend_sample",
    )(qbd, kvc, cover)


def _dec_init(m_sc, l_sc, acc_sc):
    m_sc[...] = jnp.full(m_sc.shape, NEG, F32)
    l_sc[...] = jnp.zeros(l_sc.shape, F32)
    acc_sc[...] = jnp.zeros(acc_sc.shape, F32)


def _dec_update(qb, kt, vt, bias, mask, m_sc, l_sc, acc_sc):
    s = _dot(qb, kt) * SCALE
    if bias is not None:
        s = s + bias
    if mask is not None:
        s = jnp.where(mask, s, NEG)
    m_prev = m_sc[...]
    m_new = jnp.maximum(m_prev, jnp.max(s, axis=1, keepdims=True))
    alpha = jnp.exp(m_prev - m_new)
    p = jnp.exp(s - m_new)
    if mask is not None:
        p = jnp.where(mask, p, 0.0)
    l_sc[...] = alpha * l_sc[...] + jnp.sum(p, axis=1, keepdims=True)
    acc_sc[...] = alpha * acc_sc[...] + _nt(p.astype(BF16), vt)
    m_sc[...] = m_new


def _dec_new_tokens(q, knew, vnew, biases, masks, m_sc, l_sc, acc_sc):
    n_new = knew.shape[0]
    scores = []
    for u in range(n_new):
        s = jnp.sum(q * knew[u:u + 1, :], axis=1, keepdims=True) * SCALE
        if biases is not None:
            s = s + biases[u]
        scores.append(jnp.where(masks[u], s, NEG))
    m_prev = m_sc[...]
    m_new = m_prev
    for s in scores:
        m_new = jnp.maximum(m_new, s)
    alpha = jnp.exp(m_prev - m_new)
    l = alpha * l_sc[...]
    acc = alpha * acc_sc[...]
    for u in range(n_new):
        p = jnp.where(masks[u], jnp.exp(scores[u] - m_new), 0.0)
        l = l + p
        acc = acc + p * vnew[u:u + 1, :]
    l_sc[...] = l
    acc_sc[...] = acc
    m_sc[...] = m_new


def _token_cumsum(x, n_tok, heads):
    parts, run = [], None
    for u in range(n_tok):
        seg = x[u * heads:(u + 1) * heads]
        run = seg if run is None else run + seg
        parts.append(run)
    return jnp.concatenate(parts, axis=0), parts


def _dec_window_kernel(*refs, n_tok, heads, n_kv, lb, n_back, dil, chunk, with_sink, with_lse):
    refs = list(refs)
    q_ref, cache_ref, new_ref = refs[:3]
    pos = 3
    sink_ref = None
    if with_sink:
        sink_ref = refs[pos]
        pos += 1
    o_ref = refs[pos]
    pos += 1
    lse_ref = None
    if with_lse:
        lse_ref = refs[pos]
        pos += 1
    m_sc, l_sc, acc_sc = refs[pos:pos + 3]
    r_all = n_tok * heads
    c = n_kv * HD
    group = heads // n_kv
    ridx = lax.broadcasted_iota(jnp.int32, (r_all, 1), 0)
    tok = ridx // heads
    _dec_init(m_sc, l_sc, acc_sc)
    q = q_ref[0]
    qb = q.astype(BF16)

    def ok(dist):
        good = (dist >= 0) & (dist <= n_back)
        if dil > 1:
            good = good & (dist % dil == 0)
        return good

    for c0 in range(0, lb, chunk):
        kt = cache_ref[0, 0:c, c0:c0 + chunk].astype(BF16)
        vt = cache_ref[0, c:2 * c, c0:c0 + chunk].astype(BF16)
        dist = (lb + tok) - (c0 + lax.broadcasted_iota(jnp.int32, (1, chunk), 1))
        _dec_update(qb, kt, vt, None, ok(dist), m_sc, l_sc, acc_sc)
    new = new_ref[0]
    _dec_new_tokens(q, new[:, 0:c], new[:, c:2 * c], None, [ok(tok - u) for u in range(n_tok)], m_sc, l_sc, acc_sc)
    m, l, acc = m_sc[...], l_sc[...], acc_sc[...]
    if with_sink:
        sk = sink_ref[...]
        m_f = jnp.maximum(m, sk)
        scale_old = jnp.exp(m - m_f)
        l = l * scale_old + jnp.exp(sk - m_f)
        acc = acc * scale_old
        m = m_f
    o_ref[0] = _diag_heads(acc / l, (ridx % heads) // group, n_kv)
    if with_lse:
        lse_ref[0] = jnp.broadcast_to(m + jnp.log(l), (r_all, HD))


def dec_window(qbd, cache_t, new, n_tok, heads, n_kv, n_back, dil, sink_rows=None, with_lse=False):
    b, r_all, c = qbd.shape
    lb = cache_t.shape[2]
    in_specs = [pl.BlockSpec((1, r_all, c), lambda i: (i, 0, 0)),
                pl.BlockSpec((1, 2 * c, lb), lambda i: (i, 0, 0)),
                pl.BlockSpec((1, n_tok, 2 * c), lambda i: (i, 0, 0))]
    args = [qbd, cache_t, new]
    if sink_rows is not None:
        in_specs.append(pl.BlockSpec((r_all, 1), lambda i: (0, 0)))
        args.append(sink_rows)
    o_spec = pl.BlockSpec((1, r_all, HD), lambda i: (i, 0, 0))
    o_shape = jax.ShapeDtypeStruct((b, r_all, HD), F32)
    return pl.pallas_call(
        functools.partial(_dec_window_kernel, n_tok=n_tok, heads=heads, n_kv=n_kv, lb=lb, n_back=n_back, dil=dil,
                          chunk=min(lb, 512), with_sink=sink_rows is not None, with_lse=with_lse),
        grid=(b,), in_specs=in_specs,
        out_specs=[o_spec, o_spec] if with_lse else o_spec,
        out_shape=[o_shape, o_shape] if with_lse else o_shape,
        scratch_shapes=[pltpu.VMEM((r_all, 1), F32), pltpu.VMEM((r_all, 1), F32), pltpu.VMEM((r_all, c), F32)],
        compiler_params=_params(("parallel",)), name="dec_window",
    )(*args)


def _dec_paged_kernel(*refs, n_pg, n_tok, heads, n_kv, mode):
    pt_ref, pool_ref = refs[:2]
    pos = 2
    lf_ref = None
    if mode == "fox":
        lf_ref = refs[pos]
        pos += 1
    q_ref, new_ref, x_ref = refs[pos:pos + 3]
    pos += 3
    if mode == "fox":
        tri_ref = refs[pos]
        pos += 1
    o_ref = refs[pos]
    pos += 1
    buf, sem, k_sc, v_sc, m_sc, l_sc, acc_sc = refs[pos:pos + 7]
    pos += 7
    if mode == "fox":
        lf_buf, lf_sem, c_sc, carry_sc = refs[pos:pos + 4]
    bi, gi = pl.program_id(0), pl.program_id(1)
    n_grp = pl.num_programs(1)
    step = bi * n_grp + gi
    slot = lax.rem(step, 2)

    def page_copies(seq_i, grp_i, into):
        copies = []
        for p in range(n_pg):
            page = pt_ref[seq_i, grp_i * n_pg + p]
            copies.append(pltpu.make_async_copy(pool_ref.at[page], buf.at[into, p], sem.at[into]))
            if mode == "fox":
                copies.append(pltpu.make_async_copy(lf_ref.at[page], lf_buf.at[into, p], lf_sem.at[into]))
        return copies

    @pl.when(step == 0)
    def _():
        for cp in page_copies(bi, gi, slot):
            cp.start()

    @pl.when(step + 1 < pl.num_programs(0) * n_grp)
    def _():
        nxt = step + 1
        for cp in page_copies(nxt // n_grp, lax.rem(nxt, n_grp), 1 - slot):
            cp.start()

    for cp in page_copies(bi, gi, slot):
        cp.wait()
    pages = [buf.at[slot, p] for p in range(n_pg)]
    lf_pages = [lf_buf.at[slot, p] for p in range(n_pg)] if mode == "fox" else None
    r_all = n_tok * heads
    c = n_kv * HD
    group = heads // n_kv
    tk = n_pg * PAGE
    ridx = lax.broadcasted_iota(jnp.int32, (r_all, 1), 0)
    tok = ridx // heads
    head = ridx % heads

    @pl.when(gi == 0)
    def _():
        _dec_init(m_sc, l_sc, acc_sc)
        if mode == "fox":
            carry_sc[...] = jnp.zeros(carry_sc.shape, F32)

    for i in range(n_pg):
        k_sc[:, i * PAGE:(i + 1) * PAGE] = pages[i][0:c, :].astype(BF16)
        v_sc[:, i * PAGE:(i + 1) * PAGE] = pages[i][c:2 * c, :].astype(BF16)
    q = q_ref[0]
    qb = q.astype(BF16)

    if mode == "fox":
        tri = tri_ref[...]
        hi, mid, lo = _split3(jnp.concatenate([lf_pages[i][...] for i in range(n_pg)], axis=0))
        within = _dot(hi, tri) + _dot(mid, tri) + _dot(lo, tri)
        carry = carry_sc[...]
        for i in range(n_pg):
            page = within[i * heads:(i + 1) * heads]
            c_sc[:, i * PAGE:(i + 1) * PAGE] = page + carry
            carry = carry + page[:, PAGE - 1:PAGE]
        carry_sc[...] = carry
        bias = _token_cumsum(x_ref[0], n_tok, heads)[0] - jnp.concatenate([c_sc[...]] * n_tok, axis=0)
        _dec_update(qb, k_sc[...], v_sc[...], bias, None, m_sc, l_sc, acc_sc)
    else:
        flags = x_ref[0].astype(BF16)
        width = flags.shape[1]
        key_blk = (gi * tk + lax.broadcasted_iota(jnp.int32, (width, tk), 1)) // SEL_BLOCK
        expand = jnp.where(lax.broadcasted_iota(jnp.int32, (width, tk), 0) == key_blk, 1.0, 0.0).astype(BF16)
        _dec_update(qb, k_sc[...], v_sc[...], None, _dot(flags, expand) > 0.5, m_sc, l_sc, acc_sc)

    @pl.when(gi == pl.num_programs(1) - 1)
    def _():
        new = new_ref[0]
        causal = [tok >= u for u in range(n_tok)]
        if mode == "fox":
            fq, per_tok = _token_cumsum(x_ref[0], n_tok, heads)
            total = jnp.concatenate([carry_sc[...]] * n_tok, axis=0)
            biases = [fq - jnp.concatenate([per_tok[u]] * n_tok, axis=0) - total for u in range(n_tok)]
            _dec_new_tokens(q, new[:, 0:c], new[:, c:2 * c], biases, causal, m_sc, l_sc, acc_sc)
        else:
            n_past_blk = (pl.num_programs(1) * tk) // SEL_BLOCK
            flags = x_ref[0]
            lane = lax.broadcasted_iota(jnp.int32, flags.shape, 1)
            new_on = jnp.sum(jnp.where(lane == n_past_blk, flags, 0.0), axis=1, keepdims=True) > 0.5
            _dec_new_tokens(q, new[:, 0:c], new[:, c:2 * c], None, [m & new_on for m in causal], m_sc, l_sc, acc_sc)
        o_ref[0] = _diag_heads(acc_sc[...] / l_sc[...], head // group, n_kv)


def dec_paged(qbd, pool_t, page_table, new, x, n_tok, heads, n_kv, mode, n_pg, lf_pool_t=None):
    b, r_all, c = qbd.shape
    n_pages = page_table.shape[1]
    in_specs = [pl.BlockSpec(memory_space=pl.ANY)]
    args = [pool_t]
    if mode == "fox":
        in_specs.append(pl.BlockSpec(memory_space=pl.ANY))
        args.append(lf_pool_t)
    in_specs += [pl.BlockSpec((1, r_all, c), lambda i, g, pt: (i, 0, 0)),
                 pl.BlockSpec((1, n_tok, 2 * c), lambda i, g, pt: (i, 0, 0)),
                 pl.BlockSpec((1, r_all, x.shape[2]), lambda i, g, pt: (i, 0, 0))]
    args += [qbd, new, x]
    scratch = [pltpu.VMEM((2, n_pg, 2 * c, PAGE), F32), pltpu.SemaphoreType.DMA((2,)),
               pltpu.VMEM((c, n_pg * PAGE), BF16), pltpu.VMEM((c, n_pg * PAGE), BF16),
               pltpu.VMEM((r_all, 1), F32), pltpu.VMEM((r_all, 1), F32), pltpu.VMEM((r_all, c), F32)]
    if mode == "fox":
        tri = jnp.asarray(np.triu(np.ones((PAGE, PAGE), np.float32)), dtype=BF16)
        in_specs.append(pl.BlockSpec((PAGE, PAGE), lambda i, g, pt: (0, 0)))
        args.append(tri)
        scratch += [pltpu.VMEM((2, n_pg, heads, PAGE), F32), pltpu.SemaphoreType.DMA((2,)),
                    pltpu.VMEM((heads, n_pg * PAGE), F32), pltpu.VMEM((heads, 1), F32)]
    return pl.pallas_call(
        functools.partial(_dec_paged_kernel, n_pg=n_pg, n_tok=n_tok, heads=heads, n_kv=n_kv, mode=mode),
        grid_spec=pltpu.PrefetchScalarGridSpec(
            num_scalar_prefetch=1, grid=(b, n_pages // n_pg), in_specs=in_specs,
            out_specs=pl.BlockSpec((1, r_all, HD), lambda i, g, pt: (i, 0, 0)), scratch_shapes=scratch),
        out_shape=jax.ShapeDtypeStruct((b, r_all, HD), F32),
        compiler_params=_params(("arbitrary", "arbitrary")), name="dec_paged_" + mode,
    )(page_table, *args)


def _block_diag_q(q, n_tok, heads, n_kv):
    b = q.shape[0] // n_tok
    own = jnp.asarray(np.arange(heads)[:, None] // (heads // n_kv) == np.arange(n_kv)[None, :], dtype=F32)
    qb = q.reshape(b, n_tok, heads, 1, HD) * own[None, None, :, :, None]
    return qb.reshape(b, n_tok * heads, n_kv * HD)


def _rows_first(z_t, n_kv):
    b, _, l = z_t.shape
    return z_t.reshape(b, 2, n_kv, HD, l).transpose(0, 4, 1, 2, 3)


def _rows_last(cache):
    n, rows = cache.shape[:2]
    return cache.transpose(0, 2, 3, 4, 1).reshape(n, -1, rows)


def _mixer_a(hp, hs, g, w_in, w_out, phi, caches, page_table, dims, ropes):
    bsz, seq, dec_b, dec_t, past_len = dims
    (cos_p, sin_p), (cos_s, sin_s) = ropes
    cache_cmp, cache_sel, cache_win = caches
    heads, n_kv = 16, 2
    segs = [(0, 1024, 0, 1024, True), (1024, 1280, 0, 0, False), (1280, 1536, 0, 128, False),
            (1536, 1792, 0, 128, False), (1792, 1840, 0, 0, False)]
    w1p, b1p, w2p = phi
    q_cmp, q_rot, cmp_raw, sel_kv, win_kv, gates, cmp_t, sel_t, win_t = norm_proj(
        hp, g, w_in, cos_p, sin_p, segs, 512, flipped=(1, 2, 3), seq=seq)
    n = bsz * seq
    kvc = compress(cmp_raw.reshape(bsz, seq, -1), None, w1p, b1p, w2p)
    o_cmp, flags = cmp_attend_prompt(q_cmp.reshape(bsz, seq, -1), kvc, 512, n_kv)
    q3 = q_rot.reshape(bsz, seq, -1)
    o_sel = flash_attention(q3, sel_kv.reshape(bsz, seq, -1), 256, 1024, 1, "sel", (flags,))
    o_win = band_attention(q3, win_kv.reshape(bsz, seq, -1), A_WINDOW - 1, 1, 256)
    hp = merge_a(o_cmp.reshape(n, -1), o_sel.reshape(n, -1), o_win.reshape(n, -1), gates, hp, w_out)
    kv5 = lambda z, b, t: z.reshape(b, t, 2, n_kv, HD)
    w = min(A_WINDOW, seq)
    outs_p = (_rows_first(cmp_t, n_kv), _rows_first(sel_t, n_kv), _rows_first(win_t[:, :, seq - w:], n_kv))
    q_cmp, q_rot, cmp_raw, sel_kv, win_kv, gates = norm_proj(hs, g, w_in, cos_s, sin_s, segs, 512)
    ns = dec_b * dec_t
    kvc = compress(_rows_last(cache_cmp), page_table, w1p, b1p, w2p)
    o_cmp, flags = cmp_attend_sample(_block_diag_q(q_cmp, dec_t, heads, n_kv), kvc, dec_t, heads, n_kv, past_len)
    qbd = _block_diag_q(q_rot, dec_t, heads, n_kv)
    flag_rows = jnp.repeat(flags.reshape(dec_b, dec_t, n_kv, 1, -1), heads // n_kv, axis=3).reshape(dec_b, dec_t * heads, -1)
    o_sel = dec_paged(qbd, _rows_last(cache_sel), page_table, sel_kv.reshape(dec_b, dec_t, -1), flag_rows,
                      dec_t, heads, n_kv, "sel", 32)
    o_win = dec_window(qbd, _rows_last(cache_win), win_kv.reshape(dec_b, dec_t, -1), dec_t, heads, n_kv, A_WINDOW - 1, 1)
    hs = merge_a(o_cmp.reshape(ns, -1), o_sel.reshape(ns, -1), o_win.reshape(ns, -1), gates, hs, w_out)
    outs_s = (kv5(cmp_raw, dec_b, dec_t), kv5(sel_kv, dec_b, dec_t), kv5(win_kv, dec_b, dec_t))
    return hp, hs, outs_p, outs_s


def _mixer_b(hp, hs, g, w_in, sinks, w_out, cache_win, dims, ropes):
    bsz, seq, dec_b, dec_t, _ = dims
    (cos_p, sin_p), (cos_s, sin_s) = ropes
    heads, n_kv = 16, 2
    segs = [(0, 1024, 0, 1024, False), (1024, 1280, 0, 128, False)]
    q, kv, kv_t = norm_proj(hp, g, w_in, cos_p, sin_p, segs, 512, flipped=(1,), seq=seq)
    o = band_attention(q.reshape(bsz, seq, -1), kv.reshape(bsz, seq, -1), B_WINDOW - 1, 1, 256, sink=sinks)
    hp = out_proj(o.reshape(bsz * seq, -1), hp, w_out)
    w = min(B_WINDOW, seq)
    out_p = _rows_first(kv_t[:, :, seq - w:], n_kv)
    q, kv = norm_proj(hs, g, w_in, cos_s, sin_s, segs, 512)
    sink_rows = jnp.tile(sinks.astype(F32), dec_t).reshape(dec_t * heads, 1)
    o = dec_window(_block_diag_q(q, dec_t, heads, n_kv), _rows_last(cache_win), kv.reshape(dec_b, dec_t, -1),
                   dec_t, heads, n_kv, B_WINDOW - 1, 1, sink_rows=sink_rows)
    hs = out_proj(o.reshape(dec_b * dec_t, -1), hs, w_out)
    return hp, hs, out_p, kv.reshape(dec_b, dec_t, 2, n_kv, HD)


def _mixer_c(hp, hs, g, w_in, b_f, w_out, cache_kv, cache_logf, page_table, dims, ropes):
    bsz, seq, dec_b, dec_t, _ = dims
    (cos_p, sin_p), (cos_s, sin_s) = ropes
    heads, n_kv = 16, 4
    segs = [(0, 1024, 0, 0, False), (1024, 1536, 0, 0, False), (1536, 1552, 0, 0, False)]
    q, kv, f, kv_t = norm_proj(hp, g, w_in, cos_p, sin_p, segs, 512, flipped=(1,), seq=seq)
    lf_t, cum_t = fox_gate(f.reshape(bsz, seq, heads).transpose(0, 2, 1), b_f)
    logf_p = lf_t.transpose(0, 2, 1)
    cum_q = cum_t.reshape(bsz, 1, heads, seq)
    cum_k = cum_q.transpose(0, 1, 3, 2)
    o = flash_attention(q.reshape(bsz, seq, -1), kv.reshape(bsz, seq, -1), 512, 1024, 1, "fox", (cum_q, cum_k),
                        kv_per_col=n_kv)
    hp = out_proj(o.reshape(bsz * seq, -1), hp, w_out)
    out_p = _rows_first(kv_t, n_kv)
    q, kv, f = norm_proj(hs, g, w_in, cos_s, sin_s, segs, 512)
    ns = dec_b * dec_t
    lf_t, _ = fox_gate(f.reshape(1, ns, heads).transpose(0, 2, 1), b_f)
    logf_s = lf_t.transpose(0, 2, 1).reshape(dec_b, dec_t, heads)
    o = dec_paged(_block_diag_q(q, dec_t, heads, n_kv), _rows_last(cache_kv), page_table, kv.reshape(dec_b, dec_t, -1),
                  logf_s.reshape(dec_b, dec_t * heads, 1), dec_t, heads, n_kv, "fox", 32,
                  lf_pool_t=cache_logf.transpose(0, 2, 1))
    hs = out_proj(o.reshape(ns, -1), hs, w_out)
    return hp, hs, (out_p, logf_p), (kv.reshape(dec_b, dec_t, 2, n_kv, HD), logf_s)


def _mixer_d(hp, hs, g, w_in, w_out, caches, dims, ropes):
    bsz, seq, dec_b, dec_t, _ = dims
    (cos_p, sin_p), (cos_s, sin_s) = ropes
    heads = 8
    c = heads * HD
    segs = []
    for grp in range(len(D_GROUPS)):
        base = 3 * grp * c
        segs += [(base, base + c, 0, c, False), (base + c, base + 3 * c, 0, c, False)]
    res = norm_proj(hp, g, w_in, cos_p, sin_p, segs, 256, flipped=(1, 3, 5), seq=seq)
    outs, lses, states_p = [], [], []
    for grp, (window, dil) in enumerate(D_GROUPS):
        q, kv = res[2 * grp], res[2 * grp + 1]
        if dil == 1:
            o, lse = band_attention(q.reshape(bsz, seq, c), kv.reshape(bsz, seq, 2 * c), window, 1, 256, with_lse=True)
        else:
            o, lse = dilated_attention(q.reshape(bsz, seq, c), kv.reshape(bsz, seq, 2 * c), window, dil)
        outs.append(o.reshape(bsz * seq, c))
        lses.append(lse.reshape(bsz * seq, c))
        states_p.append(_rows_first(res[6 + grp][:, :, seq - min(window, seq):], heads))
    hp = merge_d(outs, lses, hp, w_out)
    res = norm_proj(hs, g, w_in, cos_s, sin_s, segs, 256)
    outs, lses, states_s = [], [], []
    for grp, (window, dil) in enumerate(D_GROUPS):
        q, kv = res[2 * grp], res[2 * grp + 1]
        o, lse = dec_window(_block_diag_q(q, dec_t, heads, heads), _rows_last(caches[grp]),
                            kv.reshape(dec_b, dec_t, -1), dec_t, heads, heads, window, dil, with_lse=True)
        outs.append(o.reshape(dec_b * dec_t, c))
        lses.append(lse.reshape(dec_b * dec_t, c))
        states_s.append(kv.reshape(dec_b, dec_t, 2, heads, HD))
    hs = merge_d(outs, lses, hs, w_out)
    return hp, hs, states_p, states_s


def kernel(x_prompt, x_sample, cache_a_cmp, cache_a_sel, cache_a_win, cache_b_win, cache_c_kv, cache_c_logf, cache_d1, cache_d2, cache_d3, page_table, norm_mix, norm_ffn, norm_final, a_w_in, a_phi_w1, a_phi_b1, a_phi_w2, a_w_out, b_w_in, b_sinks, b_w_out, c_w_in, c_b_f, c_w_out, d_w_in, d_w_out, ffn_w_gate, ffn_w_up, ffn_w_down):
    bsz, seq, d_model = x_prompt.shape
    dec_b, dec_t, _ = x_sample.shape
    past_len = page_table.shape[1] * PAGE
    dims = (bsz, seq, dec_b, dec_t, past_len)
    ropes = (_rope_tables(jnp.arange(seq)), _rope_tables(past_len + jnp.arange(dec_b * dec_t) % dec_t))
    hp = x_prompt.reshape(bsz * seq, d_model)
    hs = x_sample.reshape(dec_b * dec_t, d_model)
    bf = lambda z: z.astype(BF16)
    phi = _compress_weights(a_phi_w1, a_phi_b1, a_phi_w2)
    depth = norm_mix.shape[0]
    for layer in range(depth):
        kind = layer % 4
        g = norm_mix[layer]
        if kind == 0:
            hp, hs, a_p, a_s = _mixer_a(hp, hs, g, bf(a_w_in), bf(a_w_out), phi,
                                        (cache_a_cmp, cache_a_sel, cache_a_win), page_table, dims, ropes)
        elif kind == 1:
            hp, hs, b_p, b_s = _mixer_b(hp, hs, g, bf(b_w_in), b_sinks, bf(b_w_out), cache_b_win, dims, ropes)
        elif kind == 2:
            hp, hs, c_p, c_s = _mixer_c(hp, hs, g, bf(c_w_in), c_b_f, bf(c_w_out), cache_c_kv, cache_c_logf,
                                        page_table, dims, ropes)
        else:
            hp, hs, d_p, d_s = _mixer_d(hp, hs, g, bf(d_w_in), bf(d_w_out), (cache_d1, cache_d2, cache_d3), dims, ropes)
        wg, wu, wd = bf(ffn_w_gate[layer]), bf(ffn_w_up[layer]), bf(ffn_w_down[layer])
        hp = ffn(hp, norm_ffn[layer], wg, wu, wd)
        hs = ffn(hs, norm_ffn[layer], wg, wu, wd)
    y_p = final_norm(hp, norm_final).reshape(bsz, seq, d_model)
    y_s = final_norm(hs, norm_final).reshape(dec_b, dec_t, d_model)
    return (y_p, y_s, a_p[0], a_s[0], a_p[1], a_s[1], a_p[2], a_s[2], b_p, b_s,
            c_p[0], c_s[0], c_p[1], c_s[1], d_p[0], d_s[0], d_p[1], d_s[1], d_p[2], d_s[2])
```

```python
import functools

import numpy as np
import jax
import jax.numpy as jnp
from jax import lax
from jax.experimental import pallas as pl
from jax.experimental.pallas import tpu as pltpu

F32 = jnp.float32
BF16 = jnp.bfloat16

HD = 64
SCALE = HD ** -0.5
LOG2E = 1.4426950408889634
LN2 = 0.6931471805599453
RMS_EPS = 1e-6
ROPE_THETA = 10000.0
PAGE = 128
CMP_STRIDE = 16
CMP_BLOCK = 32
SEL_BLOCK = 64
N_SEL = 16
A_WINDOW = 512
B_WINDOW = 128
D_GROUPS = ((128, 1), (512, 4), (2048, 16))
FORCE_SCORE = 1e9
NEG = -1e30
DEAD = -3e38
LANE = 128
SUBLANE = 8
ONES_ROWS = 16
VMEM_LIMIT = 52 * 1024 * 1024


def _params(sem, vmem=VMEM_LIMIT):
    return pltpu.CompilerParams(dimension_semantics=sem, vmem_limit_bytes=vmem)


def _nt(a, b):
    return lax.dot_general(a, b, (((1,), (1,)), ((), ())), preferred_element_type=F32)


def _dot(a, b):
    return jnp.dot(a, b, preferred_element_type=F32)


def _split3(x):
    hi = x.astype(BF16)
    r1 = x - hi.astype(F32)
    mid = r1.astype(BF16)
    lo = (r1 - mid.astype(F32)).astype(BF16)
    return hi, mid, lo


def _rms(x, g):
    var = jnp.mean(x * x, axis=-1, keepdims=True)
    return x * lax.rsqrt(var + RMS_EPS) * g


def _rope_chunk(x, cos2, sin2):
    lane = lax.broadcasted_iota(jnp.int32, x.shape, 1)
    first = (lane % HD) < (HD // 2)
    swapped = jnp.where(first, pltpu.roll(x, LANE - HD // 2, axis=1), pltpu.roll(x, HD // 2, axis=1))
    return x * cos2 + swapped * sin2


def _rope_tables(pos):
    half = HD // 2
    inv = ROPE_THETA ** (-jnp.arange(half, dtype=F32) / half)
    ang = pos.astype(F32)[:, None] * inv[None, :]
    c, s = jnp.cos(ang), jnp.sin(ang)
    return jnp.tile(c, (1, 4)), jnp.tile(jnp.concatenate([-s, s], axis=1), (1, 2))


def _norm_proj_kernel(x_ref, g_ref, w_ref, cos_ref, sin_ref, *out_refs, segs, flipped):
    xn = _rms(x_ref[...], g_ref[...]).astype(BF16)
    outs = list(out_refs)
    written = []
    for c0, c1, r0, r1, dup in segs:
        acc = _dot(xn, w_ref[:, c0:c1])
        if dup:
            outs.pop(0)[...] = acc
        o_ref = outs.pop(0)
        written.append(o_ref)
        if r1 <= r0:
            o_ref[...] = acc
            continue
        if r0 > 0:
            o_ref[:, 0:r0] = acc[:, 0:r0]
        for a in range(r0, r1, LANE):
            o_ref[:, a:a + LANE] = _rope_chunk(acc[:, a:a + LANE], cos_ref[...], sin_ref[...])
        if r1 < c1 - c0:
            o_ref[:, r1:] = acc[:, r1:]
    for idx in flipped:
        outs.pop(0)[0] = written[idx][...].T


def norm_proj(x, g, w, cos2, sin2, segs, tm, flipped=(), seq=None):
    n, d = x.shape
    tm = min(tm, n)
    out_shape, out_specs = [], []
    for c0, c1, _, _, dup in segs:
        for _ in range(2 if dup else 1):
            out_shape.append(jax.ShapeDtypeStruct((n, c1 - c0), F32))
            out_specs.append(pl.BlockSpec((tm, c1 - c0), lambda i: (i, 0)))
    for idx in flipped:
        width = segs[idx][1] - segs[idx][0]
        per_seq = seq // tm
        out_shape.append(jax.ShapeDtypeStruct((n // seq, width, seq), F32))
        out_specs.append(pl.BlockSpec((1, width, tm), lambda i: (i // per_seq, 0, i % per_seq)))
    return pl.pallas_call(
        functools.partial(_norm_proj_kernel, segs=tuple(segs), flipped=tuple(flipped)),
        grid=(n // tm,),
        in_specs=[pl.BlockSpec((tm, d), lambda i: (i, 0)),
                  pl.BlockSpec((1, d), lambda i: (0, 0)),
                  pl.BlockSpec(w.shape, lambda i: (0, 0)),
                  pl.BlockSpec((tm, LANE), lambda i: (i % (cos2.shape[0] // tm), 0)),
                  pl.BlockSpec((tm, LANE), lambda i: (i % (sin2.shape[0] // tm), 0))],
        out_specs=out_specs, out_shape=out_shape,
        compiler_params=_params(("parallel",)), name="norm_proj",
    )(x, g.reshape(1, d), w, cos2, sin2)


def _out_proj_kernel(o_ref, h_ref, w_ref, out_ref):
    out_ref[...] = h_ref[...] + _dot(o_ref[...].astype(BF16), w_ref[...])


def out_proj(o, h, w, tm=512):
    n, k = o.shape
    d = w.shape[1]
    tm = min(tm, n)
    return pl.pallas_call(
        _out_proj_kernel, grid=(n // tm,),
        in_specs=[pl.BlockSpec((tm, k), lambda i: (i, 0)), pl.BlockSpec((tm, d), lambda i: (i, 0)),
                  pl.BlockSpec(w.shape, lambda i: (0, 0))],
        out_specs=pl.BlockSpec((tm, d), lambda i: (i, 0)),
        out_shape=jax.ShapeDtypeStruct((n, d), F32),
        compiler_params=_params(("parallel",)), name="out_proj",
    )(o, h, w)


def _merge_a_kernel(oc_ref, os_ref, ow_ref, g_ref, h_ref, w_ref, out_ref, o_sc, *, heads):
    gate = jax.nn.sigmoid(g_ref[...])
    for h in range(heads):
        sl = slice(h * HD, (h + 1) * HD)
        o = (gate[:, 3 * h:3 * h + 1] * oc_ref[:, sl] + gate[:, 3 * h + 1:3 * h + 2] * os_ref[:, sl]
             + gate[:, 3 * h + 2:3 * h + 3] * ow_ref[:, sl])
        o_sc[:, sl] = o
    out_ref[...] = h_ref[...] + _dot(o_sc[...].astype(BF16), w_ref[...])


def merge_a(o_cmp, o_sel, o_win, gates, h, w, tm=512):
    n, k = o_cmp.shape
    d = w.shape[1]
    tm = min(tm, n)
    row = lambda c: pl.BlockSpec((tm, c), lambda i: (i, 0))
    return pl.pallas_call(
        functools.partial(_merge_a_kernel, heads=k // HD), grid=(n // tm,),
        in_specs=[row(k), row(k), row(k), row(gates.shape[1]), row(d), pl.BlockSpec(w.shape, lambda i: (0, 0))],
        out_specs=row(d), out_shape=jax.ShapeDtypeStruct((n, d), F32),
        scratch_shapes=[pltpu.VMEM((tm, k), F32)],
        compiler_params=_params(("parallel",)), name="merge_a",
    )(o_cmp, o_sel, o_win, gates, h, w)


def _merge_d_kernel(o1, o2, o3, l1, l2, l3, h_ref, w_ref, out_ref):
    a, b, c = l1[...], l2[...], l3[...]
    m = jnp.maximum(jnp.maximum(a, b), c)
    ea, eb, ec = jnp.exp(a - m), jnp.exp(b - m), jnp.exp(c - m)
    den = ea + eb + ec
    o = (ea / den) * o1[...] + (eb / den) * o2[...] + (ec / den) * o3[...]
    out_ref[...] = h_ref[...] + _dot(o.astype(BF16), w_ref[...])


def merge_d(outs, lses, h, w, tm=512):
    n, k = outs[0].shape
    d = w.shape[1]
    tm = min(tm, n)
    row = lambda c: pl.BlockSpec((tm, c), lambda i: (i, 0))
    return pl.pallas_call(
        _merge_d_kernel, grid=(n // tm,),
        in_specs=[row(k)] * 6 + [row(d), pl.BlockSpec(w.shape, lambda i: (0, 0))],
        out_specs=row(d), out_shape=jax.ShapeDtypeStruct((n, d), F32),
        compiler_params=_params(("parallel",)), name="merge_d",
    )(*outs, *lses, h, w)


def _ffn_kernel(h_ref, g_ref, wg_ref, wu_ref, wd_ref, out_ref, xn_sc, acc_sc):
    j = pl.program_id(1)

    @pl.when(j == 0)
    def _():
        x = h_ref[...]
        xn_sc[...] = _rms(x, g_ref[...]).astype(BF16)
        acc_sc[...] = x

    xn = xn_sc[...]
    a = _dot(xn, wg_ref[...])
    u = _dot(xn, wu_ref[...])
    mid = (a * jax.nn.sigmoid(a) * u).astype(BF16)
    acc_sc[...] += _dot(mid, wd_ref[...])

    @pl.when(j == pl.num_programs(1) - 1)
    def _():
        out_ref[...] = acc_sc[...]


def ffn(h, g, wg, wu, wd, tm=512, n_f=2):
    n, d = h.shape
    f = wg.shape[1]
    tm = min(tm, n)
    tf = f // n_f
    return pl.pallas_call(
        _ffn_kernel, grid=(n // tm, n_f),
        in_specs=[pl.BlockSpec((tm, d), lambda i, j: (i, 0)), pl.BlockSpec((1, d), lambda i, j: (0, 0)),
                  pl.BlockSpec((d, tf), lambda i, j: (0, j)), pl.BlockSpec((d, tf), lambda i, j: (0, j)),
                  pl.BlockSpec((tf, d), lambda i, j: (j, 0))],
        out_specs=pl.BlockSpec((tm, d), lambda i, j: (i, 0)),
        out_shape=jax.ShapeDtypeStruct((n, d), F32),
        scratch_shapes=[pltpu.VMEM((tm, d), BF16), pltpu.VMEM((tm, d), F32)],
        compiler_params=_params(("parallel", "arbitrary")), name="ffn",
    )(h, g.reshape(1, d), wg, wu, wd)


def _final_norm_kernel(x_ref, g_ref, o_ref):
    o_ref[...] = _rms(x_ref[...], g_ref[...])


def final_norm(x, g, tm=1024):
    n, d = x.shape
    tm = min(tm, n)
    return pl.pallas_call(
        _final_norm_kernel, grid=(n // tm,),
        in_specs=[pl.BlockSpec((tm, d), lambda i: (i, 0)), pl.BlockSpec((1, d), lambda i: (0, 0))],
        out_specs=pl.BlockSpec((tm, d), lambda i: (i, 0)),
        out_shape=jax.ShapeDtypeStruct((n, d), F32),
        compiler_params=_params(("parallel",)), name="final_norm",
    )(x, g.reshape(1, d))


def _fox_gate_kernel(f_ref, b_ref, lf_ref, cum_ref):
    x = f_ref[0] + b_ref[...]
    lf = jnp.minimum(x, 0.0) - jnp.log1p(jnp.exp(-jnp.abs(x)))
    lf_ref[0] = lf
    n = x.shape[1]
    lane = lax.broadcasted_iota(jnp.int32, x.shape, 1)
    c = lf
    s = 1
    while s < n:
        c = c + jnp.where(lane >= s, pltpu.roll(c, s, axis=1), 0.0)
        s *= 2
    cum_ref[0] = c


def fox_gate(f_t, b_f):
    b, hh, n = f_t.shape
    spec = pl.BlockSpec((1, hh, n), lambda i: (i, 0, 0))
    return pl.pallas_call(
        _fox_gate_kernel, grid=(b,),
        in_specs=[spec, pl.BlockSpec((hh, 1), lambda i: (0, 0))],
        out_specs=[spec, spec], out_shape=[jax.ShapeDtypeStruct(f_t.shape, F32)] * 2,
        compiler_params=_params(("parallel",)), name="fox_gate",
    )(f_t, b_f.reshape(hh, 1))


def _attn_init(m_sc, acc_sc):
    m_sc[...] = jnp.full(m_sc.shape, NEG, F32)
    acc_sc[...] = jnp.zeros(acc_sc.shape, F32)


def _value_rows(v_ref):
    vt = v_ref[0].T.astype(BF16)
    ones = jnp.ones((ONES_ROWS, vt.shape[1]), BF16)
    return [jnp.concatenate([vt[kv * HD:(kv + 1) * HD], ones], axis=0) for kv in range(vt.shape[0] // HD)]


def _attn_steps(slots, work, tq, m_sc, acc_sc):
    state = [(m_sc[h], acc_sc[h]) for h in slots]
    steps = [_attn_step(m, acc, *args, tq) for (m, acc), args in zip(state, work)]
    new = {}
    while len(new) < len(steps):
        for i, step in enumerate(steps):
            if i not in new:
                try:
                    next(step)
                except StopIteration as done:
                    new[i] = done.value
    for i, h in enumerate(slots):
        m_sc[h], acc_sc[h] = new[i]


def _attn_step(m_prev, acc_prev, k, vt, q, addends, tq):
    s = _nt(k, q)
    yield
    tk = s.shape[0]
    per_query = None
    if addends is not None and any(t.shape[0] == 1 for terms in addends for t in terms):
        per_query = [[t for t in terms if t.shape[0] == 1] for terms in addends]
        per_query = [sum(ts[1:], ts[0]) if ts else jnp.zeros((1, tq), F32) for ts in per_query]
        per_query = per_query[0] if len(per_query) == 1 else jnp.concatenate(per_query, axis=1)

    def rows(r0, r1):
        blk = s[r0:r1]
        if addends is None:
            return blk
        pieces = []
        for g, terms in enumerate(addends):
            piece = blk[:, g * tq:(g + 1) * tq]
            for term in terms:
                if term.shape[0] != 1:
                    piece = piece + term[r0:r1]
            pieces.append(piece)
        return pieces[0] if len(pieces) == 1 else jnp.concatenate(pieces, axis=1)

    mx = rows(0, SUBLANE)
    for r0 in range(SUBLANE, tk, SUBLANE):
        mx = jnp.maximum(mx, rows(r0, r0 + SUBLANE))
    mx = jnp.max(mx, axis=0, keepdims=True)
    m_new = jnp.maximum(m_prev, mx if per_query is None else mx + per_query)
    shift = m_new if per_query is None else m_new - per_query
    yield
    strip = 2 * SUBLANE
    p = jnp.concatenate([jnp.exp2(rows(r0, r0 + strip) - shift).astype(BF16) for r0 in range(0, tk, strip)], axis=0)
    yield
    return m_new, jnp.exp2(m_prev - m_new) * acc_prev + _dot(vt, p)


def _attn_finish(hq, group, tq, m_sc, acc_sc, ot_sc, o_ref, sink_ref=None, lset_sc=None, lse_ref=None):
    for h in range(hq):
        kv, g = divmod(h, group)
        cols = slice(g * tq, (g + 1) * tq)
        num, l, m = acc_sc[kv, 0:HD, cols], acc_sc[kv, HD:HD + 1, cols], m_sc[kv, :, cols]
        if sink_ref is not None:
            sk = sink_ref[:, h:h + 1] * LOG2E
            m_f = jnp.maximum(m, sk)
            w_old = jnp.exp2(m - m_f)
            l = l * w_old + jnp.exp2(sk - m_f)
            num = num * w_old
            m = m_f
        ot_sc[h * HD:(h + 1) * HD, :] = num / l
        if lset_sc is not None:
            lset_sc[h * HD:(h + 1) * HD, :] = jnp.broadcast_to(m * LN2 + jnp.log(l), num.shape)
    o_ref[0] = ot_sc[...].T
    if lset_sc is not None:
        lse_ref[0] = lset_sc[...].T


def _q_group(q_ref, kv, group):
    heads = [(q_ref[0, :, h * HD:(h + 1) * HD] * (SCALE * LOG2E)).astype(BF16)
             for h in range(kv * group, (kv + 1) * group)]
    return heads[0] if group == 1 else jnp.concatenate(heads, axis=0)


def _band_kernel(*refs, t, nb, n_back, dil, hq, group, with_sink, with_lse):
    refs = list(refs)
    q_ref, k_ref, v_ref = refs[:3]
    pos = 3
    sink_ref = None
    if with_sink:
        sink_ref = refs[pos]
        pos += 1
    o_ref = refs[pos]
    pos += 1
    lse_ref = None
    if with_lse:
        lse_ref = refs[pos]
        pos += 1
    m_sc, acc_sc, ot_sc = refs[pos:pos + 3]
    lset_sc = refs[pos + 3] if with_lse else None
    qi, j = pl.program_id(1), pl.program_id(2)
    kt = qi - nb + j

    @pl.when(j == 0)
    def _():
        _attn_init(m_sc, acc_sc)

    @pl.when(kt >= 0)
    def _():
        dist = (qi * t + lax.broadcasted_iota(jnp.int32, (t, t), 1)) - (kt * t + lax.broadcasted_iota(jnp.int32, (t, t), 0))
        band = jnp.where(dist >= 0, dist, n_back + 1)
        if dil > 1:
            band = jnp.where((dist & (dil - 1)) == 0, band, n_back + 1)
        addends = [[jnp.where(band <= n_back, 0.0, NEG)]] * group
        vts = _value_rows(v_ref)
        work = [(k_ref[0, :, kv * HD:(kv + 1) * HD].astype(BF16), vts[kv], _q_group(q_ref, kv, group), addends)
                for kv in range(hq // group)]
        _attn_steps(range(hq // group), work, t, m_sc, acc_sc)

    @pl.when(j == nb)
    def _():
        _attn_finish(hq, group, t, m_sc, acc_sc, ot_sc, o_ref, sink_ref, lset_sc, lse_ref)


def band_attention(q, kv, n_back, dil, t, sink=None, with_lse=False):
    b, l, cq = q.shape
    ck = kv.shape[2] // 2
    nb = -(-n_back // t)
    hq, hkv = cq // HD, ck // HD
    kmap = lambda blk: (lambda bi, qi, j: (bi, jnp.maximum(qi - nb + j, 0), blk))
    in_specs = [pl.BlockSpec((1, t, cq), lambda bi, qi, j: (bi, qi, 0)),
                pl.BlockSpec((1, t, ck), kmap(0)), pl.BlockSpec((1, t, ck), kmap(1))]
    args = [q, kv, kv]
    if sink is not None:
        in_specs.append(pl.BlockSpec((1, hq), lambda bi, qi, j: (0, 0)))
        args.append(sink.reshape(1, hq))
    o_spec = pl.BlockSpec((1, t, cq), lambda bi, qi, j: (bi, qi, 0))
    o_shape = jax.ShapeDtypeStruct(q.shape, F32)
    wide = hq // hkv * t
    scratch = [pltpu.VMEM((hkv, 1, wide), F32), pltpu.VMEM((hkv, HD + ONES_ROWS, wide), F32), pltpu.VMEM((cq, t), F32)]
    if with_lse:
        scratch.append(pltpu.VMEM((cq, t), F32))
    return pl.pallas_call(
        functools.partial(_band_kernel, t=t, nb=nb, n_back=n_back, dil=dil, hq=hq, group=hq // hkv,
                          with_sink=sink is not None, with_lse=with_lse),
        grid=(b, l // t, nb + 1),
        in_specs=in_specs,
        out_specs=[o_spec, o_spec] if with_lse else o_spec,
        out_shape=[o_shape, o_shape] if with_lse else o_shape,
        scratch_shapes=scratch,
        compiler_params=_params(("parallel", "parallel", "arbitrary")), name="band_attention",
    )(*args)


def _dilated_kernel(q_ref, k_ref, v_ref, o_ref, lse_ref, m_sc, acc_sc, *, dil, sub, n_back):
    qi, j = pl.program_id(2), pl.program_id(3)
    kt = qi - 1 + j

    @pl.when(j == 0)
    def _():
        _attn_init(m_sc, acc_sc)

    @pl.when(kt >= 0)
    def _():
        dist = (qi * sub + lax.broadcasted_iota(jnp.int32, (sub, sub), 1)) - (kt * sub + lax.broadcasted_iota(jnp.int32, (sub, sub), 0))
        addends = [[jnp.where(jnp.where(dist >= 0, dist, n_back + 1) <= n_back, 0.0, NEG)]]
        ones = jnp.ones((ONES_ROWS, sub), BF16)
        work = []
        for r in range(dil):
            rows = pl.ds(r, sub, stride=dil)
            q = (q_ref[0, rows, :] * (SCALE * LOG2E)).astype(BF16)
            k = k_ref[0, rows, :].astype(BF16)
            vt = v_ref[0, rows, :].T.astype(BF16)
            for h in range(LANE // HD):
                cols = slice(h * HD, (h + 1) * HD)
                work.append((k[:, cols], jnp.concatenate([vt[cols], ones], axis=0), q[:, cols], addends))
        _attn_steps(range(dil * (LANE // HD)), work, sub, m_sc, acc_sc)

    @pl.when(j == 1)
    def _():
        for r in range(dil):
            outs, lses = [], []
            for h in range(LANE // HD):
                slot = r * (LANE // HD) + h
                num, l = acc_sc[slot, 0:HD], acc_sc[slot, HD:HD + 1]
                outs.append(num / l)
                lses.append(jnp.broadcast_to(m_sc[slot] * LN2 + jnp.log(l), num.shape))
            rows = pl.ds(r, sub, stride=dil)
            o_ref[0, rows, :] = jnp.concatenate(outs, axis=0).T
            lse_ref[0, rows, :] = jnp.concatenate(lses, axis=0).T


def dilated_attention(q, kv, window, dil):
    b, l, cq = q.shape
    sub = window // dil
    n_pair = cq // LANE
    kmap = lambda off: (lambda bi, c, qi, j: (bi, jnp.maximum(qi - 1 + j, 0), off + c))
    spec = pl.BlockSpec((1, window, LANE), lambda bi, c, qi, j: (bi, qi, c))
    slots = dil * (LANE // HD)
    return pl.pallas_call(
        functools.partial(_dilated_kernel, dil=dil, sub=sub, n_back=sub),
        grid=(b, n_pair, l // window, 2),
        in_specs=[spec, pl.BlockSpec((1, window, LANE), kmap(0)), pl.BlockSpec((1, window, LANE), kmap(n_pair))],
        out_specs=[spec, spec], out_shape=[jax.ShapeDtypeStruct(q.shape, F32)] * 2,
        scratch_shapes=[pltpu.VMEM((slots, 1, sub), F32), pltpu.VMEM((slots, HD + ONES_ROWS, sub), F32)],
        compiler_params=_params(("parallel", "parallel", "parallel", "arbitrary")), name="dilated_attention",
    )(q, kv, kv)


def _flash_kernel(qt_ref, kt_ref, *refs, t, hq, group, mode):
    if mode == "fox":
        q_ref, k_ref, v_ref, cq_ref, ck_ref, o_ref, m_sc, acc_sc, ot_sc = refs
    else:
        q_ref, k_ref, v_ref, sel_ref, o_ref, m_sc, acc_sc, ot_sc = refs
    step = pl.program_id(2)
    qi, kj = qt_ref[step], kt_ref[step]

    @pl.when(kj == 0)
    def _():
        _attn_init(m_sc, acc_sc)

    def body(diagonal):
        vts = _value_rows(v_ref)
        causal = lax.broadcasted_iota(jnp.int32, (t, t), 1) >= lax.broadcasted_iota(jnp.int32, (t, t), 0)
        if mode == "sel":
            selb = sel_ref[0].astype(BF16)
            width = selb.shape[1]
            n_blk = width // (hq // group)
            key_blk = (kj * t + lax.broadcasted_iota(jnp.int32, (t, width), 0)) // SEL_BLOCK
            col = lax.broadcasted_iota(jnp.int32, (t, width), 1)
        work = []
        for kv in range(hq // group):
            k = k_ref[0, :, kv * HD:(kv + 1) * HD].astype(BF16)
            if mode == "fox":
                tail = [jnp.where(causal, 0.0, NEG)] if diagonal else []
                addends = [[cq_ref[0, 0, h:h + 1, :] * LOG2E, ck_ref[0, 0, :, h:h + 1] * (-LOG2E)] + tail
                           for h in range(kv * group, (kv + 1) * group)]
            else:
                expand = jnp.where(col == kv * n_blk + key_blk, 1.0, 0.0).astype(BF16)
                flag = _nt(expand, selb)
                if diagonal:
                    flag = jnp.where(causal, flag, 0.0)
                addends = [[jnp.where(flag > 0.5, 0.0, NEG)]] * group
            work.append((k, vts[kv], _q_group(q_ref, kv, group), addends))
        _attn_steps(range(hq // group), work, t, m_sc, acc_sc)

    @pl.when(kj < qi)
    def _():
        body(False)

    @pl.when(kj == qi)
    def _():
        body(True)
        _attn_finish(hq, group, t, m_sc, acc_sc, ot_sc, o_ref)


def flash_attention(q, kv, t, cq, n_col, mode, extra, kv_per_col=2):
    b, l, _ = q.shape
    hq = cq // HD
    nq = l // t
    pairs = [(qi, kj) for qi in range(nq) for kj in range(qi + 1)]
    qt = jnp.asarray([p[0] for p in pairs], jnp.int32)
    kt = jnp.asarray([p[1] for p in pairs], jnp.int32)
    kmap = lambda off: (lambda bi, c, s, qt, kt: (bi, kt[s], off + c))
    in_specs = [pl.BlockSpec((1, t, cq), lambda bi, c, s, qt, kt: (bi, qt[s], c)),
                pl.BlockSpec((1, t, kv_per_col * HD), kmap(0)), pl.BlockSpec((1, t, kv_per_col * HD), kmap(n_col))]
    if mode == "fox":
        in_specs += [pl.BlockSpec((1, 1, hq, t), lambda bi, c, s, qt, kt: (bi, c, 0, qt[s])),
                     pl.BlockSpec((1, 1, t, hq), lambda bi, c, s, qt, kt: (bi, c, kt[s], 0))]
    else:
        in_specs += [pl.BlockSpec((1, t, extra[0].shape[2]), lambda bi, c, s, qt, kt: (bi, qt[s], 0))]
    return pl.pallas_call(
        functools.partial(_flash_kernel, t=t, hq=hq, group=hq // kv_per_col, mode=mode),
        grid_spec=pltpu.PrefetchScalarGridSpec(
            num_scalar_prefetch=2, grid=(b, n_col, len(pairs)), in_specs=in_specs,
            out_specs=pl.BlockSpec((1, t, cq), lambda bi, c, s, qt, kt: (bi, qt[s], c)),
            scratch_shapes=[pltpu.VMEM((kv_per_col, 1, hq // kv_per_col * t), F32),
                            pltpu.VMEM((kv_per_col, HD + ONES_ROWS, hq // kv_per_col * t), F32),
                            pltpu.VMEM((cq, t), F32)]),
        out_shape=jax.ShapeDtypeStruct(q.shape, F32),
        compiler_params=_params(("parallel", "parallel", "arbitrary")), name="flash_" + mode,
    )(qt, kt, q, kv, kv, *extra)


def _compress_regroup(x_sc, j, rows_at, n_rows, r0, total):
    for s in range(0, CMP_STRIDE, 2):
        a, b = rows_at(s), rows_at(s + 1)
        low = lax.broadcasted_iota(jnp.int32, a.shape, 1) < HD
        cols = slice(s * HD, (s + 2) * HD)
        x_sc[j, pl.ds(r0, n_rows), cols] = jnp.where(low, a, pltpu.roll(b, HD, axis=1)).astype(BF16)
        x_sc[j, pl.ds(total + r0, n_rows), cols] = jnp.where(low, pltpu.roll(a, HD, axis=1), b).astype(BF16)


def _compress_mlp(x_sc, w1_ref, b1_ref, w2_ref, out_ref, total):
    for j in range(2):
        parts = _dot(x_sc[j], w1_ref[j])
        half = parts.shape[1] // 2
        pre = b1_ref[j] + parts[:, :half] + pltpu.roll(parts[:, half:], 2 * total - 1, axis=0)
        act = pre * jax.nn.sigmoid(pre)
        res = _dot(act.astype(BF16), w2_ref[j])
        for kv in range(2):
            out_ref[0, :, j * LANE + kv * HD:j * LANE + (kv + 1) * HD] = res[kv * total:(kv + 1) * total]


def _compress_rows_kernel(k_ref, v_ref, w1_ref, b1_ref, w2_ref, out_ref, x_sc, *, chunks):
    for j, src in enumerate((k_ref, v_ref)):
        _compress_regroup(x_sc, j, lambda s, src=src: src[0, pl.ds(s, chunks, stride=CMP_STRIDE), :], chunks, 0, chunks)
    _compress_mlp(x_sc, w1_ref, b1_ref, w2_ref, out_ref, chunks)


def _compress_paged_kernel(pt_ref, pool_ref, perm_ref, w1_ref, b1_ref, w2_ref, out_ref, x_sc, buf, sem, *, n_src, chunks):
    bi = pl.program_id(0)
    slot = lax.rem(bi, 2)
    total = n_src * chunks
    fold = 2 * SUBLANE // chunks

    def page_copies(seq_i, into):
        return [pltpu.make_async_copy(pool_ref.at[pt_ref[seq_i, p]], buf.at[into, p], sem.at[into]) for p in range(n_src)]

    @pl.when(bi == 0)
    def _():
        for cp in page_copies(bi, slot):
            cp.start()

    @pl.when(bi + 1 < pl.num_programs(0))
    def _():
        for cp in page_copies(bi + 1, 1 - slot):
            cp.start()

    for cp in page_copies(bi, slot):
        cp.wait()

    def group(i2, carry):
        perm = perm_ref[...]
        for j in range(2):
            sorted_tokens = [_nt(perm, buf[slot, i2 * fold + f, j * LANE:(j + 1) * LANE, :].astype(BF16))
                             for f in range(fold)]
            rows_at = lambda s, toks=sorted_tokens: jnp.concatenate(
                [tok[s * chunks:(s + 1) * chunks] for tok in toks], axis=0)
            _compress_regroup(x_sc, j, rows_at, fold * chunks, pl.multiple_of(i2 * (fold * chunks), fold * chunks), total)
        return carry

    lax.fori_loop(0, n_src // fold, group, 0, unroll=8)
    _compress_mlp(x_sc, w1_ref, b1_ref, w2_ref, out_ref, total)


def compress(x, page_table, w1p, b1p, w2p):
    const = lambda nd: (lambda *_: (0,) * nd)
    w_specs = [pl.BlockSpec(w1p.shape, const(3)), pl.BlockSpec(b1p.shape, const(3)), pl.BlockSpec(w2p.shape, const(3))]
    if page_table is None:
        b, l, width = x.shape
        n_src, chunks = 1, l // CMP_STRIDE
        scratch = [pltpu.VMEM((2, 2 * chunks, CMP_STRIDE * HD), BF16)]
        halves = [pl.BlockSpec((1, l, LANE), lambda i: (i, 0, 0)), pl.BlockSpec((1, l, LANE), lambda i: (i, 0, 1))]
        grid_spec = pl.GridSpec(
            grid=(b,), in_specs=halves + w_specs,
            out_specs=pl.BlockSpec((1, chunks, width), lambda i: (i, 0, 0)), scratch_shapes=scratch)
        args = (x, x, w1p, b1p, w2p)
        body = functools.partial(_compress_rows_kernel, chunks=chunks)
    else:
        b, n_src = page_table.shape
        _, width, rows = x.shape
        chunks = rows // CMP_STRIDE
        fold = 2 * SUBLANE // chunks
        scratch = [pltpu.VMEM((2, 2 * n_src * chunks, CMP_STRIDE * HD), BF16),
                   pltpu.VMEM((2, n_src, width, rows), F32), pltpu.SemaphoreType.DMA((2,))]
        tok = np.arange(rows)
        perm = jnp.asarray(tok[None, :] == ((tok % chunks) * CMP_STRIDE + tok // chunks)[:, None], dtype=BF16)
        grid_spec = pltpu.PrefetchScalarGridSpec(
            num_scalar_prefetch=1, grid=(b,),
            in_specs=[pl.BlockSpec(memory_space=pl.ANY), pl.BlockSpec((rows, rows), lambda i, pt: (0, 0))] + w_specs,
            out_specs=pl.BlockSpec((1, n_src * chunks, width), lambda i, pt: (i, 0, 0)), scratch_shapes=scratch)
        args = (page_table, x, perm, w1p, b1p, w2p)
        body = functools.partial(_compress_paged_kernel, n_src=n_src, chunks=chunks)
    return pl.pallas_call(
        body,
        grid_spec=grid_spec,
        out_shape=jax.ShapeDtypeStruct((b, n_src * chunks, width), F32),
        compiler_params=_params(("arbitrary",)), name="compress",
    )(*args)


def _compress_weights(phi_w1, phi_b1, phi_w2):
    hidden = phi_w1.shape[-1]
    parts = CMP_BLOCK // CMP_STRIDE
    w1p = phi_w1.reshape(2, parts, CMP_STRIDE * HD, hidden).transpose(0, 2, 1, 3).reshape(2, CMP_STRIDE * HD, parts * hidden)
    return w1p.astype(BF16), phi_b1[:, None, :], phi_w2.astype(BF16)


def _top_blocks(imp, n_take):
    lane = lax.broadcasted_iota(jnp.int32, imp.shape, 1).astype(F32)
    sel = jnp.zeros(imp.shape, F32)
    cur = imp
    for _ in range(n_take):
        mx = jnp.max(cur, axis=1, keepdims=True)
        first = jnp.min(jnp.where(cur == mx, lane, 1e9), axis=1, keepdims=True)
        pick = lane == first
        sel = jnp.where(pick, 1.0, sel)
        cur = jnp.where(pick, DEAD, cur)
    return sel


def _cmp_prompt_kernel(q_ref, kvc_ref, cover_ref, o_ref, sel_ref, *, t, n_cmp, n_kv, group, n_blk):
    qi = pl.program_id(1)
    rows = kvc_ref.shape[1]
    qpos = qi * t + lax.broadcasted_iota(jnp.int32, (t, 1), 0)
    ci = lax.broadcasted_iota(jnp.int32, (1, rows), 1)
    valid = (ci * CMP_STRIDE + CMP_BLOCK - 1 <= qpos) & (ci < n_cmp)
    blk = lax.broadcasted_iota(jnp.int32, (1, n_blk), 1)
    cur = qpos // SEL_BLOCK
    forced = (blk == 0) | (blk == cur) | (blk == cur - 1)
    cover = cover_ref[...]
    for kv in range(n_kv):
        kc = kvc_ref[0, :, kv * HD:(kv + 1) * HD].astype(BF16)
        vc = kvc_ref[0, :, (n_kv + kv) * HD:(n_kv + kv + 1) * HD].astype(BF16)
        imp = jnp.zeros((t, n_blk), F32)
        for g in range(group):
            h = kv * group + g
            s = jnp.where(valid, _nt(q_ref[0, :, h * HD:(h + 1) * HD].astype(BF16), kc) * SCALE, NEG)
            m = jnp.max(s, axis=1, keepdims=True)
            e = jnp.where(valid, jnp.exp(s - m), 0.0)
            den = jnp.sum(e, axis=1, keepdims=True)
            p = (e / jnp.where(den > 0, den, 1.0)).astype(BF16)
            o_ref[0, :, h * HD:(h + 1) * HD] = _dot(p, vc)
            imp = imp + _dot(p, cover)
        imp = jnp.where(blk * SEL_BLOCK <= qpos, jnp.where(forced, FORCE_SCORE, imp), NEG)
        sel_ref[0, :, kv * n_blk:(kv + 1) * n_blk] = _top_blocks(imp, min(N_SEL, n_blk))


def _cover_matrix(rows, n_cmp, n_blk, width):
    c_start = np.arange(rows)[:, None] * CMP_STRIDE
    s_start = np.arange(width)[None, :] * SEL_BLOCK
    cov = (c_start < s_start + SEL_BLOCK) & (c_start + CMP_BLOCK > s_start)
    cov &= (np.arange(rows)[:, None] < n_cmp) & (np.arange(width)[None, :] < n_blk)
    return jnp.asarray(cov, dtype=BF16)


def cmp_attend_prompt(q, kvc, t, n_kv):
    b, l, cq = q.shape
    rows = kvc.shape[1]
    n_blk = l // SEL_BLOCK
    cover = _cover_matrix(rows, rows - 1, n_blk, n_blk)
    return pl.pallas_call(
        functools.partial(_cmp_prompt_kernel, t=t, n_cmp=rows - 1, n_kv=n_kv, group=cq // HD // n_kv, n_blk=n_blk),
        grid=(b, l // t),
        in_specs=[pl.BlockSpec((1, t, cq), lambda bi, qi: (bi, qi, 0)),
                  pl.BlockSpec((1, rows, kvc.shape[2]), lambda bi, qi: (bi, 0, 0)),
                  pl.BlockSpec(cover.shape, lambda bi, qi: (0, 0))],
        out_specs=[pl.BlockSpec((1, t, cq), lambda bi, qi: (bi, qi, 0)),
                   pl.BlockSpec((1, t, n_kv * n_blk), lambda bi, qi: (bi, qi, 0))],
        out_shape=[jax.ShapeDtypeStruct(q.shape, F32), jax.ShapeDtypeStruct((b, l, n_kv * n_blk), F32)],
        compiler_params=_params(("parallel", "parallel")), name="cmp_attend_prompt",
    )(q, kvc, cover)


def _diag_heads(x, rows_kv, n_kv):
    out = jnp.zeros((x.shape[0], HD), F32)
    for kv in range(n_kv):
        out = out + jnp.where(rows_kv == kv, x[:, kv * HD:(kv + 1) * HD], 0.0)
    return out


def _cmp_sample_kernel(q_ref, kvc_ref, cover_ref, o_ref, sel_ref, *, n_tok, heads, n_kv, n_cmp, past_len, n_blk):
    rows = kvc_ref.shape[1]
    r_all = n_tok * heads
    c = n_kv * HD
    width = cover_ref.shape[1]
    group = heads // n_kv
    ridx = lax.broadcasted_iota(jnp.int32, (r_all, 1), 0)
    qpos = past_len + ridx // heads
    ci = lax.broadcasted_iota(jnp.int32, (1, rows), 1)
    valid = (ci * CMP_STRIDE + CMP_BLOCK - 1 <= qpos) & (ci < n_cmp)
    g_rows = r_all // group
    n_seq = q_ref.shape[0]
    imps = []
    for i in range(n_seq):
        kc = kvc_ref[i, :, 0:c].astype(BF16)
        vc = kvc_ref[i, :, c:2 * c].astype(BF16)
        s = jnp.where(valid, _nt(q_ref[i].astype(BF16), kc) * SCALE, NEG)
        m = jnp.max(s, axis=1, keepdims=True)
        e = jnp.where(valid, jnp.exp(s - m), 0.0)
        den = jnp.sum(e, axis=1, keepdims=True)
        p = (e / jnp.where(den > 0, den, 1.0)).astype(BF16)
        o_ref[i] = _diag_heads(_dot(p, vc), (ridx % heads) // group, n_kv)
        imps.append(jnp.sum(_dot(p, cover_ref[...]).reshape(g_rows, group, width), axis=1))
    imp = jnp.concatenate(imps, axis=0)
    gpos = past_len + (lax.broadcasted_iota(jnp.int32, (n_seq * g_rows, 1), 0) % g_rows) // n_kv
    blk = lax.broadcasted_iota(jnp.int32, (1, width), 1)
    cur = gpos // SEL_BLOCK
    forced = (blk == 0) | (blk == cur) | (blk == cur - 1)
    imp = jnp.where(blk * SEL_BLOCK <= gpos, jnp.where(forced, FORCE_SCORE, imp), NEG)
    imp = jnp.where(blk < n_blk, imp, DEAD)
    sel = _top_blocks(imp, min(N_SEL, n_blk))
    for i in range(n_seq):
        sel_ref[i] = sel[i * g_rows:(i + 1) * g_rows]


def cmp_attend_sample(qbd, kvc, n_tok, heads, n_kv, past_len):
    b, r_all, c = qbd.shape
    rows = kvc.shape[1]
    n_blk = past_len // SEL_BLOCK + -(-n_tok // SEL_BLOCK)
    width = -(-n_blk // LANE) * LANE
    cover = _cover_matrix(rows, rows - 1, n_blk, width)
    g_rows = r_all // (heads // n_kv)
    n_seq = next(n for n in (8, 4, 2, 1) if b % n == 0)
    return pl.pallas_call(
        functools.partial(_cmp_sample_kernel, n_tok=n_tok, heads=heads, n_kv=n_kv, n_cmp=rows - 1,
                          past_len=past_len, n_blk=n_blk),
        grid=(b // n_seq,),
        in_specs=[pl.BlockSpec((n_seq, r_all, c), lambda i: (i, 0, 0)),
                  pl.BlockSpec((n_seq, rows, 2 * c), lambda i: (i, 0, 0)),
                  pl.BlockSpec(cover.shape, lambda i: (0, 0))],
        out_specs=[pl.BlockSpec((n_seq, r_all, HD), lambda i: (i, 0, 0)),
                   pl.BlockSpec((n_seq, g_rows, width), lambda i: (i, 0, 0))],
        out_shape=[jax.ShapeDtypeStruct((b, r_all, HD), F32), jax.ShapeDtypeStruct((b, g_rows, width), F32)],
        compiler_params=_params(("parallel",)), name="cmp_attend_sample",
    )(qbd, kvc, cover)


def _dec_init(m_sc, l_sc, acc_sc):
    m_sc[...] = jnp.full(m_sc.shape, NEG, F32)
    l_sc[...] = jnp.zeros(l_sc.shape, F32)
    acc_sc[...] = jnp.zeros(acc_sc.shape, F32)


def _dec_update(qb, kt, vt, bias, mask, m_sc, l_sc, acc_sc):
    s = _dot(qb, kt) * SCALE
    if bias is not None:
        s = s + bias
    if mask is not None:
        s = jnp.where(mask, s, NEG)
    m_prev = m_sc[...]
    m_new = jnp.maximum(m_prev, jnp.max(s, axis=1, keepdims=True))
    alpha = jnp.exp(m_prev - m_new)
    p = jnp.exp(s - m_new)
    if mask is not None:
        p = jnp.where(mask, p, 0.0)
    l_sc[...] = alpha * l_sc[...] + jnp.sum(p, axis=1, keepdims=True)
    acc_sc[...] = alpha * acc_sc[...] + _nt(p.astype(BF16), vt)
    m_sc[...] = m_new


def _dec_new_tokens(q, knew, vnew, biases, masks, m_sc, l_sc, acc_sc):
    n_new = knew.shape[0]
    scores = []
    for u in range(n_new):
        s = jnp.sum(q * knew[u:u + 1, :], axis=1, keepdims=True) * SCALE
        if biases is not None:
            s = s + biases[u]
        scores.append(jnp.where(masks[u], s, NEG))
    m_prev = m_sc[...]
    m_new = m_prev
    for s in scores:
        m_new = jnp.maximum(m_new, s)
    alpha = jnp.exp(m_prev - m_new)
    l = alpha * l_sc[...]
    acc = alpha * acc_sc[...]
    for u in range(n_new):
        p = jnp.where(masks[u], jnp.exp(scores[u] - m_new), 0.0)
        l = l + p
        acc = acc + p * vnew[u:u + 1, :]
    l_sc[...] = l
    acc_sc[...] = acc
    m_sc[...] = m_new


def _token_cumsum(x, n_tok, heads):
    parts, run = [], None
    for u in range(n_tok):
        seg = x[u * heads:(u + 1) * heads]
        run = seg if run is None else run + seg
        parts.append(run)
    return jnp.concatenate(parts, axis=0), parts


def _dec_window_kernel(*refs, n_tok, heads, n_kv, lb, n_back, dil, chunk, with_sink, with_lse):
    refs = list(refs)
    q_ref, cache_ref, new_ref = refs[:3]
    pos = 3
    sink_ref = None
    if with_sink:
        sink_ref = refs[pos]
        pos += 1
    o_ref = refs[pos]
    pos += 1
    lse_ref = None
    if with_lse:
        lse_ref = refs[pos]
        pos += 1
    m_sc, l_sc, acc_sc = refs[pos:pos + 3]
    r_all = n_tok * heads
    c = n_kv * HD
    group = heads // n_kv
    ridx = lax.broadcasted_iota(jnp.int32, (r_all, 1), 0)
    tok = ridx // heads
    _dec_init(m_sc, l_sc, acc_sc)
    q = q_ref[0]
    qb = q.astype(BF16)

    def ok(dist):
        good = (dist >= 0) & (dist <= n_back)
        if dil > 1:
            good = good & (dist % dil == 0)
        return good

    for c0 in range(0, lb, chunk):
        kt = cache_ref[0, 0:c, c0:c0 + chunk].astype(BF16)
        vt = cache_ref[0, c:2 * c, c0:c0 + chunk].astype(BF16)
        dist = (lb + tok) - (c0 + lax.broadcasted_iota(jnp.int32, (1, chunk), 1))
        _dec_update(qb, kt, vt, None, ok(dist), m_sc, l_sc, acc_sc)
    new = new_ref[0]
    _dec_new_tokens(q, new[:, 0:c], new[:, c:2 * c], None, [ok(tok - u) for u in range(n_tok)], m_sc, l_sc, acc_sc)
    m, l, acc = m_sc[...], l_sc[...], acc_sc[...]
    if with_sink:
        sk = sink_ref[...]
        m_f = jnp.maximum(m, sk)
        scale_old = jnp.exp(m - m_f)
        l = l * scale_old + jnp.exp(sk - m_f)
        acc = acc * scale_old
        m = m_f
    o_ref[0] = _diag_heads(acc / l, (ridx % heads) // group, n_kv)
    if with_lse:
        lse_ref[0] = jnp.broadcast_to(m + jnp.log(l), (r_all, HD))


def dec_window(qbd, cache_t, new, n_tok, heads, n_kv, n_back, dil, sink_rows=None, with_lse=False):
    b, r_all, c = qbd.shape
    lb = cache_t.shape[2]
    in_specs = [pl.BlockSpec((1, r_all, c), lambda i: (i, 0, 0)),
                pl.BlockSpec((1, 2 * c, lb), lambda i: (i, 0, 0)),
                pl.BlockSpec((1, n_tok, 2 * c), lambda i: (i, 0, 0))]
    args = [qbd, cache_t, new]
    if sink_rows is not None:
        in_specs.append(pl.BlockSpec((r_all, 1), lambda i: (0, 0)))
        args.append(sink_rows)
    o_spec = pl.BlockSpec((1, r_all, HD), lambda i: (i, 0, 0))
    o_shape = jax.ShapeDtypeStruct((b, r_all, HD), F32)
    return pl.pallas_call(
        functools.partial(_dec_window_kernel, n_tok=n_tok, heads=heads, n_kv=n_kv, lb=lb, n_back=n_back, dil=dil,
                          chunk=min(lb, 512), with_sink=sink_rows is not None, with_lse=with_lse),
        grid=(b,), in_specs=in_specs,
        out_specs=[o_spec, o_spec] if with_lse else o_spec,
        out_shape=[o_shape, o_shape] if with_lse else o_shape,
        scratch_shapes=[pltpu.VMEM((r_all, 1), F32), pltpu.VMEM((r_all, 1), F32), pltpu.VMEM((r_all, c), F32)],
        compiler_params=_params(("parallel",)), name="dec_window",
    )(*args)


def _dec_paged_kernel(*refs, n_pg, n_tok, heads, n_kv, mode):
    pt_ref, pool_ref = refs[:2]
    pos = 2
    lf_ref = None
    if mode == "fox":
        lf_ref = refs[pos]
        pos += 1
    q_ref, new_ref, x_ref = refs[pos:pos + 3]
    pos += 3
    if mode == "fox":
        tri_ref = refs[pos]
        pos += 1
    o_ref = refs[pos]
    pos += 1
    buf, sem, k_sc, v_sc, m_sc, l_sc, acc_sc = refs[pos:pos + 7]
    pos += 7
    if mode == "fox":
        lf_buf, lf_sem, c_sc, carry_sc = refs[pos:pos + 4]
    bi, gi = pl.program_id(0), pl.program_id(1)
    n_grp = pl.num_programs(1)
    step = bi * n_grp + gi
    slot = lax.rem(step, 2)

    def page_copies(seq_i, grp_i, into):
        copies = []
        for p in range(n_pg):
            page = pt_ref[seq_i, grp_i * n_pg + p]
            copies.append(pltpu.make_async_copy(pool_ref.at[page], buf.at[into, p], sem.at[into]))
            if mode == "fox":
                copies.append(pltpu.make_async_copy(lf_ref.at[page], lf_buf.at[into, p], lf_sem.at[into]))
        return copies

    @pl.when(step == 0)
    def _():
        for cp in page_copies(bi, gi, slot):
            cp.start()

    @pl.when(step + 1 < pl.num_programs(0) * n_grp)
    def _():
        nxt = step + 1
        for cp in page_copies(nxt // n_grp, lax.rem(nxt, n_grp), 1 - slot):
            cp.start()

    for cp in page_copies(bi, gi, slot):
        cp.wait()
    pages = [buf.at[slot, p] for p in range(n_pg)]
    lf_pages = [lf_buf.at[slot, p] for p in range(n_pg)] if mode == "fox" else None
    r_all = n_tok * heads
    c = n_kv * HD
    group = heads // n_kv
    tk = n_pg * PAGE
    ridx = lax.broadcasted_iota(jnp.int32, (r_all, 1), 0)
    tok = ridx // heads
    head = ridx % heads

    @pl.when(gi == 0)
    def _():
        _dec_init(m_sc, l_sc, acc_sc)
        if mode == "fox":
            carry_sc[...] = jnp.zeros(carry_sc.shape, F32)

    for i in range(n_pg):
        k_sc[:, i * PAGE:(i + 1) * PAGE] = pages[i][0:c, :].astype(BF16)
        v_sc[:, i * PAGE:(i + 1) * PAGE] = pages[i][c:2 * c, :].astype(BF16)
    q = q_ref[0]
    qb = q.astype(BF16)

    if mode == "fox":
        tri = tri_ref[...]
        hi, mid, lo = _split3(jnp.concatenate([lf_pages[i][...] for i in range(n_pg)], axis=0))
        within = _dot(hi, tri) + _dot(mid, tri) + _dot(lo, tri)
        carry = carry_sc[...]
        for i in range(n_pg):
            page = within[i * heads:(i + 1) * heads]
            c_sc[:, i * PAGE:(i + 1) * PAGE] = page + carry
            carry = carry + page[:, PAGE - 1:PAGE]
        carry_sc[...] = carry
        bias = _token_cumsum(x_ref[0], n_tok, heads)[0] - jnp.concatenate([c_sc[...]] * n_tok, axis=0)
        _dec_update(qb, k_sc[...], v_sc[...], bias, None, m_sc, l_sc, acc_sc)
    else:
        flags = x_ref[0].astype(BF16)
        width = flags.shape[1]
        key_blk = (gi * tk + lax.broadcasted_iota(jnp.int32, (width, tk), 1)) // SEL_BLOCK
        expand = jnp.where(lax.broadcasted_iota(jnp.int32, (width, tk), 0) == key_blk, 1.0, 0.0).astype(BF16)
        _dec_update(qb, k_sc[...], v_sc[...], None, _dot(flags, expand) > 0.5, m_sc, l_sc, acc_sc)

    @pl.when(gi == pl.num_programs(1) - 1)
    def _():
        new = new_ref[0]
        causal = [tok >= u for u in range(n_tok)]
        if mode == "fox":
            fq, per_tok = _token_cumsum(x_ref[0], n_tok, heads)
            total = jnp.concatenate([carry_sc[...]] * n_tok, axis=0)
            biases = [fq - jnp.concatenate([per_tok[u]] * n_tok, axis=0) - total for u in range(n_tok)]
            _dec_new_tokens(q, new[:, 0:c], new[:, c:2 * c], biases, causal, m_sc, l_sc, acc_sc)
        else:
            n_past_blk = (pl.num_programs(1) * tk) // SEL_BLOCK
            flags = x_ref[0]
            lane = lax.broadcasted_iota(jnp.int32, flags.shape, 1)
            new_on = jnp.sum(jnp.where(lane == n_past_blk, flags, 0.0), axis=1, keepdims=True) > 0.5
            _dec_new_tokens(q, new[:, 0:c], new[:, c:2 * c], None, [m & new_on for m in causal], m_sc, l_sc, acc_sc)
        o_ref[0] = _diag_heads(acc_sc[...] / l_sc[...], head // group, n_kv)


def dec_paged(qbd, pool_t, page_table, new, x, n_tok, heads, n_kv, mode, n_pg, lf_pool_t=None):
    b, r_all, c = qbd.shape
    n_pages = page_table.shape[1]
    in_specs = [pl.BlockSpec(memory_space=pl.ANY)]
    args = [pool_t]
    if mode == "fox":
        in_specs.append(pl.BlockSpec(memory_space=pl.ANY))
        args.append(lf_pool_t)
    in_specs += [pl.BlockSpec((1, r_all, c), lambda i, g, pt: (i, 0, 0)),
                 pl.BlockSpec((1, n_tok, 2 * c), lambda i, g, pt: (i, 0, 0)),
                 pl.BlockSpec((1, r_all, x.shape[2]), lambda i, g, pt: (i, 0, 0))]
    args += [qbd, new, x]
    scratch = [pltpu.VMEM((2, n_pg, 2 * c, PAGE), F32), pltpu.SemaphoreType.DMA((2,)),
               pltpu.VMEM((c, n_pg * PAGE), BF16), pltpu.VMEM((c, n_pg * PAGE), BF16),
               pltpu.VMEM((r_all, 1), F32), pltpu.VMEM((r_all, 1), F32), pltpu.VMEM((r_all, c), F32)]
    if mode == "fox":
        tri = jnp.asarray(np.triu(np.ones((PAGE, PAGE), np.float32)), dtype=BF16)
        in_specs.append(pl.BlockSpec((PAGE, PAGE), lambda i, g, pt: (0, 0)))
        args.append(tri)
        scratch += [pltpu.VMEM((2, n_pg, heads, PAGE), F32), pltpu.SemaphoreType.DMA((2,)),
                    pltpu.VMEM((heads, n_pg * PAGE), F32), pltpu.VMEM((heads, 1), F32)]
    return pl.pallas_call(
        functools.partial(_dec_paged_kernel, n_pg=n_pg, n_tok=n_tok, heads=heads, n_kv=n_kv, mode=mode),
        grid_spec=pltpu.PrefetchScalarGridSpec(
            num_scalar_prefetch=1, grid=(b, n_pages // n_pg), in_specs=in_specs,
            out_specs=pl.BlockSpec((1, r_all, HD), lambda i, g, pt: (i, 0, 0)), scratch_shapes=scratch),
        out_shape=jax.ShapeDtypeStruct((b, r_all, HD), F32),
        compiler_params=_params(("arbitrary", "arbitrary")), name="dec_paged_" + mode,
    )(page_table, *args)


def _block_diag_q(q, n_tok, heads, n_kv):
    b = q.shape[0] // n_tok
    own = jnp.asarray(np.arange(heads)[:, None] // (heads // n_kv) == np.arange(n_kv)[None, :], dtype=F32)
    qb = q.reshape(b, n_tok, heads, 1, HD) * own[None, None, :, :, None]
    return qb.reshape(b, n_tok * heads, n_kv * HD)


def _rows_first(z_t, n_kv):
    b, _, l = z_t.shape
    return z_t.reshape(b, 2, n_kv, HD, l).transpose(0, 4, 1, 2, 3)


def _rows_last(cache):
    n, rows = cache.shape[:2]
    return cache.transpose(0, 2, 3, 4, 1).reshape(n, -1, rows)


def _mixer_a(hp, hs, g, w_in, w_out, phi, caches, page_table, dims, ropes):
    bsz, seq, dec_b, dec_t, past_len = dims
    (cos_p, sin_p), (cos_s, sin_s) = ropes
    cache_cmp, cache_sel, cache_win = caches
    heads, n_kv = 16, 2
    segs = [(0, 1024, 0, 1024, True), (1024, 1280, 0, 0, False), (1280, 1536, 0, 128, False),
            (1536, 1792, 0, 128, False), (1792, 1840, 0, 0, False)]
    w1p, b1p, w2p = phi
    q_cmp, q_rot, cmp_raw, sel_kv, win_kv, gates, cmp_t, sel_t, win_t = norm_proj(
        hp, g, w_in, cos_p, sin_p, segs, 512, flipped=(1, 2, 3), seq=seq)
    n = bsz * seq
    kvc = compress(cmp_raw.reshape(bsz, seq, -1), None, w1p, b1p, w2p)
    o_cmp, flags = cmp_attend_prompt(q_cmp.reshape(bsz, seq, -1), kvc, 512, n_kv)
    q3 = q_rot.reshape(bsz, seq, -1)
    o_sel = flash_attention(q3, sel_kv.reshape(bsz, seq, -1), 512, 1024, 1, "sel", (flags,))
    o_win = band_attention(q3, win_kv.reshape(bsz, seq, -1), A_WINDOW - 1, 1, 256)
    hp = merge_a(o_cmp.reshape(n, -1), o_sel.reshape(n, -1), o_win.reshape(n, -1), gates, hp, w_out)
    kv5 = lambda z, b, t: z.reshape(b, t, 2, n_kv, HD)
    w = min(A_WINDOW, seq)
    outs_p = (_rows_first(cmp_t, n_kv), _rows_first(sel_t, n_kv), _rows_first(win_t[:, :, seq - w:], n_kv))
    q_cmp, q_rot, cmp_raw, sel_kv, win_kv, gates = norm_proj(hs, g, w_in, cos_s, sin_s, segs, 512)
    ns = dec_b * dec_t
    kvc = compress(_rows_last(cache_cmp), page_table, w1p, b1p, w2p)
    o_cmp, flags = cmp_attend_sample(_block_diag_q(q_cmp, dec_t, heads, n_kv), kvc, dec_t, heads, n_kv, past_len)
    qbd = _block_diag_q(q_rot, dec_t, heads, n_kv)
    flag_rows = jnp.repeat(flags.reshape(dec_b, dec_t, n_kv, 1, -1), heads // n_kv, axis=3).reshape(dec_b, dec_t * heads, -1)
    o_sel = dec_paged(qbd, _rows_last(cache_sel), page_table, sel_kv.reshape(dec_b, dec_t, -1), flag_rows,
                      dec_t, heads, n_kv, "sel", 32)
    o_win = dec_window(qbd, _rows_last(cache_win), win_kv.reshape(dec_b, dec_t, -1), dec_t, heads, n_kv, A_WINDOW - 1, 1)
    hs = merge_a(o_cmp.reshape(ns, -1), o_sel.reshape(ns, -1), o_win.reshape(ns, -1), gates, hs, w_out)
    outs_s = (kv5(cmp_raw, dec_b, dec_t), kv5(sel_kv, dec_b, dec_t), kv5(win_kv, dec_b, dec_t))
    return hp, hs, outs_p, outs_s


def _mixer_b(hp, hs, g, w_in, sinks, w_out, cache_win, dims, ropes):
    bsz, seq, dec_b, dec_t, _ = dims
    (cos_p, sin_p), (cos_s, sin_s) = ropes
    heads, n_kv = 16, 2
    segs = [(0, 1024, 0, 1024, False), (1024, 1280, 0, 128, False)]
    q, kv, kv_t = norm_proj(hp, g, w_in, cos_p, sin_p, segs, 512, flipped=(1,), seq=seq)
    o = band_attention(q.reshape(bsz, seq, -1), kv.reshape(bsz, seq, -1), B_WINDOW - 1, 1, 256, sink=sinks)
    hp = out_proj(o.reshape(bsz * seq, -1), hp, w_out)
    w = min(B_WINDOW, seq)
    out_p = _rows_first(kv_t[:, :, seq - w:], n_kv)
    q, kv = norm_proj(hs, g, w_in, cos_s, sin_s, segs, 512)
    sink_rows = jnp.tile(sinks.astype(F32), dec_t).reshape(dec_t * heads, 1)
    o = dec_window(_block_diag_q(q, dec_t, heads, n_kv), _rows_last(cache_win), kv.reshape(dec_b, dec_t, -1),
                   dec_t, heads, n_kv, B_WINDOW - 1, 1, sink_rows=sink_rows)
    hs = out_proj(o.reshape(dec_b * dec_t, -1), hs, w_out)
    return hp, hs, out_p, kv.reshape(dec_b, dec_t, 2, n_kv, HD)


def _mixer_c(hp, hs, g, w_in, b_f, w_out, cache_kv, cache_logf, page_table, dims, ropes):
    bsz, seq, dec_b, dec_t, _ = dims
    (cos_p, sin_p), (cos_s, sin_s) = ropes
    heads, n_kv = 16, 4
    segs = [(0, 1024, 0, 0, False), (1024, 1536, 0, 0, False), (1536, 1552, 0, 0, False)]
    q, kv, f, kv_t = norm_proj(hp, g, w_in, cos_p, sin_p, segs, 512, flipped=(1,), seq=seq)
    lf_t, cum_t = fox_gate(f.reshape(bsz, seq, heads).transpose(0, 2, 1), b_f)
    logf_p = lf_t.transpose(0, 2, 1)
    cum_q = cum_t.reshape(bsz, 1, heads, seq)
    cum_k = cum_q.transpose(0, 1, 3, 2)
    o = flash_attention(q.reshape(bsz, seq, -1), kv.reshape(bsz, seq, -1), 512, 1024, 1, "fox", (cum_q, cum_k),
                        kv_per_col=n_kv)
    hp = out_proj(o.reshape(bsz * seq, -1), hp, w_out)
    out_p = _rows_first(kv_t, n_kv)
    q, kv, f = norm_proj(hs, g, w_in, cos_s, sin_s, segs, 512)
    ns = dec_b * dec_t
    lf_t, _ = fox_gate(f.reshape(1, ns, heads).transpose(0, 2, 1), b_f)
    logf_s = lf_t.transpose(0, 2, 1).reshape(dec_b, dec_t, heads)
    o = dec_paged(_block_diag_q(q, dec_t, heads, n_kv), _rows_last(cache_kv), page_table, kv.reshape(dec_b, dec_t, -1),
                  logf_s.reshape(dec_b, dec_t * heads, 1), dec_t, heads, n_kv, "fox", 32,
                  lf_pool_t=cache_logf.transpose(0, 2, 1))
    hs = out_proj(o.reshape(ns, -1), hs, w_out)
    return hp, hs, (out_p, logf_p), (kv.reshape(dec_b, dec_t, 2, n_kv, HD), logf_s)


def _mixer_d(hp, hs, g, w_in, w_out, caches, dims, ropes):
    bsz, seq, dec_b, dec_t, _ = dims
    (cos_p, sin_p), (cos_s, sin_s) = ropes
    heads = 8
    c = heads * HD
    segs = []
    for grp in range(len(D_GROUPS)):
        base = 3 * grp * c
        segs += [(base, base + c, 0, c, False), (base + c, base + 3 * c, 0, c, False)]
    res = norm_proj(hp, g, w_in, cos_p, sin_p, segs, 256, flipped=(1, 3, 5), seq=seq)
    outs, lses, states_p = [], [], []
    for grp, (window, dil) in enumerate(D_GROUPS):
        q, kv = res[2 * grp], res[2 * grp + 1]
        if dil == 1:
            o, lse = band_attention(q.reshape(bsz, seq, c), kv.reshape(bsz, seq, 2 * c), window, 1, 256, with_lse=True)
        else:
            o, lse = dilated_attention(q.reshape(bsz, seq, c), kv.reshape(bsz, seq, 2 * c), window, dil)
        outs.append(o.reshape(bsz * seq, c))
        lses.append(lse.reshape(bsz * seq, c))
        states_p.append(_rows_first(res[6 + grp][:, :, seq - min(window, seq):], heads))
    hp = merge_d(outs, lses, hp, w_out)
    res = norm_proj(hs, g, w_in, cos_s, sin_s, segs, 256)
    outs, lses, states_s = [], [], []
    for grp, (window, dil) in enumerate(D_GROUPS):
        q, kv = res[2 * grp], res[2 * grp + 1]
        o, lse = dec_window(_block_diag_q(q, dec_t, heads, heads), _rows_last(caches[grp]),
                            kv.reshape(dec_b, dec_t, -1), dec_t, heads, heads, window, dil, with_lse=True)
        outs.append(o.reshape(dec_b * dec_t, c))
        lses.append(lse.reshape(dec_b * dec_t, c))
        states_s.append(kv.reshape(dec_b, dec_t, 2, heads, HD))
    hs = merge_d(outs, lses, hs, w_out)
    return hp, hs, states_p, states_s


def kernel(x_prompt, x_sample, cache_a_cmp, cache_a_sel, cache_a_win, cache_b_win, cache_c_kv, cache_c_logf, cache_d1, cache_d2, cache_d3, page_table, norm_mix, norm_ffn, norm_final, a_w_in, a_phi_w1, a_phi_b1, a_phi_w2, a_w_out, b_w_in, b_sinks, b_w_out, c_w_in, c_b_f, c_w_out, d_w_in, d_w_out, ffn_w_gate, ffn_w_up, ffn_w_down):
    bsz, seq, d_model = x_prompt.shape
    dec_b, dec_t, _ = x_sample.shape
    past_len = page_table.shape[1] * PAGE
    dims = (bsz, seq, dec_b, dec_t, past_len)
    ropes = (_rope_tables(jnp.arange(seq)), _rope_tables(past_len + jnp.arange(dec_b * dec_t) % dec_t))
    hp = x_prompt.reshape(bsz * seq, d_model)
    hs = x_sample.reshape(dec_b * dec_t, d_model)
    bf = lambda z: z.astype(BF16)
    phi = _compress_weights(a_phi_w1, a_phi_b1, a_phi_w2)
    depth = norm_mix.shape[0]
    for layer in range(depth):
        kind = layer % 4
        g = norm_mix[layer]
        if kind == 0:
            hp, hs, a_p, a_s = _mixer_a(hp, hs, g, bf(a_w_in), bf(a_w_out), phi,
                                        (cache_a_cmp, cache_a_sel, cache_a_win), page_table, dims, ropes)
        elif kind == 1:
            hp, hs, b_p, b_s = _mixer_b(hp, hs, g, bf(b_w_in), b_sinks, bf(b_w_out), cache_b_win, dims, ropes)
        elif kind == 2:
            hp, hs, c_p, c_s = _mixer_c(hp, hs, g, bf(c_w_in), c_b_f, bf(c_w_out), cache_c_kv, cache_c_logf,
                                        page_table, dims, ropes)
        else:
            hp, hs, d_p, d_s = _mixer_d(hp, hs, g, bf(d_w_in), bf(d_w_out), (cache_d1, cache_d2, cache_d3), dims, ropes)
        wg, wu, wd = bf(ffn_w_gate[layer]), bf(ffn_w_up[layer]), bf(ffn_w_down[layer])
        hp = ffn(hp, norm_ffn[layer], wg, wu, wd)
        hs = ffn(hs, norm_ffn[layer], wg, wu, wd)
    y_p = final_norm(hp, norm_final).reshape(bsz, seq, d_model)
    y_s = final_norm(hs, norm_final).reshape(dec_b, dec_t, d_model)
    return (y_p, y_s, a_p[0], a_s[0], a_p[1], a_s[1], a_p[2], a_s[2], b_p, b_s,
            c_p[0], c_s[0], c_p[1], c_s[1], d_p[0], d_s[0], d_p[1], d_s[1], d_p[2], d_s[2])
```
